```python
import math
import jax, jax.numpy as jnp
from jax import lax
import numpy as np

D_MODEL = 1024
BATCH = 16
SEQ = 2048
DEPTH = 4

MEM_LEN = 256
EPS = 1e-6
SSM_WIDTH = 384
SSM_GROUP = 16
SSM_GROUPS = SSM_WIDTH // SSM_GROUP
SSM_STATE = 64
DIFF_HEADS = 4
DIFF_HD = 64
DIFF_QK = DIFF_HEADS * 2 * DIFF_HD
DIFF_V = DIFF_HEADS * 2 * DIFF_HD
Q_BLOCK = 128
ROPE_THETA = 10000.0
LRU_WIDTH = 512
LRU_HEADS = 8
LRU_HD = LRU_WIDTH // LRU_HEADS
LRU_CONV = 4
LRU_C = 8.0
N_BRANCH = 3
IN_WIDTHS = (SSM_WIDTH, DIFF_QK, DIFF_QK, DIFF_V, LRU_WIDTH, LRU_WIDTH, N_BRANCH * D_MODEL)
D_IN = SSM_WIDTH + 2 * DIFF_QK + DIFF_V + 2 * LRU_WIDTH + N_BRANCH * D_MODEL
XATTN_HEADS = 4
XATTN_HD = D_MODEL // XATTN_HEADS
D_FF = 2816
FFN_CONV = 3

kernel_name = "hybrid_s5_diffattn_rglru_convffn"


def rmsnorm(x, g):
    x32 = x.astype(jnp.float32)
    y = x32 * lax.rsqrt(jnp.mean(x32 * x32, axis=-1, keepdims=True) + EPS)
    return (y * g.astype(jnp.float32)).astype(x.dtype)


def causal_dwconv(x, w, b):
    k_w = w.shape[0]
    L = x.shape[1]
    xp = jnp.pad(x, ((0, 0), (k_w - 1, 0), (0, 0)))
    y = xp[:, 0:L] * w[0]
    for j in range(1, k_w):
        y = y + xp[:, j:j + L] * w[j]
    return y + b


def rope_tables(positions):
    inv = jnp.exp(-math.log(ROPE_THETA) * jnp.arange(0, DIFF_HD, 2, dtype=jnp.float32) / DIFF_HD)
    ang = positions.astype(jnp.float32)[..., None] * inv
    return jnp.cos(ang)[:, :, None, None, :], jnp.sin(ang)[:, :, None, None, :]


def apply_rope(x, cos, sin):
    half = x.shape[-1] // 2
    x1, x2 = x[..., :half], x[..., half:]
    c = cos.astype(x.dtype)
    s = sin.astype(x.dtype)
    return jnp.concatenate([x1 * c - x2 * s, x2 * c + x1 * s], axis=-1)


def _complex_combine(e1, e2):
    a1r, a1i, b1r, b1i = e1
    a2r, a2i, b2r, b2i = e2
    return (a2r * a1r - a2i * a1i,
            a2r * a1i + a2i * a1r,
            a2r * b1r - a2i * b1i + b2r,
            a2r * b1i + a2i * b1r + b2i)


def _real_combine(e1, e2):
    a1, b1 = e1
    a2, b2 = e2
    return a2 * a1, a2 * b1 + b2


def s5_branch(u, lam_re, lam_im, log_step, b_re, b_im, c_re, c_im, d_skip, w_glu, b_glu):
    f32 = jnp.float32
    Bsz, L, _ = u.shape
    u32 = u.astype(f32)
    ug = u32.reshape(Bsz, L, SSM_GROUPS, SSM_GROUP)
    lr = lam_re.astype(f32)
    li = lam_im.astype(f32)
    dt = jnp.exp(log_step.astype(f32))[:, None]
    mag = jnp.exp(lr * dt)
    ab_r = mag * jnp.cos(li * dt)
    ab_i = mag * jnp.sin(li * dt)
    den = lr * lr + li * li
    nr = ab_r - 1.0
    f_r = (nr * lr + ab_i * li) / den
    f_i = (ab_i * lr - nr * li) / den
    br = b_re.astype(f32)
    bi = b_im.astype(f32)
    bb_r = f_r[..., None] * br - f_i[..., None] * bi
    bb_i = f_r[..., None] * bi + f_i[..., None] * br
    bu_r = jnp.einsum('gph,blgh->blgp', bb_r, ug)
    bu_i = jnp.einsum('gph,blgh->blgp', bb_i, ug)
    a_r = jnp.broadcast_to(ab_r, (1, L, SSM_GROUPS, SSM_STATE))
    a_i = jnp.broadcast_to(ab_i, (1, L, SSM_GROUPS, SSM_STATE))
    _, _, s_r, s_i = lax.associative_scan(_complex_combine, (a_r, a_i, bu_r, bu_i), axis=1)
    y = (jnp.einsum('ghp,blgp->blgh', c_re.astype(f32), s_r)
         - jnp.einsum('ghp,blgp->blgh', c_im.astype(f32), s_i))
    y = y.reshape(Bsz, L, SSM_WIDTH) + d_skip.astype(f32) * u32
    y = jax.nn.gelu(y)
    z = y @ w_glu.astype(f32) + b_glu.astype(f32)
    out = z[..., :SSM_WIDTH] * jax.nn.sigmoid(z[..., SSM_WIDTH:])
    return out.astype(u.dtype)


def diff_attention_branch(q, k, v, cos, sin, lq1, lk1, lq2, lk2, subln_g, lambda_init):
    Bsz, L, _ = q.shape
    q = apply_rope(q.reshape(Bsz, L, DIFF_HEADS, 2, DIFF_HD), cos, sin)
    k = apply_rope(k.reshape(Bsz, L, DIFF_HEADS, 2, DIFF_HD), cos, sin)
    v = v.reshape(Bsz, L, DIFF_HEADS, 2 * DIFF_HD)
    f32 = jnp.float32
    lam = (jnp.exp(jnp.sum(lq1.astype(f32) * lk1.astype(f32)))
           - jnp.exp(jnp.sum(lq2.astype(f32) * lk2.astype(f32))) + lambda_init)
    scale = DIFF_HD ** -0.5
    nb = L // Q_BLOCK
    qb = q.reshape(Bsz, nb, Q_BLOCK, DIFF_HEADS, 2, DIFF_HD).transpose(1, 0, 2, 3, 4, 5)
    key_pos = jnp.arange(L)

    def block(args):
        qblk, bidx = args
        s = jnp.einsum('bqhcd,bkhcd->bhcqk', qblk, k).astype(f32) * scale
        qpos = bidx * Q_BLOCK + jnp.arange(Q_BLOCK)
        mask = key_pos[None, :] <= qpos[:, None]
        s = jnp.where(mask, s, -jnp.inf)
        p = jax.nn.softmax(s, axis=-1)
        attn = p[:, :, 0] - lam * p[:, :, 1]
        return jnp.einsum('bhqk,bkhe->bqhe', attn.astype(v.dtype), v)

    o = lax.map(block, (qb, jnp.arange(nb)))
    o = o.transpose(1, 0, 2, 3, 4).reshape(Bsz, L, DIFF_HEADS, 2 * DIFF_HD)
    o = rmsnorm(o, subln_g) * (1.0 - lambda_init)
    return o.reshape(Bsz, L, DIFF_V)


def rglru_branch(xr, gr, conv_w, conv_b, wa, ba, wx, bx, lam):
    Bsz, L, _ = xr.shape
    f32 = jnp.float32
    xc = causal_dwconv(xr, conv_w, conv_b)
    xh = xc.reshape(Bsz, L, LRU_HEADS, LRU_HD)
    r = jax.nn.sigmoid(jnp.einsum('blhi,hij->blhj', xh, wa).reshape(Bsz, L, LRU_WIDTH) + ba)
    ig = jax.nn.sigmoid(jnp.einsum('blhi,hij->blhj', xh, wx).reshape(Bsz, L, LRU_WIDTH) + bx)
    log_a = -LRU_C * r.astype(f32) * jax.nn.softplus(-lam.astype(f32))
    a = jnp.exp(log_a)
    b = jnp.sqrt(-jnp.expm1(2.0 * log_a)) * (ig * xc).astype(f32)
    _, h = lax.associative_scan(_real_combine, (a, b), axis=1)
    return h.astype(xr.dtype) * jax.nn.gelu(gr)


def cross_attention(h, m, wq, wkv, wo):
    Bsz, L, _ = h.shape
    M = m.shape[1]
    q = (h @ wq).reshape(Bsz, L, XATTN_HEADS, XATTN_HD)
    kv = m @ wkv
    k = kv[..., :D_MODEL].reshape(Bsz, M, XATTN_HEADS, XATTN_HD)
    v = kv[..., D_MODEL:].reshape(Bsz, M, XATTN_HEADS, XATTN_HD)
    s = jnp.einsum('bqhd,bkhd->bhqk', q, k).astype(jnp.float32) * (XATTN_HD ** -0.5)
    p = jax.nn.softmax(s, axis=-1)
    o = jnp.einsum('bhqk,bkhd->bqhd', p.astype(v.dtype), v).reshape(Bsz, L, D_MODEL)
    return o @ wo


def conv_ffn(h, w_up, conv_w, conv_b, w_down):
    up = causal_dwconv(h @ w_up, conv_w, conv_b)
    val = up[..., :D_FF]
    gate = up[..., D_FF:]
    return (jax.nn.silu(gate) * val) @ w_down


def setup_inputs(seed: int = 0) -> dict:
    key = jax.random.key(seed)
    ks = iter(jax.random.split(key, 64))
    f32 = jnp.float32

    def nrm(shape, scale):
        return scale * jax.random.normal(next(ks), shape, f32)

    x = nrm((BATCH, SEQ, D_MODEL), 1.0)
    mem = nrm((BATCH, MEM_LEN, D_MODEL), 1.0)
    offs = jax.random.randint(next(ks), (BATCH, 1), 0, 4096, dtype=jnp.int32)
    positions = (offs + jnp.arange(SEQ, dtype=jnp.int32)[None, :]).astype(jnp.int32)
    norm_mix_g = 1.0 + nrm((DEPTH, D_MODEL), 0.05)
    w_in = nrm((DEPTH, D_MODEL, D_IN), D_MODEL ** -0.5)
    ssm_lambda_re = -0.5 + nrm((DEPTH, SSM_GROUPS, SSM_STATE), 0.01)
    ssm_lambda_im = math.pi * jnp.arange(SSM_STATE, dtype=f32)[None, None, :] + nrm((DEPTH, SSM_GROUPS, SSM_STATE), 0.01)
    ssm_log_step = jax.random.uniform(next(ks), (DEPTH, SSM_GROUPS), f32, math.log(1e-3), math.log(1e-1))
    ssm_b_re = nrm((DEPTH, SSM_GROUPS, SSM_STATE, SSM_GROUP), (2 * SSM_GROUP) ** -0.5)
    ssm_b_im = nrm((DEPTH, SSM_GROUPS, SSM_STATE, SSM_GROUP), (2 * SSM_GROUP) ** -0.5)
    ssm_c_re = nrm((DEPTH, SSM_GROUPS, SSM_GROUP, SSM_STATE), SSM_STATE ** -0.5)
    ssm_c_im = nrm((DEPTH, SSM_GROUPS, SSM_GROUP, SSM_STATE), SSM_STATE ** -0.5)
    ssm_d = nrm((DEPTH, SSM_WIDTH), 1.0)
    ssm_w_glu = nrm((DEPTH, SSM_WIDTH, 2 * SSM_WIDTH), SSM_WIDTH ** -0.5)
    ssm_b_glu = nrm((DEPTH, 2 * SSM_WIDTH), 0.01)
    diff_lq1 = nrm((DEPTH, DIFF_HD), 0.1)
    diff_lk1 = nrm((DEPTH, DIFF_HD), 0.1)
    diff_lq2 = nrm((DEPTH, DIFF_HD), 0.1)
    diff_lk2 = nrm((DEPTH, DIFF_HD), 0.1)
    diff_subln_g = 1.0 + nrm((DEPTH, 2 * DIFF_HD), 0.05)
    lru_conv_w = nrm((DEPTH, LRU_CONV, LRU_WIDTH), LRU_CONV ** -0.5)
    lru_conv_b = nrm((DEPTH, LRU_WIDTH), 0.01)
    lru_wa = nrm((DEPTH, LRU_HEADS, LRU_HD, LRU_HD), LRU_HD ** -0.5)
    lru_ba = nrm((DEPTH, LRU_WIDTH), 0.01)
    lru_wx = nrm((DEPTH, LRU_HEADS, LRU_HD, LRU_HD), LRU_HD ** -0.5)
    lru_bx = nrm((DEPTH, LRU_WIDTH), 0.01)
    a0 = jax.random.uniform(next(ks), (DEPTH, LRU_WIDTH), f32, 0.9, 0.999)
    s0 = a0 ** (1.0 / LRU_C)
    lru_lambda = jnp.log(s0) - jnp.log1p(-s0)
    w_br_ssm = nrm((DEPTH, SSM_WIDTH, D_MODEL), SSM_WIDTH ** -0.5)
    w_br_attn = nrm((DEPTH, DIFF_V, D_MODEL), DIFF_V ** -0.5)
    w_br_lru = nrm((DEPTH, LRU_WIDTH, D_MODEL), LRU_WIDTH ** -0.5)
    w_out = nrm((DEPTH, D_MODEL, D_MODEL), D_MODEL ** -0.5)
    norm_xattn_g = 1.0 + nrm((DEPTH, D_MODEL), 0.05)
    norm_mem_g = 1.0 + nrm((DEPTH, D_MODEL), 0.05)
    xattn_wq = nrm((DEPTH, D_MODEL, D_MODEL), D_MODEL ** -0.5)
    xattn_wkv = nrm((DEPTH, D_MODEL, 2 * D_MODEL), D_MODEL ** -0.5)
    xattn_wo = nrm((DEPTH, D_MODEL, D_MODEL), D_MODEL ** -0.5)
    norm_ffn_g = 1.0 + nrm((DEPTH, D_MODEL), 0.05)
    ffn_w_up = nrm((DEPTH, D_MODEL, 2 * D_FF), D_MODEL ** -0.5)
    ffn_conv_w = nrm((DEPTH, FFN_CONV, 2 * D_FF), FFN_CONV ** -0.5)
    ffn_conv_b = nrm((DEPTH, 2 * D_FF), 0.01)
    ffn_w_down = nrm((DEPTH, D_FF, D_MODEL), D_FF ** -0.5)
    final_norm_g = 1.0 + nrm((D_MODEL,), 0.05)
    return {"x": x, "mem": mem, "positions": positions, "norm_mix_g": norm_mix_g, "w_in": w_in,
            "ssm_lambda_re": ssm_lambda_re, "ssm_lambda_im": ssm_lambda_im, "ssm_log_step": ssm_log_step,
            "ssm_b_re": ssm_b_re, "ssm_b_im": ssm_b_im, "ssm_c_re": ssm_c_re, "ssm_c_im": ssm_c_im,
            "ssm_d": ssm_d, "ssm_w_glu": ssm_w_glu, "ssm_b_glu": ssm_b_glu,
            "diff_lq1": diff_lq1, "diff_lk1": diff_lk1, "diff_lq2": diff_lq2, "diff_lk2": diff_lk2,
            "diff_subln_g": diff_subln_g, "lru_conv_w": lru_conv_w, "lru_conv_b": lru_conv_b,
            "lru_wa": lru_wa, "lru_ba": lru_ba, "lru_wx": lru_wx, "lru_bx": lru_bx, "lru_lambda": lru_lambda,
            "w_br_ssm": w_br_ssm, "w_br_attn": w_br_attn, "w_br_lru": w_br_lru, "w_out": w_out,
            "norm_xattn_g": norm_xattn_g, "norm_mem_g": norm_mem_g, "xattn_wq": xattn_wq,
            "xattn_wkv": xattn_wkv, "xattn_wo": xattn_wo, "norm_ffn_g": norm_ffn_g, "ffn_w_up": ffn_w_up,
            "ffn_conv_w": ffn_conv_w, "ffn_conv_b": ffn_conv_b, "ffn_w_down": ffn_w_down,
            "final_norm_g": final_norm_g}


def reference(x, mem, positions, norm_mix_g, w_in, ssm_lambda_re, ssm_lambda_im, ssm_log_step,
              ssm_b_re, ssm_b_im, ssm_c_re, ssm_c_im, ssm_d, ssm_w_glu, ssm_b_glu,
              diff_lq1, diff_lk1, diff_lq2, diff_lk2, diff_subln_g, lru_conv_w, lru_conv_b,
              lru_wa, lru_ba, lru_wx, lru_bx, lru_lambda, w_br_ssm, w_br_attn, w_br_lru, w_out,
              norm_xattn_g, norm_mem_g, xattn_wq, xattn_wkv, xattn_wo, norm_ffn_g, ffn_w_up,
              ffn_conv_w, ffn_conv_b, ffn_w_down, final_norm_g):
    cos, sin = rope_tables(positions)
    for l in range(DEPTH):
        hn = rmsnorm(x, norm_mix_g[l])
        proj = hn @ w_in[l]
        parts = []
        start = 0
        for w in IN_WIDTHS:
            parts.append(proj[..., start:start + w])
            start += w
        u, q, k, v, xr, gr, gates = parts
        y_ssm = s5_branch(u, ssm_lambda_re[l], ssm_lambda_im[l], ssm_log_step[l], ssm_b_re[l], ssm_b_im[l],
                          ssm_c_re[l], ssm_c_im[l], ssm_d[l], ssm_w_glu[l], ssm_b_glu[l])
        lambda_init = 0.8 - 0.6 * math.exp(-0.3 * l)
        y_att = diff_attention_branch(q, k, v, cos, sin, diff_lq1[l], diff_lk1[l], diff_lq2[l], diff_lk2[l],
                                      diff_subln_g[l], lambda_init)
        y_lru = rglru_branch(xr, gr, lru_conv_w[l], lru_conv_b[l], lru_wa[l], lru_ba[l], lru_wx[l], lru_bx[l],
                             lru_lambda[l])
        g = jax.nn.sigmoid(gates)
        merged = (g[..., :D_MODEL] * (y_ssm @ w_br_ssm[l])
                  + g[..., D_MODEL:2 * D_MODEL] * (y_att @ w_br_attn[l])
                  + g[..., 2 * D_MODEL:] * (y_lru @ w_br_lru[l]))
        x = x + merged @ w_out[l]
        x = x + cross_attention(rmsnorm(x, norm_xattn_g[l]), rmsnorm(mem, norm_mem_g[l]),
                                xattn_wq[l], xattn_wkv[l], xattn_wo[l])
        x = x + conv_ffn(rmsnorm(x, norm_ffn_g[l]), ffn_w_up[l], ffn_conv_w[l], ffn_conv_b[l], ffn_w_down[l])
    return rmsnorm(x, final_norm_g)
```

```python
import functools
import math

import jax
import jax.numpy as jnp
from jax import lax
from jax.experimental import pallas as pl
from jax.experimental.pallas import tpu as pltpu

F32 = jnp.float32
BF16 = jnp.bfloat16

EPS = 1e-6
D_MODEL = 1024
SSM_WIDTH = 384
SSM_GROUP = 16
SSM_GROUPS = 24
SSM_STATE = 64
SSM_NSTATE = SSM_GROUPS * SSM_STATE
SSM_LANE_BLOCKS = 3
DIFF_HEADS = 4
DIFF_HD = 64
DIFF_W = 512
ROPE_THETA = 10000.0
LRU_WIDTH = 512
LRU_HEADS = 8
LRU_HD = 64
LRU_CONV = 4
LRU_C = 8.0
XATTN_HEADS = 4
XATTN_HD = 256
D_FF = 2816
FFN_CONV = 3
FFN_CHUNK = 256
FFN_NCHUNK = D_FF // FFN_CHUNK
OFF_U, OFF_Q, OFF_K, OFF_V, OFF_XR, OFF_GR, OFF_G = 0, 384, 896, 1408, 1920, 2432, 2944
D_IN = 6016

ROW_TILE = 512
ATTN_TQ = 256
VMEM_LIMIT = 56 * 1024 * 1024


def _dot(a, b):
    return jnp.dot(a, b, preferred_element_type=F32)


def _dot_nt(a, b):
    return lax.dot_general(a, b, (((1,), (1,)), ((), ())), preferred_element_type=F32)


def _rms(x, g):
    ms = jnp.mean(x * x, axis=-1, keepdims=True)
    return x * lax.rsqrt(ms + EPS) * g


def _sigmoid(x):
    return 1.0 / (1.0 + jnp.exp(-x))


def _gelu(x):
    return 0.5 * x * (1.0 + jnp.tanh(0.7978845608028654 * (x + 0.044715 * (x * x * x))))


def _params(*sem):
    return pltpu.CompilerParams(dimension_semantics=sem, vmem_limit_bytes=VMEM_LIMIT)


def _whole(shape):
    zeros = (0,) * len(shape)
    return pl.BlockSpec(shape, lambda *_: zeros, pipeline_mode=pl.Buffered(1))


def _x_spec(batch_major, rows, width):
    if batch_major:
        return pl.BlockSpec((None, rows, width), lambda b, i: (b, i, 0))
    return pl.BlockSpec((rows, width), lambda b, i: (i, b))


def _rope_kernel(pos_ref, cos_ref, sin_ref):
    lane = lax.broadcasted_iota(jnp.int32, (1, 128), 1)
    j = (lane & 31).astype(F32)
    inv = jnp.exp((-math.log(ROPE_THETA) * (2.0 * j)) / DIFF_HD)
    ang = pos_ref[...] * inv
    first_half = (lane & 63) < 32
    cos_ref[...] = jnp.cos(ang)
    s = jnp.sin(ang)
    sin_ref[...] = jnp.where(first_half, -s, s)


def _rope_tables(pos_f, B, S):
    rows = ROW_TILE
    spec = pl.BlockSpec((None, rows, 128), lambda b, i: (b, i, 0))
    return pl.pallas_call(
        _rope_kernel,
        grid=(B, S // rows),
        in_specs=[pl.BlockSpec((None, rows, 1), lambda b, i: (b, i, 0))],
        out_specs=[spec, spec],
        out_shape=[jax.ShapeDtypeStruct((B, S, 128), F32)] * 2,
        compiler_params=_params("parallel", "parallel"),
        name="rope_tables",
    )(pos_f)


def _s5_prep_kernel(lr_ref, li_ref, ls_ref, br_ref, bi_ref, abr_ref, abi_ref, bbr_ref, bbi_ref):
    lr = lr_ref[...]
    li = li_ref[...]
    dt = jnp.exp(ls_ref[...])
    mag = jnp.exp(lr * dt)
    ab_r = mag * jnp.cos(li * dt)
    ab_i = mag * jnp.sin(li * dt)
    den = lr * lr + li * li
    nr = ab_r - 1.0
    f_r = (nr * lr + ab_i * li) / den
    f_i = (ab_i * lr - nr * li) / den
    br = br_ref[...]
    bi = bi_ref[...]
    abr_ref[...] = ab_r
    abi_ref[...] = ab_i
    bbr_ref[...] = f_r * br - f_i * bi
    bbi_ref[...] = f_r * bi + f_i * br


def _s5_prep(lr_rep, li_rep, ls_rep, br_t, bi_t):
    depth = lr_rep.shape[0]
    spec = pl.BlockSpec((None, SSM_WIDTH, SSM_STATE), lambda l: (l, 0, 0))
    return pl.pallas_call(
        _s5_prep_kernel,
        grid=(depth,),
        in_specs=[spec, spec, pl.BlockSpec((None, SSM_WIDTH, 1), lambda l: (l, 0, 0)), spec, spec],
        out_specs=[spec] * 4,
        out_shape=[jax.ShapeDtypeStruct((depth, SSM_WIDTH, SSM_STATE), F32)] * 4,
        compiler_params=_params("parallel"),
        name="s5_prep",
    )(lr_rep, li_rep, ls_rep, br_t, bi_t)


def _rope(x, cos, sin_signed, first_half):
    back = pltpu.roll(x, 32, 1)
    fwd = pltpu.roll(x, DIFF_W - 32, 1)
    return x * cos + jnp.where(first_half, fwd, back) * sin_signed


def _inproj_kernel(x_ref, g_ref, w_ref, cos_ref, sin_ref,
                   u_ref, q_ref, k_ref, v_ref, xr_ref, gr_ref, gt_ref):
    hn = _rms(x_ref[...], g_ref[...]).astype(BF16)

    def seg(off, width):
        return _dot(hn, w_ref[:, off:off + width])

    cos = jnp.concatenate([cos_ref[...]] * 4, axis=1)
    sin = jnp.concatenate([sin_ref[...]] * 4, axis=1)
    lane = lax.broadcasted_iota(jnp.int32, (1, DIFF_W), 1)
    first_half = (lane & 63) < 32

    u_ref[...] = seg(OFF_U, SSM_WIDTH).astype(BF16)
    q_ref[...] = (_rope(seg(OFF_Q, DIFF_W), cos, sin, first_half) * (DIFF_HD ** -0.5)).astype(BF16)
    k_ref[...] = _rope(seg(OFF_K, DIFF_W), cos, sin, first_half).astype(BF16)
    v_ref[...] = seg(OFF_V, DIFF_W).astype(BF16)
    xr_ref[...] = seg(OFF_XR, LRU_WIDTH).astype(BF16)
    gr_ref[...] = seg(OFF_GR, LRU_WIDTH).astype(BF16)
    for c in range(6):
        gt_ref[:, c * 512:(c + 1) * 512] = _sigmoid(seg(OFF_G + c * 512, 512)).astype(BF16)


def _inproj(x, batch_major, g, w_in, cos, sin, B, S):
    rows = ROW_TILE

    def out(width):
        return pl.BlockSpec((rows, width), lambda b, i: (i, b))

    widths = (SSM_WIDTH, DIFF_W, DIFF_W, DIFF_W, LRU_WIDTH, LRU_WIDTH, 3 * D_MODEL)
    tab = pl.BlockSpec((None, rows, 128), lambda b, i: (b, i, 0))
    return pl.pallas_call(
        _inproj_kernel,
        grid=(B, S // rows),
        in_specs=[_x_spec(batch_major, rows, D_MODEL), _whole((1, D_MODEL)), _whole((D_MODEL, D_IN)), tab, tab],
        out_specs=[out(w) for w in widths],
        out_shape=[jax.ShapeDtypeStruct((S, B * w), BF16) for w in widths],
        compiler_params=_params("parallel", "parallel"),
        name="inproj",
    )(x, g, w_in, cos, sin)


def _s5_kernel(u_ref, wb_ref, abr_ref, abi_ref, cre_ref, cim_ref, d_ref, wglu_ref, bglu_ref,
               o_ref, sre, sim, st_re, st_im, *, batch, steps):
    @pl.when(pl.program_id(0) == 0)
    def _():
        st_re[...] = jnp.zeros_like(st_re)
        st_im[...] = jnp.zeros_like(st_im)

    u = u_ref[...]
    nb = SSM_NSTATE // SSM_LANE_BLOCKS
    for j in range(SSM_LANE_BLOCKS):
        bu = _dot(u[:, j * 128:(j + 1) * 128], wb_ref[j])
        sre[:, j * nb:(j + 1) * nb] = bu[:, :nb]
        sim[:, j * nb:(j + 1) * nb] = bu[:, nb:]

    for j in range(SSM_LANE_BLOCKS):
        cols = slice(j * nb, (j + 1) * nb)
        ar = jnp.broadcast_to(abr_ref[:, cols], (batch, nb))
        ai = jnp.broadcast_to(abi_ref[:, cols], (batch, nb))
        sr = st_re[:, cols]
        si = st_im[:, cols]
        for t in range(steps):
            rows = slice(t * batch, (t + 1) * batch)
            nr = ar * sr - ai * si + sre[rows, cols]
            ni = ar * si + ai * sr + sim[rows, cols]
            sre[rows, cols] = nr
            sim[rows, cols] = ni
            sr, si = nr, ni
        st_re[:, cols] = sr
        st_im[:, cols] = si

    ys = []
    for j in range(SSM_LANE_BLOCKS):
        cols = slice(j * nb, (j + 1) * nb)
        ys.append(_dot(sre[:, cols].astype(BF16), cre_ref[j]) - _dot(sim[:, cols].astype(BF16), cim_ref[j]))
    y = jnp.concatenate(ys, axis=1) + d_ref[...] * u.astype(F32)
    y = _gelu(y)
    z = _dot(y.astype(BF16), wglu_ref[...]) + bglu_ref[...]
    o_ref[...] = (z[:, :SSM_WIDTH] * _sigmoid(z[:, SSM_WIDTH:])).astype(BF16)


def _s5(u_flat, wb, abr, abi, cre, cim, d, wglu, bglu, B):
    nt = u_flat.shape[0]
    rows = ROW_TILE
    row = pl.BlockSpec((rows, SSM_WIDTH), lambda i: (i, 0))
    return pl.pallas_call(
        functools.partial(_s5_kernel, batch=B, steps=rows // B),
        grid=(nt // rows,),
        in_specs=[row, _whole(wb.shape), _whole(abr.shape), _whole(abi.shape), _whole(cre.shape),
                  _whole(cim.shape), _whole(d.shape), _whole(wglu.shape), _whole(bglu.shape)],
        out_specs=row,
        out_shape=jax.ShapeDtypeStruct((nt, SSM_WIDTH), BF16),
        scratch_shapes=[pltpu.VMEM((rows, SSM_NSTATE), F32), pltpu.VMEM((rows, SSM_NSTATE), F32),
                        pltpu.VMEM((B, SSM_NSTATE), F32), pltpu.VMEM((B, SSM_NSTATE), F32)],
        compiler_params=_params("arbitrary"),
        name="s5_branch",
    )(u_flat, wb, abr, abi, cre, cim, d, wglu, bglu)


def _lru_kernel(xr_ref, gr_ref, cw_ref, cb_ref, wax_ref, ba_ref, bx_ref, lam_ref,
                o_ref, ext, a_s, b_s, h_st, *, batch, steps):
    rows = batch * steps
    halo = (LRU_CONV - 1) * batch

    @pl.when(pl.program_id(0) == 0)
    def _():
        ext[0:halo, :] = jnp.zeros((halo, LRU_WIDTH), F32)
        h_st[...] = jnp.zeros_like(h_st)

    ext[halo:halo + rows, :] = xr_ref[...].astype(F32)
    xc = cb_ref[...] + cw_ref[LRU_CONV - 1:LRU_CONV, :] * ext[halo:halo + rows, :]
    for j in range(LRU_CONV - 1):
        xc = xc + cw_ref[j:j + 1, :] * ext[j * batch:j * batch + rows, :]
    ext[0:halo, :] = ext[rows:rows + halo, :]

    z = _dot(xc.astype(BF16), wax_ref[...])
    r = _sigmoid(z[:, :LRU_WIDTH] + ba_ref[...])
    ig = _sigmoid(z[:, LRU_WIDTH:] + bx_ref[...])
    softplus_neg_lam = jnp.log1p(jnp.exp(-lam_ref[...]))
    a = jnp.exp((-LRU_C) * r * softplus_neg_lam)
    a_s[...] = a
    b_s[...] = jnp.sqrt(1.0 - a * a) * (ig * xc)

    h = h_st[...]
    for t in range(steps):
        rs = slice(t * batch, (t + 1) * batch)
        h = a_s[rs, :] * h + b_s[rs, :]
        b_s[rs, :] = h
    h_st[...] = h
    o_ref[...] = (b_s[...] * _gelu(gr_ref[...].astype(F32))).astype(BF16)


def _lru(xr_flat, gr_flat, cw, cb, wax, ba, bx, lam, B):
    nt = xr_flat.shape[0]
    rows = ROW_TILE
    row = pl.BlockSpec((rows, LRU_WIDTH), lambda i: (i, 0))
    halo = (LRU_CONV - 1) * B
    return pl.pallas_call(
        functools.partial(_lru_kernel, batch=B, steps=rows // B),
        grid=(nt // rows,),
        in_specs=[row, row, _whole(cw.shape), _whole(cb.shape), _whole(wax.shape), _whole(ba.shape),
                  _whole(bx.shape), _whole(lam.shape)],
        out_specs=row,
        out_shape=jax.ShapeDtypeStruct((nt, LRU_WIDTH), BF16),
        scratch_shapes=[pltpu.VMEM((rows + halo, LRU_WIDTH), F32), pltpu.VMEM((rows, LRU_WIDTH), F32),
                        pltpu.VMEM((rows, LRU_WIDTH), F32), pltpu.VMEM((B, LRU_WIDTH), F32)],
        compiler_params=_params("arbitrary"),
        name="rglru_branch",
    )(xr_flat, gr_flat, cw, cb, wax, ba, bx, lam)


def _diff_attn_kernel(q_ref, k_ref, v_ref, lq1_ref, lk1_ref, lq2_ref, lk2_ref, g_ref, o_ref,
                      *, lambda_init, tq):
    qi = pl.program_id(2)
    q = q_ref[...]
    lane = lax.broadcasted_iota(jnp.int32, (1, 2 * DIFF_HD), 1)
    zero = jnp.zeros_like(q)
    qq = jnp.concatenate([jnp.where(lane < DIFF_HD, q, zero), jnp.where(lane >= DIFF_HD, q, zero)], axis=0)

    def step(j, carry, masked):
        m, l, acc = carry
        start = pl.multiple_of(j * tq, tq)
        kb = k_ref[pl.ds(start, tq), :]
        vb = v_ref[pl.ds(start, tq), :]
        s = _dot_nt(qq, kb)
        if masked:
            row = lax.broadcasted_iota(jnp.int32, s.shape, 0) & (tq - 1)
            col = lax.broadcasted_iota(jnp.int32, s.shape, 1)
            s = jnp.where(col <= row, s, -1e30)
        m_new = jnp.maximum(m, jnp.max(s, axis=-1, keepdims=True))
        alpha = jnp.exp(m - m_new)
        p = jnp.exp(s - m_new)
        l = alpha * l + jnp.sum(p, axis=-1, keepdims=True)
        acc = alpha * acc + _dot(p.astype(BF16), vb)
        return m_new, l, acc

    init = (jnp.full((2 * tq, 1), -1e30, F32), jnp.zeros((2 * tq, 1), F32),
            jnp.zeros((2 * tq, 2 * DIFF_HD), F32))
    carry = lax.fori_loop(0, qi, lambda j, c: step(j, c, False), init)
    _, l, acc = step(qi, carry, True)
    o = acc / l
    lam = (jnp.exp(jnp.sum(lq1_ref[...] * lk1_ref[...], axis=-1, keepdims=True))
           - jnp.exp(jnp.sum(lq2_ref[...] * lk2_ref[...], axis=-1, keepdims=True)) + lambda_init)
    o = o[:tq] - lam * o[tq:]
    o_ref[...] = (_rms(o, g_ref[...]) * (1.0 - lambda_init)).astype(BF16)


def _diff_attn(q, k, v, lq1, lk1, lq2, lk2, g, lambda_init, B, S):
    tq = ATTN_TQ
    hw = 2 * DIFF_HD
    qspec = pl.BlockSpec((tq, hw), lambda b, h, i: (i, b * DIFF_HEADS + h))
    kspec = pl.BlockSpec((S, hw), lambda b, h, i: (0, b * DIFF_HEADS + h))
    small = pl.BlockSpec((1, DIFF_HD), lambda b, h, i: (0, 0))
    return pl.pallas_call(
        functools.partial(_diff_attn_kernel, lambda_init=lambda_init, tq=tq),
        grid=(B, DIFF_HEADS, S // tq),
        in_specs=[qspec, kspec, kspec, small, small, small, small,
                  pl.BlockSpec((1, hw), lambda b, h, i: (0, 0))],
        out_specs=qspec,
        out_shape=jax.ShapeDtypeStruct((S, B * DIFF_W), BF16),
        compiler_params=_params("parallel", "parallel", "arbitrary"),
        name="diff_attn",
    )(q, k, v, lq1, lk1, lq2, lk2, g)


def _merge_kernel(x_ref, ys_ref, ya_ref, yl_ref, gt_ref, wbs_ref, wba_ref, wbl_ref, wout_ref, o_ref):
    d = D_MODEL
    m = gt_ref[:, 0:d].astype(F32) * _dot(ys_ref[...], wbs_ref[...])
    m = m + gt_ref[:, d:2 * d].astype(F32) * _dot(ya_ref[...], wba_ref[...])
    m = m + gt_ref[:, 2 * d:3 * d].astype(F32) * _dot(yl_ref[...], wbl_ref[...])
    o_ref[...] = x_ref[...] + _dot(m.astype(BF16), wout_ref[...])


def _merge(x, batch_major, ys, ya, yl, gt, wbs, wba, wbl, wout, B, S):
    rows = ROW_TILE

    def blk(width):
        return pl.BlockSpec((rows, width), lambda b, i: (i, b))

    return pl.pallas_call(
        _merge_kernel,
        grid=(B, S // rows),
        in_specs=[_x_spec(batch_major, rows, D_MODEL), blk(SSM_WIDTH), blk(DIFF_W), blk(LRU_WIDTH),
                  blk(3 * D_MODEL), _whole(wbs.shape), _whole(wba.shape), _whole(wbl.shape), _whole(wout.shape)],
        out_specs=blk(D_MODEL),
        out_shape=jax.ShapeDtypeStruct((S, B * D_MODEL), F32),
        compiler_params=_params("parallel", "parallel"),
        name="merge_outproj",
    )(x, ys, ya, yl, gt, wbs, wba, wbl, wout)


def _memkv_kernel(m_ref, g_ref, w_ref, o_ref):
    hn = _rms(m_ref[...], g_ref[...]).astype(BF16)
    o_ref[...] = _dot(hn, w_ref[...]).astype(BF16)


def _memkv(mem_rows, g, wkv):
    n = mem_rows.shape[0]
    rows = ROW_TILE
    return pl.pallas_call(
        _memkv_kernel,
        grid=(n // rows,),
        in_specs=[pl.BlockSpec((rows, D_MODEL), lambda i: (i, 0)), _whole(g.shape), _whole(wkv.shape)],
        out_specs=pl.BlockSpec((rows, 2 * D_MODEL), lambda i: (i, 0)),
        out_shape=jax.ShapeDtypeStruct((n, 2 * D_MODEL), BF16),
        compiler_params=_params("parallel"),
        name="mem_kv",
    )(mem_rows, g, wkv)


def _xattn_kernel(x_ref, g_ref, wq_ref, k_ref, v_ref, wo_ref, o_ref):
    x = x_ref[...]
    hn = _rms(x, g_ref[...]).astype(BF16)
    q = (_dot(hn, wq_ref[...]) * (XATTN_HD ** -0.5)).astype(BF16)
    outs = []
    for h in range(XATTN_HEADS):
        cols = slice(h * XATTN_HD, (h + 1) * XATTN_HD)
        s = _dot_nt(q[:, cols], k_ref[:, cols])
        p = jnp.exp(s - jnp.max(s, axis=-1, keepdims=True))
        p = p / jnp.sum(p, axis=-1, keepdims=True)
        outs.append(_dot(p.astype(BF16), v_ref[:, cols]).astype(BF16))
    o_ref[...] = x + _dot(jnp.concatenate(outs, axis=1), wo_ref[...])


def _xattn(x, g, wq, kv, wo, mem_len, B, S):
    rows = ROW_TILE
    blk = pl.BlockSpec((rows, D_MODEL), lambda b, i: (i, b))
    return pl.pallas_call(
        _xattn_kernel,
        grid=(B, S // rows),
        in_specs=[blk, _whole(g.shape), _whole(wq.shape),
                  pl.BlockSpec((mem_len, D_MODEL), lambda b, i: (b, 0)),
                  pl.BlockSpec((mem_len, D_MODEL), lambda b, i: (b, 1)),
                  _whole(wo.shape)],
        out_specs=blk,
        out_shape=jax.ShapeDtypeStruct((S, B * D_MODEL), F32),
        compiler_params=_params("parallel", "parallel"),
        name="cross_attn",
    )(x, g, wq, kv, kv, wo)


def _ffn_kernel(x_ref, g_ref, wup_ref, cw_ref, cb_ref, wdn_ref, o_ref, ext, prev, *, batch):
    rows = x_ref.shape[0]
    halo = (FFN_CONV - 1) * batch

    @pl.when(pl.program_id(0) == 0)
    def _():
        prev[...] = jnp.zeros_like(prev)

    x = x_ref[...]
    hn = _rms(x, g_ref[...]).astype(BF16)
    acc = jnp.zeros((rows, D_MODEL), F32)
    for j in range(FFN_NCHUNK):
        e = ext.at[j % 2]
        e[0:halo, :] = prev[j]
        e[halo:halo + rows, :] = _dot(hn, wup_ref[j])
        y = cb_ref[j] + cw_ref[j, FFN_CONV - 1:FFN_CONV, :] * e[halo:halo + rows, :]
        for t in range(FFN_CONV - 1):
            y = y + cw_ref[j, t:t + 1, :] * e[t * batch:t * batch + rows, :]
        prev[j] = e[rows:rows + halo, :]
        val = y[:, :FFN_CHUNK]
        gate = y[:, FFN_CHUNK:]
        act = (gate * _sigmoid(gate) * val).astype(BF16)
        acc = acc + _dot(act, wdn_ref[j])
    o_ref[...] = x + acc


def _ffn(x_flat, g, wup, cw, cb, wdn, B):
    nt = x_flat.shape[0]
    rows = ROW_TILE
    halo = (FFN_CONV - 1) * B
    row = pl.BlockSpec((rows, D_MODEL), lambda i: (i, 0))
    return pl.pallas_call(
        functools.partial(_ffn_kernel, batch=B),
        grid=(nt // rows,),
        in_specs=[row, _whole(g.shape), _whole(wup.shape), _whole(cw.shape), _whole(cb.shape), _whole(wdn.shape)],
        out_specs=row,
        out_shape=jax.ShapeDtypeStruct((nt, D_MODEL), F32),
        scratch_shapes=[pltpu.VMEM((2, rows + halo, 2 * FFN_CHUNK), F32),
                        pltpu.VMEM((FFN_NCHUNK, halo, 2 * FFN_CHUNK), F32)],
        compiler_params=_params("arbitrary"),
        name="conv_ffn",
    )(x_flat, g, wup, cw, cb, wdn)


def _final_kernel(x_ref, g_ref, o_ref):
    o_ref[...] = _rms(x_ref[...], g_ref[...])


def _final_norm(x, g, B, S):
    rows = ROW_TILE
    return pl.pallas_call(
        _final_kernel,
        grid=(B, S // rows),
        in_specs=[pl.BlockSpec((rows, D_MODEL), lambda b, i: (i, b)), _whole(g.shape)],
        out_specs=pl.BlockSpec((None, rows, D_MODEL), lambda b, i: (b, i, 0)),
        out_shape=jax.ShapeDtypeStruct((B, S, D_MODEL), F32),
        compiler_params=_params("parallel", "parallel"),
        name="final_norm",
    )(x, g)


def _row(v):
    return v.reshape(1, -1).astype(F32)


def _block_diag(blocks):
    n, r, c = blocks.shape
    eye = jnp.eye(n, dtype=blocks.dtype)
    return (blocks[:, :, None, :] * eye[:, None, :, None]).reshape(n * r, n * c)


def _s5_b_weights(bbr, bbi):
    per = SSM_GROUPS // SSM_LANE_BLOCKS
    out = []
    for j in range(SSM_LANE_BLOCKS):
        r = bbr[j * 128:(j + 1) * 128].reshape(per, SSM_GROUP, SSM_STATE)
        i = bbi[j * 128:(j + 1) * 128].reshape(per, SSM_GROUP, SSM_STATE)
        out.append(jnp.concatenate([_block_diag(r), _block_diag(i)], axis=1))
    return jnp.stack(out).astype(BF16)


def _s5_c_weights(c):
    per = SSM_GROUPS // SSM_LANE_BLOCKS
    ct = jnp.swapaxes(c, 1, 2)
    return jnp.stack([_block_diag(ct[j * per:(j + 1) * per]) for j in range(SSM_LANE_BLOCKS)]).astype(BF16)


def kernel(x, mem, positions, norm_mix_g, w_in, ssm_lambda_re, ssm_lambda_im, ssm_log_step, ssm_b_re, ssm_b_im, ssm_c_re, ssm_c_im, ssm_d, ssm_w_glu, ssm_b_glu, diff_lq1, diff_lk1, diff_lq2, diff_lk2, diff_subln_g, lru_conv_w, lru_conv_b, lru_wa, lru_ba, lru_wx, lru_bx, lru_lambda, w_br_ssm, w_br_attn, w_br_lru, w_out, norm_xattn_g, norm_mem_g, xattn_wq, xattn_wkv, xattn_wo, norm_ffn_g, ffn_w_up, ffn_conv_w, ffn_conv_b, ffn_w_down, final_norm_g):
    B, S, _ = x.shape
    depth = norm_mix_g.shape[0]
    mem_len = mem.shape[1]
    assert S % ROW_TILE == 0 and ROW_TILE % B == 0 and B % 8 == 0 and (B * mem_len) % ROW_TILE == 0

    cos, sin = _rope_tables(positions.astype(F32)[..., None], B, S)

    rep = lambda a: jnp.repeat(a, SSM_GROUP, axis=1)
    b_t = lambda a: jnp.swapaxes(a, 2, 3).reshape(depth, SSM_WIDTH, SSM_STATE)
    abr, abi, bbr, bbi = _s5_prep(rep(ssm_lambda_re), rep(ssm_lambda_im), rep(ssm_log_step[..., None]),
                                  b_t(ssm_b_re), b_t(ssm_b_im))
    mem_rows = mem.reshape(B * mem_len, D_MODEL)

    xs = x
    batch_major = True
    for l in range(depth):
        lambda_init = 0.8 - 0.6 * math.exp(-0.3 * l)
        u, q, k, v, xr, gr, gt = _inproj(xs, batch_major, _row(norm_mix_g[l]), w_in[l].astype(BF16), cos, sin, B, S)

        y_ssm = _s5(u.reshape(S * B, SSM_WIDTH), _s5_b_weights(bbr[l], bbi[l]),
                    abr[l, ::SSM_GROUP].reshape(1, SSM_NSTATE), abi[l, ::SSM_GROUP].reshape(1, SSM_NSTATE),
                    _s5_c_weights(ssm_c_re[l]), _s5_c_weights(ssm_c_im[l]), _row(ssm_d[l]),
                    ssm_w_glu[l].astype(BF16), _row(ssm_b_glu[l]), B).reshape(S, B * SSM_WIDTH)

        wax = jnp.concatenate([_block_diag(lru_wa[l]), _block_diag(lru_wx[l])], axis=1).astype(BF16)
        y_lru = _lru(xr.reshape(S * B, LRU_WIDTH), gr.reshape(S * B, LRU_WIDTH), lru_conv_w[l].astype(F32),
                     _row(lru_conv_b[l]), wax, _row(lru_ba[l]), _row(lru_bx[l]), _row(lru_lambda[l]),
                     B).reshape(S, B * LRU_WIDTH)

        y_att = _diff_attn(q, k, v, _row(diff_lq1[l]), _row(diff_lk1[l]), _row(diff_lq2[l]), _row(diff_lk2[l]),
                           _row(diff_subln_g[l]), lambda_init, B, S)

        xs = _merge(xs, batch_major, y_ssm, y_att, y_lru, gt, w_br_ssm[l].astype(BF16), w_br_attn[l].astype(BF16),
                    w_br_lru[l].astype(BF16), w_out[l].astype(BF16), B, S)
        batch_major = False

        kv = _memkv(mem_rows, _row(norm_mem_g[l]), xattn_wkv[l].astype(BF16))
        xs = _xattn(xs, _row(norm_xattn_g[l]), xattn_wq[l].astype(BF16), kv, xattn_wo[l].astype(BF16), mem_len, B, S)

        wup = ffn_w_up[l]
        wup = jnp.concatenate([wup[:, :D_FF].reshape(D_MODEL, FFN_NCHUNK, FFN_CHUNK),
                               wup[:, D_FF:].reshape(D_MODEL, FFN_NCHUNK, FFN_CHUNK)], axis=2)
        wup = jnp.swapaxes(wup, 0, 1).astype(BF16)

        def chunked(a):
            r = a.shape[0]
            a = jnp.concatenate([a[:, :D_FF].reshape(r, FFN_NCHUNK, FFN_CHUNK),
                                 a[:, D_FF:].reshape(r, FFN_NCHUNK, FFN_CHUNK)], axis=2)
            return jnp.swapaxes(a, 0, 1).astype(F32)

        xs = _ffn(xs.reshape(S * B, D_MODEL), _row(norm_ffn_g[l]), wup, chunked(ffn_conv_w[l]),
                  chunked(ffn_conv_b[l][None, :]), ffn_w_down[l].reshape(FFN_NCHUNK, FFN_CHUNK, D_MODEL).astype(BF16),
                  B).reshape(S, B * D_MODEL)

    return _final_norm(xs, _row(final_norm_g), B, S)
```

```python
import functools
import math

import jax
import jax.numpy as jnp
from jax import lax
from jax.experimental import pallas as pl
from jax.experimental.pallas import tpu as pltpu

F32 = jnp.float32
BF16 = jnp.bfloat16

EPS = 1e-6
D_MODEL = 1024
SSM_WIDTH = 384
SSM_GROUP = 16
SSM_GROUPS = 24
SSM_STATE = 64
SSM_NSTATE = SSM_GROUPS * SSM_STATE
SSM_LANE_BLOCKS = 3
DIFF_HEADS = 4
DIFF_HD = 64
DIFF_W = 512
ROPE_THETA = 10000.0
LRU_WIDTH = 512
LRU_HEADS = 8
LRU_HD = 64
LRU_CONV = 4
LRU_C = 8.0
XATTN_HEADS = 4
XATTN_HD = 256
D_FF = 2816
FFN_CONV = 3
FFN_CHUNK = 256
FFN_NCHUNK = D_FF // FFN_CHUNK
OFF_U, OFF_Q, OFF_K, OFF_V, OFF_XR, OFF_GR, OFF_G = 0, 384, 896, 1408, 1920, 2432, 2944
D_IN = 6016

ROW_TILE = 512
ATTN_TQ = 256
VMEM_LIMIT = 56 * 1024 * 1024


def _dot(a, b):
    return jnp.dot(a, b, preferred_element_type=F32)


def _dot_nt(a, b):
    return lax.dot_general(a, b, (((1,), (1,)), ((), ())), preferred_element_type=F32)


def _rms(x, g):
    ms = jnp.mean(x * x, axis=-1, keepdims=True)
    return x * lax.rsqrt(ms + EPS) * g


def _sigmoid(x):
    return 1.0 / (1.0 + jnp.exp(-x))


def _gelu(x):
    return 0.5 * x * (1.0 + jnp.tanh(0.7978845608028654 * (x + 0.044715 * (x * x * x))))


def _params(*sem):
    return pltpu.CompilerParams(dimension_semantics=sem, vmem_limit_bytes=VMEM_LIMIT)


def _whole(shape):
    zeros = (0,) * len(shape)
    return pl.BlockSpec(shape, lambda *_: zeros, pipeline_mode=pl.Buffered(1))


def _x_spec(batch_major, rows, width):
    if batch_major:
        return pl.BlockSpec((None, rows, width), lambda b, i: (b, i, 0))
    return pl.BlockSpec((rows, width), lambda b, i: (i, b))


def _rope_kernel(pos_ref, cos_ref, sin_ref):
    lane = lax.broadcasted_iota(jnp.int32, (1, 128), 1)
    j = (lane & 31).astype(F32)
    inv = jnp.exp((-math.log(ROPE_THETA) * (2.0 * j)) / DIFF_HD)
    ang = pos_ref[...] * inv
    first_half = (lane & 63) < 32
    cos_ref[...] = jnp.cos(ang)
    s = jnp.sin(ang)
    sin_ref[...] = jnp.where(first_half, -s, s)


def _rope_tables(pos_f, B, S):
    rows = ROW_TILE
    spec = pl.BlockSpec((None, rows, 128), lambda b, i: (b, i, 0))
    return pl.pallas_call(
        _rope_kernel,
        grid=(B, S // rows),
        in_specs=[pl.BlockSpec((None, rows, 1), lambda b, i: (b, i, 0))],
        out_specs=[spec, spec],
        out_shape=[jax.ShapeDtypeStruct((B, S, 128), F32)] * 2,
        compiler_params=_params("parallel", "parallel"),
        name="rope_tables",
    )(pos_f)


def _s5_prep_kernel(lr_ref, li_ref, ls_ref, br_ref, bi_ref, abr_ref, abi_ref, bbr_ref, bbi_ref):
    lr = lr_ref[...]
    li = li_ref[...]
    dt = jnp.exp(ls_ref[...])
    mag = jnp.exp(lr * dt)
    ab_r = mag * jnp.cos(li * dt)
    ab_i = mag * jnp.sin(li * dt)
    den = lr * lr + li * li
    nr = ab_r - 1.0
    f_r = (nr * lr + ab_i * li) / den
    f_i = (ab_i * lr - nr * li) / den
    br = br_ref[...]
    bi = bi_ref[...]
    abr_ref[...] = ab_r
    abi_ref[...] = ab_i
    bbr_ref[...] = f_r * br - f_i * bi
    bbi_ref[...] = f_r * bi + f_i * br


def _s5_prep(lr_rep, li_rep, ls_rep, br_t, bi_t):
    depth = lr_rep.shape[0]
    spec = pl.BlockSpec((None, SSM_WIDTH, SSM_STATE), lambda l: (l, 0, 0))
    return pl.pallas_call(
        _s5_prep_kernel,
        grid=(depth,),
        in_specs=[spec, spec, pl.BlockSpec((None, SSM_WIDTH, 1), lambda l: (l, 0, 0)), spec, spec],
        out_specs=[spec] * 4,
        out_shape=[jax.ShapeDtypeStruct((depth, SSM_WIDTH, SSM_STATE), F32)] * 4,
        compiler_params=_params("parallel"),
        name="s5_prep",
    )(lr_rep, li_rep, ls_rep, br_t, bi_t)


def _rope(x, cos, sin_signed, first_half):
    back = pltpu.roll(x, 32, 1)
    fwd = pltpu.roll(x, DIFF_W - 32, 1)
    return x * cos + jnp.where(first_half, fwd, back) * sin_signed


def _inproj_kernel(x_ref, g_ref, w_ref, cos_ref, sin_ref,
                   u_ref, q_ref, k_ref, v_ref, xr_ref, gr_ref, gt_ref):
    hn = _rms(x_ref[...], g_ref[...]).astype(BF16)

    def seg(off, width):
        return _dot(hn, w_ref[:, off:off + width])

    cos = jnp.concatenate([cos_ref[...]] * 4, axis=1)
    sin = jnp.concatenate([sin_ref[...]] * 4, axis=1)
    lane = lax.broadcasted_iota(jnp.int32, (1, DIFF_W), 1)
    first_half = (lane & 63) < 32

    u_ref[...] = seg(OFF_U, SSM_WIDTH).astype(BF16)
    q_ref[...] = (_rope(seg(OFF_Q, DIFF_W), cos, sin, first_half) * (DIFF_HD ** -0.5)).astype(BF16)
    k_ref[...] = _rope(seg(OFF_K, DIFF_W), cos, sin, first_half).astype(BF16)
    v_ref[...] = seg(OFF_V, DIFF_W).astype(BF16)
    xr_ref[...] = seg(OFF_XR, LRU_WIDTH).astype(BF16)
    gr_ref[...] = seg(OFF_GR, LRU_WIDTH).astype(BF16)
    for c in range(6):
        gt_ref[:, c * 512:(c + 1) * 512] = _sigmoid(seg(OFF_G + c * 512, 512)).astype(BF16)


def _inproj(x, batch_major, g, w_in, cos, sin, B, S):
    rows = ROW_TILE

    def out(width):
        return pl.BlockSpec((rows, width), lambda b, i: (i, b))

    widths = (SSM_WIDTH, DIFF_W, DIFF_W, DIFF_W, LRU_WIDTH, LRU_WIDTH, 3 * D_MODEL)
    tab = pl.BlockSpec((None, rows, 128), lambda b, i: (b, i, 0))
    return pl.pallas_call(
        _inproj_kernel,
        grid=(B, S // rows),
        in_specs=[_x_spec(batch_major, rows, D_MODEL), _whole((1, D_MODEL)), _whole((D_MODEL, D_IN)), tab, tab],
        out_specs=[out(w) for w in widths],
        out_shape=[jax.ShapeDtypeStruct((S, B * w), BF16) for w in widths],
        compiler_params=_params("parallel", "parallel"),
        name="inproj",
    )(x, g, w_in, cos, sin)


def _swap_matrix(batch):
    idx = jnp.arange(batch * batch)
    return (idx[:, None] == (idx[None, :] % batch) * batch + idx[None, :] // batch).astype(BF16)


def _load_time_major(ref, width, batch, steps, swap):
    subs = []
    for k in range(steps // batch):
        rows = slice(k * batch, (k + 1) * batch)
        xb = jnp.concatenate([ref[rows, b * width:(b + 1) * width] for b in range(batch)], axis=0)
        subs.append(_dot(swap, xb).astype(BF16))
    return jnp.concatenate(subs, axis=0)


def _store_batch_major(ref, y, width, batch, steps, swap):
    n = batch * batch
    for k in range(steps // batch):
        yb = _dot(swap, y[k * n:(k + 1) * n]).astype(BF16)
        for b in range(batch):
            ref[k * batch:(k + 1) * batch, b * width:(b + 1) * width] = yb[b * batch:(b + 1) * batch]


def _s5_kernel(u_ref, swap_ref, wb_ref, abr_ref, abi_ref, cre_ref, cim_ref, d_ref, wglu_ref, bglu_ref,
               o_ref, sre, sim, st_re, st_im, *, batch, steps):
    @pl.when(pl.program_id(0) == 0)
    def _():
        st_re[...] = jnp.zeros_like(st_re)
        st_im[...] = jnp.zeros_like(st_im)

    swap = swap_ref[...]
    u = _load_time_major(u_ref, SSM_WIDTH, batch, steps, swap)
    nb = SSM_NSTATE // SSM_LANE_BLOCKS
    for j in range(SSM_LANE_BLOCKS):
        bu = _dot(u[:, j * 128:(j + 1) * 128], wb_ref[j])
        sre[:, j * nb:(j + 1) * nb] = bu[:, :nb]
        sim[:, j * nb:(j + 1) * nb] = bu[:, nb:]

    for j in range(SSM_LANE_BLOCKS):
        cols = slice(j * nb, (j + 1) * nb)
        ar = jnp.broadcast_to(abr_ref[:, cols], (batch, nb))
        ai = jnp.broadcast_to(abi_ref[:, cols], (batch, nb))
        sr = st_re[:, cols]
        si = st_im[:, cols]
        for t in range(steps):
            rows = slice(t * batch, (t + 1) * batch)
            nr = ar * sr - ai * si + sre[rows, cols]
            ni = ar * si + ai * sr + sim[rows, cols]
            sre[rows, cols] = nr
            sim[rows, cols] = ni
            sr, si = nr, ni
        st_re[:, cols] = sr
        st_im[:, cols] = si

    ys = []
    for j in range(SSM_LANE_BLOCKS):
        cols = slice(j * nb, (j + 1) * nb)
        ys.append(_dot(sre[:, cols].astype(BF16), cre_ref[j]) - _dot(sim[:, cols].astype(BF16), cim_ref[j]))
    y = jnp.concatenate(ys, axis=1) + d_ref[...] * u.astype(F32)
    y = _gelu(y)
    z = _dot(y.astype(BF16), wglu_ref[...]) + bglu_ref[...]
    out = (z[:, :SSM_WIDTH] * _sigmoid(z[:, SSM_WIDTH:])).astype(BF16)
    _store_batch_major(o_ref, out, SSM_WIDTH, batch, steps, swap)


def _s5(u, swap, wb, abr, abi, cre, cim, d, wglu, bglu, B):
    S = u.shape[0]
    rows = ROW_TILE
    steps = rows // B
    blk = pl.BlockSpec((steps, B * SSM_WIDTH), lambda i: (i, 0))
    return pl.pallas_call(
        functools.partial(_s5_kernel, batch=B, steps=steps),
        grid=(S // steps,),
        in_specs=[blk, _whole(swap.shape), _whole(wb.shape), _whole(abr.shape), _whole(abi.shape),
                  _whole(cre.shape), _whole(cim.shape), _whole(d.shape), _whole(wglu.shape), _whole(bglu.shape)],
        out_specs=blk,
        out_shape=jax.ShapeDtypeStruct((S, B * SSM_WIDTH), BF16),
        scratch_shapes=[pltpu.VMEM((rows, SSM_NSTATE), F32), pltpu.VMEM((rows, SSM_NSTATE), F32),
                        pltpu.VMEM((B, SSM_NSTATE), F32), pltpu.VMEM((B, SSM_NSTATE), F32)],
        compiler_params=_params("arbitrary"),
        name="s5_branch",
    )(u, swap, wb, abr, abi, cre, cim, d, wglu, bglu)


def _lru_kernel(xr_ref, gr_ref, swap_ref, cw_ref, cb_ref, wax_ref, ba_ref, bx_ref, lam_ref,
                o_ref, ext, a_s, b_s, h_st, *, batch, steps):
    rows = batch * steps
    halo = (LRU_CONV - 1) * batch

    @pl.when(pl.program_id(0) == 0)
    def _():
        ext[0:halo, :] = jnp.zeros((halo, LRU_WIDTH), F32)
        h_st[...] = jnp.zeros_like(h_st)

    swap = swap_ref[...]
    ext[halo:halo + rows, :] = _load_time_major(xr_ref, LRU_WIDTH, batch, steps, swap).astype(F32)
    xc = cb_ref[...] + cw_ref[LRU_CONV - 1:LRU_CONV, :] * ext[halo:halo + rows, :]
    for j in range(LRU_CONV - 1):
        xc = xc + cw_ref[j:j + 1, :] * ext[j * batch:j * batch + rows, :]
    ext[0:halo, :] = ext[rows:rows + halo, :]

    z = _dot(xc.astype(BF16), wax_ref[...])
    r = _sigmoid(z[:, :LRU_WIDTH] + ba_ref[...])
    ig = _sigmoid(z[:, LRU_WIDTH:] + bx_ref[...])
    softplus_neg_lam = jnp.log1p(jnp.exp(-lam_ref[...]))
    a = jnp.exp((-LRU_C) * r * softplus_neg_lam)
    a_s[...] = a
    b_s[...] = jnp.sqrt(1.0 - a * a) * (ig * xc)

    h = h_st[...]
    for t in range(steps):
        rs = slice(t * batch, (t + 1) * batch)
        h = a_s[rs, :] * h + b_s[rs, :]
        b_s[rs, :] = h
    h_st[...] = h
    gr = _load_time_major(gr_ref, LRU_WIDTH, batch, steps, swap).astype(F32)
    _store_batch_major(o_ref, (b_s[...] * _gelu(gr)).astype(BF16), LRU_WIDTH, batch, steps, swap)


def _lru(xr, gr, swap, cw, cb, wax, ba, bx, lam, B):
    S = xr.shape[0]
    rows = ROW_TILE
    steps = rows // B
    blk = pl.BlockSpec((steps, B * LRU_WIDTH), lambda i: (i, 0))
    halo = (LRU_CONV - 1) * B
    return pl.pallas_call(
        functools.partial(_lru_kernel, batch=B, steps=steps),
        grid=(S // steps,),
        in_specs=[blk, blk, _whole(swap.shape), _whole(cw.shape), _whole(cb.shape), _whole(wax.shape),
                  _whole(ba.shape), _whole(bx.shape), _whole(lam.shape)],
        out_specs=blk,
        out_shape=jax.ShapeDtypeStruct((S, B * LRU_WIDTH), BF16),
        scratch_shapes=[pltpu.VMEM((rows + halo, LRU_WIDTH), F32), pltpu.VMEM((rows, LRU_WIDTH), F32),
                        pltpu.VMEM((rows, LRU_WIDTH), F32), pltpu.VMEM((B, LRU_WIDTH), F32)],
        compiler_params=_params("arbitrary"),
        name="rglru_branch",
    )(xr, gr, swap, cw, cb, wax, ba, bx, lam)


def _diff_attn_kernel(q_ref, k_ref, v_ref, lq1_ref, lk1_ref, lq2_ref, lk2_ref, g_ref, o_ref,
                      *, lambda_init, tq):
    qi = pl.program_id(1)
    hw = 2 * DIFF_HD
    lane = lax.broadcasted_iota(jnp.int32, (1, hw), 1)
    qqs = []
    for h in range(DIFF_HEADS):
        q = q_ref[:, h * hw:(h + 1) * hw]
        zero = jnp.zeros_like(q)
        qqs.append(jnp.concatenate([jnp.where(lane < DIFF_HD, q, zero), jnp.where(lane >= DIFF_HD, q, zero)],
                                   axis=0))

    def step(j, carry, masked):
        start = pl.multiple_of(j * tq, tq)
        out = []
        for h in range(DIFF_HEADS):
            m, l, acc = carry[h]
            kb = k_ref[pl.ds(start, tq), h * hw:(h + 1) * hw]
            vb = v_ref[pl.ds(start, tq), h * hw:(h + 1) * hw]
            s = _dot_nt(qqs[h], kb)
            if masked:
                row = lax.broadcasted_iota(jnp.int32, s.shape, 0) & (tq - 1)
                col = lax.broadcasted_iota(jnp.int32, s.shape, 1)
                s = jnp.where(col <= row, s, -1e30)
            m_new = jnp.maximum(m, jnp.max(s, axis=-1, keepdims=True))
            alpha = jnp.exp(m - m_new)
            p = jnp.exp(s - m_new)
            l = alpha * l + jnp.sum(p, axis=-1, keepdims=True)
            acc = alpha * acc + _dot(p.astype(BF16), vb)
            out.append((m_new, l, acc))
        return tuple(out)

    init = tuple((jnp.full((2 * tq, 1), -1e30, F32), jnp.zeros((2 * tq, 1), F32), jnp.zeros((2 * tq, hw), F32))
                 for _ in range(DIFF_HEADS))
    carry = lax.fori_loop(0, qi, lambda j, c: step(j, c, False), init)
    final = step(qi, carry, True)
    lam = (jnp.exp(jnp.sum(lq1_ref[...] * lk1_ref[...], axis=-1, keepdims=True))
           - jnp.exp(jnp.sum(lq2_ref[...] * lk2_ref[...], axis=-1, keepdims=True)) + lambda_init)
    for h in range(DIFF_HEADS):
        _, l, acc = final[h]
        o = acc / l
        o = o[:tq] - lam * o[tq:]
        o_ref[:, h * hw:(h + 1) * hw] = (_rms(o, g_ref[...]) * (1.0 - lambda_init)).astype(BF16)


def _diff_attn(q, k, v, lq1, lk1, lq2, lk2, g, lambda_init, B, S):
    tq = ATTN_TQ
    qspec = pl.BlockSpec((tq, DIFF_W), lambda b, i: (i, b))
    kspec = pl.BlockSpec((S, DIFF_W), lambda b, i: (0, b))
    small = pl.BlockSpec((1, DIFF_HD), lambda b, i: (0, 0))
    return pl.pallas_call(
        functools.partial(_diff_attn_kernel, lambda_init=lambda_init, tq=tq),
        grid=(B, S // tq),
        in_specs=[qspec, kspec, kspec, small, small, small, small,
                  pl.BlockSpec((1, 2 * DIFF_HD), lambda b, i: (0, 0))],
        out_specs=qspec,
        out_shape=jax.ShapeDtypeStruct((S, B * DIFF_W), BF16),
        compiler_params=_params("parallel", "arbitrary"),
        name="diff_attn",
    )(q, k, v, lq1, lk1, lq2, lk2, g)


def _merge_kernel(x_ref, ys_ref, ya_ref, yl_ref, gt_ref, wbs_ref, wba_ref, wbl_ref, wout_ref, o_ref):
    d = D_MODEL
    m = gt_ref[:, 0:d].astype(F32) * _dot(ys_ref[...], wbs_ref[...])
    m = m + gt_ref[:, d:2 * d].astype(F32) * _dot(ya_ref[...], wba_ref[...])
    m = m + gt_ref[:, 2 * d:3 * d].astype(F32) * _dot(yl_ref[...], wbl_ref[...])
    o_ref[...] = x_ref[...] + _dot(m.astype(BF16), wout_ref[...])


def _merge(x, batch_major, ys, ya, yl, gt, wbs, wba, wbl, wout, B, S):
    rows = ROW_TILE

    def blk(width):
        return pl.BlockSpec((rows, width), lambda b, i: (i, b))

    return pl.pallas_call(
        _merge_kernel,
        grid=(B, S // rows),
        in_specs=[_x_spec(batch_major, rows, D_MODEL), blk(SSM_WIDTH), blk(DIFF_W), blk(LRU_WIDTH),
                  blk(3 * D_MODEL), _whole(wbs.shape), _whole(wba.shape), _whole(wbl.shape), _whole(wout.shape)],
        out_specs=blk(D_MODEL),
        out_shape=jax.ShapeDtypeStruct((S, B * D_MODEL), F32),
        compiler_params=_params("parallel", "parallel"),
        name="merge_outproj",
    )(x, ys, ya, yl, gt, wbs, wba, wbl, wout)


def _memkv_kernel(m_ref, g_ref, w_ref, o_ref):
    hn = _rms(m_ref[...], g_ref[...]).astype(BF16)
    o_ref[...] = _dot(hn, w_ref[...]).astype(BF16)


def _memkv(mem_rows, g, wkv):
    n = mem_rows.shape[0]
    rows = ROW_TILE
    return pl.pallas_call(
        _memkv_kernel,
        grid=(n // rows,),
        in_specs=[pl.BlockSpec((rows, D_MODEL), lambda i: (i, 0)), _whole(g.shape), _whole(wkv.shape)],
        out_specs=pl.BlockSpec((rows, 2 * D_MODEL), lambda i: (i, 0)),
        out_shape=jax.ShapeDtypeStruct((n, 2 * D_MODEL), BF16),
        compiler_params=_params("parallel"),
        name="mem_kv",
    )(mem_rows, g, wkv)


def _xattn_kernel(x_ref, g_ref, wq_ref, k_ref, v_ref, wo_ref, o_ref):
    x = x_ref[...]
    hn = _rms(x, g_ref[...]).astype(BF16)
    q = (_dot(hn, wq_ref[...]) * (XATTN_HD ** -0.5)).astype(BF16)
    outs = []
    for h in range(XATTN_HEADS):
        cols = slice(h * XATTN_HD, (h + 1) * XATTN_HD)
        s = _dot_nt(q[:, cols], k_ref[:, cols])
        p = jnp.exp(s - jnp.max(s, axis=-1, keepdims=True))
        p = p / jnp.sum(p, axis=-1, keepdims=True)
        outs.append(_dot(p.astype(BF16), v_ref[:, cols]).astype(BF16))
    o_ref[...] = x + _dot(jnp.concatenate(outs, axis=1), wo_ref[...])


def _xattn(x, g, wq, kv, wo, mem_len, B, S):
    rows = ROW_TILE
    blk = pl.BlockSpec((rows, D_MODEL), lambda b, i: (i, b))
    return pl.pallas_call(
        _xattn_kernel,
        grid=(B, S // rows),
        in_specs=[blk, _whole(g.shape), _whole(wq.shape),
                  pl.BlockSpec((mem_len, D_MODEL), lambda b, i: (b, 0)),
                  pl.BlockSpec((mem_len, D_MODEL), lambda b, i: (b, 1)),
                  _whole(wo.shape)],
        out_specs=blk,
        out_shape=jax.ShapeDtypeStruct((S, B * D_MODEL), F32),
        compiler_params=_params("parallel", "parallel"),
        name="cross_attn",
    )(x, g, wq, kv, kv, wo)


FFN_HALO = 8


def _ffn_kernel(x_ref, g_ref, wup_ref, cw_ref, cb_ref, wdn_ref, o_ref, ext, prev):
    rows = x_ref.shape[0]
    halo = FFN_HALO

    @pl.when(pl.program_id(1) == 0)
    def _():
        prev[...] = jnp.zeros_like(prev)

    x = x_ref[...]
    hn = _rms(x, g_ref[...]).astype(BF16)
    acc = jnp.zeros((rows, D_MODEL), F32)
    for j in range(FFN_NCHUNK):
        e = ext.at[j % 2]
        e[0:halo, :] = prev[j]
        e[halo:halo + rows, :] = _dot(hn, wup_ref[j])
        y = cb_ref[j] + cw_ref[j, FFN_CONV - 1:FFN_CONV, :] * e[halo:halo + rows, :]
        for t in range(FFN_CONV - 1):
            back = FFN_CONV - 1 - t
            y = y + cw_ref[j, t:t + 1, :] * e[halo - back:halo - back + rows, :]
        prev[j] = e[rows:rows + halo, :]
        val = y[:, :FFN_CHUNK]
        gate = y[:, FFN_CHUNK:]
        act = (gate * _sigmoid(gate) * val).astype(BF16)
        acc = acc + _dot(act, wdn_ref[j])
    o_ref[...] = x + acc


def _ffn(x, g, wup, cw, cb, wdn, B, S):
    rows = ROW_TILE
    blk = pl.BlockSpec((rows, D_MODEL), lambda b, i: (i, b))
    return pl.pallas_call(
        _ffn_kernel,
        grid=(B, S // rows),
        in_specs=[blk, _whole(g.shape), _whole(wup.shape), _whole(cw.shape), _whole(cb.shape), _whole(wdn.shape)],
        out_specs=blk,
        out_shape=jax.ShapeDtypeStruct((S, B * D_MODEL), F32),
        scratch_shapes=[pltpu.VMEM((2, rows + FFN_HALO, 2 * FFN_CHUNK), F32),
                        pltpu.VMEM((FFN_NCHUNK, FFN_HALO, 2 * FFN_CHUNK), F32)],
        compiler_params=_params("parallel", "arbitrary"),
        name="conv_ffn",
    )(x, g, wup, cw, cb, wdn)


def _final_kernel(x_ref, g_ref, o_ref):
    o_ref[...] = _rms(x_ref[...], g_ref[...])


def _final_norm(x, g, B, S):
    rows = ROW_TILE
    return pl.pallas_call(
        _final_kernel,
        grid=(B, S // rows),
        in_specs=[pl.BlockSpec((rows, D_MODEL), lambda b, i: (i, b)), _whole(g.shape)],
        out_specs=pl.BlockSpec((None, rows, D_MODEL), lambda b, i: (b, i, 0)),
        out_shape=jax.ShapeDtypeStruct((B, S, D_MODEL), F32),
        compiler_params=_params("parallel", "parallel"),
        name="final_norm",
    )(x, g)


def _row(v):
    return v.reshape(1, -1).astype(F32)


def _block_diag(blocks):
    n, r, c = blocks.shape
    eye = jnp.eye(n, dtype=blocks.dtype)
    return (blocks[:, :, None, :] * eye[:, None, :, None]).reshape(n * r, n * c)


def _s5_b_weights(bbr, bbi):
    per = SSM_GROUPS // SSM_LANE_BLOCKS
    out = []
    for j in range(SSM_LANE_BLOCKS):
        r = bbr[j * 128:(j + 1) * 128].reshape(per, SSM_GROUP, SSM_STATE)
        i = bbi[j * 128:(j + 1) * 128].reshape(per, SSM_GROUP, SSM_STATE)
        out.append(jnp.concatenate([_block_diag(r), _block_diag(i)], axis=1))
    return jnp.stack(out).astype(BF16)


def _s5_c_weights(c):
    per = SSM_GROUPS // SSM_LANE_BLOCKS
    ct = jnp.swapaxes(c, 1, 2)
    return jnp.stack([_block_diag(ct[j * per:(j + 1) * per]) for j in range(SSM_LANE_BLOCKS)]).astype(BF16)


def kernel(x, mem, positions, norm_mix_g, w_in, ssm_lambda_re, ssm_lambda_im, ssm_log_step, ssm_b_re, ssm_b_im, ssm_c_re, ssm_c_im, ssm_d, ssm_w_glu, ssm_b_glu, diff_lq1, diff_lk1, diff_lq2, diff_lk2, diff_subln_g, lru_conv_w, lru_conv_b, lru_wa, lru_ba, lru_wx, lru_bx, lru_lambda, w_br_ssm, w_br_attn, w_br_lru, w_out, norm_xattn_g, norm_mem_g, xattn_wq, xattn_wkv, xattn_wo, norm_ffn_g, ffn_w_up, ffn_conv_w, ffn_conv_b, ffn_w_down, final_norm_g):
    B, S, _ = x.shape
    depth = norm_mix_g.shape[0]
    mem_len = mem.shape[1]
    assert S % ROW_TILE == 0 and ROW_TILE % B == 0 and B % 8 == 0 and (B * mem_len) % ROW_TILE == 0

    cos, sin = _rope_tables(positions.astype(F32)[..., None], B, S)

    rep = lambda a: jnp.repeat(a, SSM_GROUP, axis=1)
    b_t = lambda a: jnp.swapaxes(a, 2, 3).reshape(depth, SSM_WIDTH, SSM_STATE)
    abr, abi, bbr, bbi = _s5_prep(rep(ssm_lambda_re), rep(ssm_lambda_im), rep(ssm_log_step[..., None]),
                                  b_t(ssm_b_re), b_t(ssm_b_im))
    mem_rows = mem.reshape(B * mem_len, D_MODEL)
    swap = _swap_matrix(B)

    xs = x
    batch_major = True
    for l in range(depth):
        lambda_init = 0.8 - 0.6 * math.exp(-0.3 * l)
        u, q, k, v, xr, gr, gt = _inproj(xs, batch_major, _row(norm_mix_g[l]), w_in[l].astype(BF16), cos, sin, B, S)

        y_ssm = _s5(u, swap, _s5_b_weights(bbr[l], bbi[l]),
                    abr[l, ::SSM_GROUP].reshape(1, SSM_NSTATE), abi[l, ::SSM_GROUP].reshape(1, SSM_NSTATE),
                    _s5_c_weights(ssm_c_re[l]), _s5_c_weights(ssm_c_im[l]), _row(ssm_d[l]),
                    ssm_w_glu[l].astype(BF16), _row(ssm_b_glu[l]), B)

        wax = jnp.concatenate([_block_diag(lru_wa[l]), _block_diag(lru_wx[l])], axis=1).astype(BF16)
        y_lru = _lru(xr, gr, swap, lru_conv_w[l].astype(F32), _row(lru_conv_b[l]), wax, _row(lru_ba[l]),
                     _row(lru_bx[l]), _row(lru_lambda[l]), B)

        y_att = _diff_attn(q, k, v, _row(diff_lq1[l]), _row(diff_lk1[l]), _row(diff_lq2[l]), _row(diff_lk2[l]),
                           _row(diff_subln_g[l]), lambda_init, B, S)

        xs = _merge(xs, batch_major, y_ssm, y_att, y_lru, gt, w_br_ssm[l].astype(BF16), w_br_attn[l].astype(BF16),
                    w_br_lru[l].astype(BF16), w_out[l].astype(BF16), B, S)
        batch_major = False

        kv = _memkv(mem_rows, _row(norm_mem_g[l]), xattn_wkv[l].astype(BF16))
        xs = _xattn(xs, _row(norm_xattn_g[l]), xattn_wq[l].astype(BF16), kv, xattn_wo[l].astype(BF16), mem_len, B, S)

        wup = ffn_w_up[l]
        wup = jnp.concatenate([wup[:, :D_FF].reshape(D_MODEL, FFN_NCHUNK, FFN_CHUNK),
                               wup[:, D_FF:].reshape(D_MODEL, FFN_NCHUNK, FFN_CHUNK)], axis=2)
        wup = jnp.swapaxes(wup, 0, 1).astype(BF16)

        def chunked(a):
            r = a.shape[0]
            a = jnp.concatenate([a[:, :D_FF].reshape(r, FFN_NCHUNK, FFN_CHUNK),
                                 a[:, D_FF:].reshape(r, FFN_NCHUNK, FFN_CHUNK)], axis=2)
            return jnp.swapaxes(a, 0, 1).astype(F32)

        xs = _ffn(xs, _row(norm_ffn_g[l]), wup, chunked(ffn_conv_w[l]), chunked(ffn_conv_b[l][None, :]),
                  ffn_w_down[l].reshape(FFN_NCHUNK, FFN_CHUNK, D_MODEL).astype(BF16), B, S)

    return _final_norm(xs, _row(final_norm_g), B, S)
```

```python
import functools
import math

import jax
import jax.numpy as jnp
from jax import lax
from jax.experimental import pallas as pl
from jax.experimental.pallas import tpu as pltpu

F32 = jnp.float32
BF16 = jnp.bfloat16

EPS = 1e-6
D_MODEL = 1024
SSM_WIDTH = 384
SSM_GROUP = 16
SSM_GROUPS = 24
SSM_STATE = 64
SSM_NSTATE = SSM_GROUPS * SSM_STATE
SSM_LANE_BLOCKS = 3
DIFF_HEADS = 4
DIFF_HD = 64
DIFF_W = 512
ROPE_THETA = 10000.0
LRU_WIDTH = 512
LRU_HEADS = 8
LRU_HD = 64
LRU_CONV = 4
LRU_C = 8.0
XATTN_HEADS = 4
XATTN_HD = 256
D_FF = 2816
FFN_CONV = 3
FFN_CHUNK = 256
FFN_NCHUNK = D_FF // FFN_CHUNK
OFF_U, OFF_Q, OFF_K, OFF_V, OFF_XR, OFF_GR, OFF_G = 0, 384, 896, 1408, 1920, 2432, 2944
D_IN = 6016

ROW_TILE = 512
ATTN_TQ = 512
VMEM_LIMIT = 56 * 1024 * 1024


def _dot(a, b):
    return jnp.dot(a, b, preferred_element_type=F32)


def _dot_nt(a, b):
    return lax.dot_general(a, b, (((1,), (1,)), ((), ())), preferred_element_type=F32)


def _rms(x, g):
    ms = jnp.mean(x * x, axis=-1, keepdims=True)
    return x * lax.rsqrt(ms + EPS) * g


def _sigmoid(x):
    return 1.0 / (1.0 + jnp.exp(-x))


def _gelu(x):
    return 0.5 * x * (1.0 + jnp.tanh(0.7978845608028654 * (x + 0.044715 * (x * x * x))))


def _params(*sem):
    return pltpu.CompilerParams(dimension_semantics=sem, vmem_limit_bytes=VMEM_LIMIT)


def _whole(shape):
    zeros = (0,) * len(shape)
    return pl.BlockSpec(shape, lambda *_: zeros, pipeline_mode=pl.Buffered(1))


def _x_spec(batch_major, rows, width):
    if batch_major:
        return pl.BlockSpec((None, rows, width), lambda b, i: (b, i, 0))
    return pl.BlockSpec((rows, width), lambda b, i: (i, b))


def _rope_kernel(pos_ref, cos_ref, sin_ref):
    lane = lax.broadcasted_iota(jnp.int32, (1, 128), 1)
    j = (lane & 31).astype(F32)
    inv = jnp.exp((-math.log(ROPE_THETA) * (2.0 * j)) / DIFF_HD)
    ang = pos_ref[...] * inv
    first_half = (lane & 63) < 32
    cos_ref[...] = jnp.cos(ang)
    s = jnp.sin(ang)
    sin_ref[...] = jnp.where(first_half, -s, s)


def _rope_tables(pos_f, B, S):
    rows = ROW_TILE
    spec = pl.BlockSpec((None, rows, 128), lambda b, i: (b, i, 0))
    return pl.pallas_call(
        _rope_kernel,
        grid=(B, S // rows),
        in_specs=[pl.BlockSpec((None, rows, 1), lambda b, i: (b, i, 0))],
        out_specs=[spec, spec],
        out_shape=[jax.ShapeDtypeStruct((B, S, 128), F32)] * 2,
        compiler_params=_params("parallel", "parallel"),
        name="rope_tables",
    )(pos_f)


def _s5_prep_kernel(lr_ref, li_ref, ls_ref, br_ref, bi_ref, abr_ref, abi_ref, bbr_ref, bbi_ref):
    lr = lr_ref[...]
    li = li_ref[...]
    dt = jnp.exp(ls_ref[...])
    mag = jnp.exp(lr * dt)
    ab_r = mag * jnp.cos(li * dt)
    ab_i = mag * jnp.sin(li * dt)
    den = lr * lr + li * li
    nr = ab_r - 1.0
    f_r = (nr * lr + ab_i * li) / den
    f_i = (ab_i * lr - nr * li) / den
    br = br_ref[...]
    bi = bi_ref[...]
    abr_ref[...] = ab_r
    abi_ref[...] = ab_i
    bbr_ref[...] = f_r * br - f_i * bi
    bbi_ref[...] = f_r * bi + f_i * br


def _s5_prep(lr_rep, li_rep, ls_rep, br_t, bi_t):
    depth = lr_rep.shape[0]
    spec = pl.BlockSpec((None, SSM_WIDTH, SSM_STATE), lambda l: (l, 0, 0))
    return pl.pallas_call(
        _s5_prep_kernel,
        grid=(depth,),
        in_specs=[spec, spec, pl.BlockSpec((None, SSM_WIDTH, 1), lambda l: (l, 0, 0)), spec, spec],
        out_specs=[spec] * 4,
        out_shape=[jax.ShapeDtypeStruct((depth, SSM_WIDTH, SSM_STATE), F32)] * 4,
        compiler_params=_params("parallel"),
        name="s5_prep",
    )(lr_rep, li_rep, ls_rep, br_t, bi_t)


def _rope(x, cos, sin_signed, first_half):
    back = pltpu.roll(x, 32, 1)
    fwd = pltpu.roll(x, DIFF_W - 32, 1)
    return x * cos + jnp.where(first_half, fwd, back) * sin_signed


def _inproj_kernel(x_ref, g_ref, w_ref, cos_ref, sin_ref,
                   u_ref, q_ref, k_ref, v_ref, xr_ref, gr_ref, gt_ref):
    hn = _rms(x_ref[...], g_ref[...]).astype(BF16)

    def seg(off, width):
        return _dot(hn, w_ref[:, off:off + width])

    cos = jnp.concatenate([cos_ref[...]] * 4, axis=1)
    sin = jnp.concatenate([sin_ref[...]] * 4, axis=1)
    lane = lax.broadcasted_iota(jnp.int32, (1, DIFF_W), 1)
    first_half = (lane & 63) < 32

    u_ref[...] = seg(OFF_U, SSM_WIDTH).astype(BF16)
    q_ref[...] = (_rope(seg(OFF_Q, DIFF_W), cos, sin, first_half) * (DIFF_HD ** -0.5)).astype(BF16)
    k_ref[...] = _rope(seg(OFF_K, DIFF_W), cos, sin, first_half).astype(BF16)
    v_ref[...] = seg(OFF_V, DIFF_W).astype(BF16)
    xr_ref[...] = seg(OFF_XR, LRU_WIDTH).astype(BF16)
    gr_ref[...] = seg(OFF_GR, LRU_WIDTH).astype(BF16)
    for c in range(6):
        gt_ref[:, c * 512:(c + 1) * 512] = _sigmoid(seg(OFF_G + c * 512, 512)).astype(BF16)


def _inproj(x, batch_major, g, w_in, cos, sin, B, S):
    rows = ROW_TILE

    def out(width):
        return pl.BlockSpec((rows, width), lambda b, i: (i, b))

    widths = (SSM_WIDTH, DIFF_W, DIFF_W, DIFF_W, LRU_WIDTH, LRU_WIDTH, 3 * D_MODEL)
    tab = pl.BlockSpec((None, rows, 128), lambda b, i: (b, i, 0))
    return pl.pallas_call(
        _inproj_kernel,
        grid=(B, S // rows),
        in_specs=[_x_spec(batch_major, rows, D_MODEL), _whole((1, D_MODEL)), _whole((D_MODEL, D_IN)), tab, tab],
        out_specs=[out(w) for w in widths],
        out_shape=[jax.ShapeDtypeStruct((S, B * w), BF16) for w in widths],
        compiler_params=_params("parallel", "parallel"),
        name="inproj",
    )(x, g, w_in, cos, sin)


def _swap_matrix(batch):
    idx = jnp.arange(batch * batch)
    return (idx[:, None] == (idx[None, :] % batch) * batch + idx[None, :] // batch).astype(BF16)


def _load_time_major(ref, width, batch, steps, swap):
    subs = []
    for k in range(steps // batch):
        rows = slice(k * batch, (k + 1) * batch)
        xb = jnp.concatenate([ref[rows, b * width:(b + 1) * width] for b in range(batch)], axis=0)
        subs.append(_dot(swap, xb).astype(BF16))
    return jnp.concatenate(subs, axis=0)


def _store_batch_major(ref, y, width, batch, steps, swap):
    n = batch * batch
    for k in range(steps // batch):
        yb = _dot(swap, y[k * n:(k + 1) * n]).astype(BF16)
        for b in range(batch):
            ref[k * batch:(k + 1) * batch, b * width:(b + 1) * width] = yb[b * batch:(b + 1) * batch]


def _s5_kernel(u_ref, swap_ref, wb_ref, abr_ref, abi_ref, cre_ref, cim_ref, d_ref, wglu_ref, bglu_ref,
               o_ref, sre, sim, st_re, st_im, *, batch, steps):
    @pl.when(pl.program_id(0) == 0)
    def _():
        st_re[...] = jnp.zeros_like(st_re)
        st_im[...] = jnp.zeros_like(st_im)

    swap = swap_ref[...]
    u = _load_time_major(u_ref, SSM_WIDTH, batch, steps, swap)
    nb = SSM_NSTATE // SSM_LANE_BLOCKS
    for j in range(SSM_LANE_BLOCKS):
        bu = _dot(u[:, j * 128:(j + 1) * 128], wb_ref[j])
        sre[:, j * nb:(j + 1) * nb] = bu[:, :nb]
        sim[:, j * nb:(j + 1) * nb] = bu[:, nb:]

    for j in range(SSM_LANE_BLOCKS):
        cols = slice(j * nb, (j + 1) * nb)
        ar = jnp.broadcast_to(abr_ref[:, cols], (batch, nb))
        ai = jnp.broadcast_to(abi_ref[:, cols], (batch, nb))
        sr = st_re[:, cols]
        si = st_im[:, cols]
        for t in range(steps):
            rows = slice(t * batch, (t + 1) * batch)
            nr = ar * sr - ai * si + sre[rows, cols]
            ni = ar * si + ai * sr + sim[rows, cols]
            sre[rows, cols] = nr
            sim[rows, cols] = ni
            sr, si = nr, ni
        st_re[:, cols] = sr
        st_im[:, cols] = si

    ys = []
    for j in range(SSM_LANE_BLOCKS):
        cols = slice(j * nb, (j + 1) * nb)
        ys.append(_dot(sre[:, cols].astype(BF16), cre_ref[j]) - _dot(sim[:, cols].astype(BF16), cim_ref[j]))
    y = jnp.concatenate(ys, axis=1) + d_ref[...] * u.astype(F32)
    y = _gelu(y)
    z = _dot(y.astype(BF16), wglu_ref[...]) + bglu_ref[...]
    out = (z[:, :SSM_WIDTH] * _sigmoid(z[:, SSM_WIDTH:])).astype(BF16)
    _store_batch_major(o_ref, out, SSM_WIDTH, batch, steps, swap)


def _s5(u, swap, wb, abr, abi, cre, cim, d, wglu, bglu, B):
    S = u.shape[0]
    rows = ROW_TILE
    steps = rows // B
    blk = pl.BlockSpec((steps, B * SSM_WIDTH), lambda i: (i, 0))
    return pl.pallas_call(
        functools.partial(_s5_kernel, batch=B, steps=steps),
        grid=(S // steps,),
        in_specs=[blk, _whole(swap.shape), _whole(wb.shape), _whole(abr.shape), _whole(abi.shape),
                  _whole(cre.shape), _whole(cim.shape), _whole(d.shape), _whole(wglu.shape), _whole(bglu.shape)],
        out_specs=blk,
        out_shape=jax.ShapeDtypeStruct((S, B * SSM_WIDTH), BF16),
        scratch_shapes=[pltpu.VMEM((rows, SSM_NSTATE), F32), pltpu.VMEM((rows, SSM_NSTATE), F32),
                        pltpu.VMEM((B, SSM_NSTATE), F32), pltpu.VMEM((B, SSM_NSTATE), F32)],
        compiler_params=_params("arbitrary"),
        name="s5_branch",
    )(u, swap, wb, abr, abi, cre, cim, d, wglu, bglu)


def _lru_kernel(xr_ref, gr_ref, swap_ref, cw_ref, cb_ref, wax_ref, ba_ref, bx_ref, lam_ref,
                o_ref, ext, a_s, b_s, h_st, *, batch, steps):
    rows = batch * steps
    halo = (LRU_CONV - 1) * batch

    @pl.when(pl.program_id(0) == 0)
    def _():
        ext[0:halo, :] = jnp.zeros((halo, LRU_WIDTH), F32)
        h_st[...] = jnp.zeros_like(h_st)

    swap = swap_ref[...]
    ext[halo:halo + rows, :] = _load_time_major(xr_ref, LRU_WIDTH, batch, steps, swap).astype(F32)
    xc = cb_ref[...] + cw_ref[LRU_CONV - 1:LRU_CONV, :] * ext[halo:halo + rows, :]
    for j in range(LRU_CONV - 1):
        xc = xc + cw_ref[j:j + 1, :] * ext[j * batch:j * batch + rows, :]
    ext[0:halo, :] = ext[rows:rows + halo, :]

    z = _dot(xc.astype(BF16), wax_ref[...])
    r = _sigmoid(z[:, :LRU_WIDTH] + ba_ref[...])
    ig = _sigmoid(z[:, LRU_WIDTH:] + bx_ref[...])
    softplus_neg_lam = jnp.log1p(jnp.exp(-lam_ref[...]))
    a = jnp.exp((-LRU_C) * r * softplus_neg_lam)
    a_s[...] = a
    b_s[...] = jnp.sqrt(1.0 - a * a) * (ig * xc)

    h = h_st[...]
    for t in range(steps):
        rs = slice(t * batch, (t + 1) * batch)
        h = a_s[rs, :] * h + b_s[rs, :]
        b_s[rs, :] = h
    h_st[...] = h
    gr = _load_time_major(gr_ref, LRU_WIDTH, batch, steps, swap).astype(F32)
    _store_batch_major(o_ref, (b_s[...] * _gelu(gr)).astype(BF16), LRU_WIDTH, batch, steps, swap)


def _lru(xr, gr, swap, cw, cb, wax, ba, bx, lam, B):
    S = xr.shape[0]
    rows = ROW_TILE
    steps = rows // B
    blk = pl.BlockSpec((steps, B * LRU_WIDTH), lambda i: (i, 0))
    halo = (LRU_CONV - 1) * B
    return pl.pallas_call(
        functools.partial(_lru_kernel, batch=B, steps=steps),
        grid=(S // steps,),
        in_specs=[blk, blk, _whole(swap.shape), _whole(cw.shape), _whole(cb.shape), _whole(wax.shape),
                  _whole(ba.shape), _whole(bx.shape), _whole(lam.shape)],
        out_specs=blk,
        out_shape=jax.ShapeDtypeStruct((S, B * LRU_WIDTH), BF16),
        scratch_shapes=[pltpu.VMEM((rows + halo, LRU_WIDTH), F32), pltpu.VMEM((rows, LRU_WIDTH), F32),
                        pltpu.VMEM((rows, LRU_WIDTH), F32), pltpu.VMEM((B, LRU_WIDTH), F32)],
        compiler_params=_params("arbitrary"),
        name="rglru_branch",
    )(xr, gr, swap, cw, cb, wax, ba, bx, lam)


def _diff_attn_kernel(q_ref, k_ref, v_ref, lq1_ref, lk1_ref, lq2_ref, lk2_ref, g_ref, o_ref,
                      qq, m_s, acc_s, *, lambda_init, tq):
    qi = pl.program_id(1)
    hw = 2 * DIFF_HD
    lane = lax.broadcasted_iota(jnp.int32, (1, hw), 1)
    for h in range(DIFF_HEADS):
        q = q_ref[:, h * hw:(h + 1) * hw]
        zero = jnp.zeros_like(q)
        qq[h, 0:tq, :] = jnp.where(lane < DIFF_HD, q, zero)
        qq[h, tq:2 * tq, :] = jnp.where(lane >= DIFF_HD, q, zero)
    m_s[...] = jnp.full(m_s.shape, -1e30, F32)
    acc_s[...] = jnp.zeros_like(acc_s)

    def scores(j, h, masked):
        start = pl.multiple_of(j * tq, tq)
        s = _dot_nt(qq[h], k_ref[pl.ds(start, tq), h * hw:(h + 1) * hw])
        if masked:
            row = lax.broadcasted_iota(jnp.int32, s.shape, 0) & (tq - 1)
            col = lax.broadcasted_iota(jnp.int32, s.shape, 1)
            s = jnp.where(col <= row, s, -1e30)
        return s

    chunks = [slice(c * hw, (c + 1) * hw) for c in range(tq // hw)]

    def max_step(j, masked):
        for h in range(DIFF_HEADS):
            s = scores(j, h, masked)
            m = m_s[h]
            for c in chunks:
                m = jnp.maximum(m, s[:, c])
            m_s[h] = m

    def acc_step(j, masked):
        start = pl.multiple_of(j * tq, tq)
        ones = jnp.ones((tq, hw), BF16)
        for h in range(DIFF_HEADS):
            s = scores(j, h, masked)
            m = m_s[h]
            p = jnp.exp(jnp.concatenate([s[:, c] - m for c in chunks], axis=1)).astype(BF16)
            v_ext = jnp.concatenate([v_ref[pl.ds(start, tq), h * hw:(h + 1) * hw], ones], axis=1)
            acc_s[h] = acc_s[h] + _dot(p, v_ext)

    def loop(step):
        def body(j, c):
            step(j, False)
            return c
        lax.fori_loop(0, qi, body, 0)
        step(qi, True)

    loop(max_step)
    for h in range(DIFF_HEADS):
        m_s[h] = jnp.broadcast_to(jnp.max(m_s[h], axis=-1, keepdims=True), (2 * tq, hw))
    loop(acc_step)

    lam = (jnp.exp(jnp.sum(lq1_ref[...] * lk1_ref[...], axis=-1, keepdims=True))
           - jnp.exp(jnp.sum(lq2_ref[...] * lk2_ref[...], axis=-1, keepdims=True)) + lambda_init)
    for h in range(DIFF_HEADS):
        acc = acc_s[h]
        o = acc[:, :hw] / acc[:, hw:]
        o = o[:tq] - lam * o[tq:]
        o_ref[:, h * hw:(h + 1) * hw] = (_rms(o, g_ref[...]) * (1.0 - lambda_init)).astype(BF16)


def _diff_attn(q, k, v, lq1, lk1, lq2, lk2, g, lambda_init, B, S):
    tq = ATTN_TQ
    hw = 2 * DIFF_HD
    qspec = pl.BlockSpec((tq, DIFF_W), lambda b, i: (i, b))
    kspec = pl.BlockSpec((S, DIFF_W), lambda b, i: (0, b))
    small = pl.BlockSpec((1, DIFF_HD), lambda b, i: (0, 0))
    return pl.pallas_call(
        functools.partial(_diff_attn_kernel, lambda_init=lambda_init, tq=tq),
        grid=(B, S // tq),
        in_specs=[qspec, kspec, kspec, small, small, small, small,
                  pl.BlockSpec((1, hw), lambda b, i: (0, 0))],
        out_specs=qspec,
        out_shape=jax.ShapeDtypeStruct((S, B * DIFF_W), BF16),
        scratch_shapes=[pltpu.VMEM((DIFF_HEADS, 2 * tq, hw), BF16), pltpu.VMEM((DIFF_HEADS, 2 * tq, hw), F32),
                        pltpu.VMEM((DIFF_HEADS, 2 * tq, 2 * hw), F32)],
        compiler_params=_params("parallel", "arbitrary"),
        name="diff_attn",
    )(q, k, v, lq1, lk1, lq2, lk2, g)


def _merge_xattn_kernel(x_ref, ys_ref, ya_ref, yl_ref, gt_ref, wbs_ref, wba_ref, wbl_ref, wout_ref,
                        g_ref, wq_ref, k_ref, v_ref, wo_ref, o_ref):
    d = D_MODEL
    m = gt_ref[:, 0:d].astype(F32) * _dot(ys_ref[...], wbs_ref[...])
    m = m + gt_ref[:, d:2 * d].astype(F32) * _dot(ya_ref[...], wba_ref[...])
    m = m + gt_ref[:, 2 * d:3 * d].astype(F32) * _dot(yl_ref[...], wbl_ref[...])
    x = x_ref[...] + _dot(m.astype(BF16), wout_ref[...])

    hn = _rms(x, g_ref[...]).astype(BF16)
    q = (_dot(hn, wq_ref[...]) * (XATTN_HD ** -0.5)).astype(BF16)
    outs = []
    for h in range(XATTN_HEADS):
        cols = slice(h * XATTN_HD, (h + 1) * XATTN_HD)
        s = _dot_nt(q[:, cols], k_ref[:, cols])
        p = jnp.exp(s - jnp.max(s, axis=-1, keepdims=True))
        p = p / jnp.sum(p, axis=-1, keepdims=True)
        outs.append(_dot(p.astype(BF16), v_ref[:, cols]).astype(BF16))
    o_ref[...] = x + _dot(jnp.concatenate(outs, axis=1), wo_ref[...])


def _merge_xattn(x, batch_major, ys, ya, yl, gt, wbs, wba, wbl, wout, g, wq, kv, wo, mem_len, B, S):
    rows = ROW_TILE

    def blk(width):
        return pl.BlockSpec((rows, width), lambda b, i: (i, b))

    return pl.pallas_call(
        _merge_xattn_kernel,
        grid=(B, S // rows),
        in_specs=[_x_spec(batch_major, rows, D_MODEL), blk(SSM_WIDTH), blk(DIFF_W), blk(LRU_WIDTH),
                  blk(3 * D_MODEL), _whole(wbs.shape), _whole(wba.shape), _whole(wbl.shape), _whole(wout.shape),
                  _whole(g.shape), _whole(wq.shape),
                  pl.BlockSpec((mem_len, D_MODEL), lambda b, i: (b, 0)),
                  pl.BlockSpec((mem_len, D_MODEL), lambda b, i: (b, 1)),
                  _whole(wo.shape)],
        out_specs=blk(D_MODEL),
        out_shape=jax.ShapeDtypeStruct((S, B * D_MODEL), F32),
        compiler_params=_params("parallel", "parallel"),
        name="merge_xattn",
    )(x, ys, ya, yl, gt, wbs, wba, wbl, wout, g, wq, kv, kv, wo)


def _memkv_kernel(m_ref, g_ref, w_ref, o_ref):
    hn = _rms(m_ref[...], g_ref[...]).astype(BF16)
    o_ref[...] = _dot(hn, w_ref[...]).astype(BF16)


def _memkv(mem_rows, g, wkv):
    n = mem_rows.shape[0]
    rows = ROW_TILE
    return pl.pallas_call(
        _memkv_kernel,
        grid=(n // rows,),
        in_specs=[pl.BlockSpec((rows, D_MODEL), lambda i: (i, 0)), _whole(g.shape), _whole(wkv.shape)],
        out_specs=pl.BlockSpec((rows, 2 * D_MODEL), lambda i: (i, 0)),
        out_shape=jax.ShapeDtypeStruct((n, 2 * D_MODEL), BF16),
        compiler_params=_params("parallel"),
        name="mem_kv",
    )(mem_rows, g, wkv)


def _ffn_kernel(x_ref, swap_ref, g_ref, wup_ref, cw_ref, cb_ref, wdn_ref, gf_ref, o_ref, prev,
                *, batch, steps, final):
    rows = batch * steps
    d = D_MODEL
    halo = (FFN_CONV - 1) * batch
    n = batch * batch

    @pl.when(pl.program_id(0) == 0)
    def _():
        prev[...] = jnp.zeros_like(prev)

    swap = swap_ref[...]
    g = g_ref[...]
    hn_b = [_rms(x_ref[:, b * d:(b + 1) * d], g).astype(BF16) for b in range(batch)]
    subs = []
    for k in range(steps // batch):
        hb = jnp.concatenate([h[k * batch:(k + 1) * batch] for h in hn_b], axis=0)
        subs.append(_dot(swap, hb).astype(BF16))
    hn = jnp.concatenate(subs, axis=0)

    def conv(up, part, cols):
        hist = prev[part]
        y = cb_ref[:, cols] + cw_ref[FFN_CONV - 1:FFN_CONV, cols] * up
        for t in range(FFN_CONV - 1):
            back = (FFN_CONV - 1 - t) * batch
            shifted = jnp.concatenate([hist[halo - back:], up[:rows - back]], axis=0)
            y = y + cw_ref[t:t + 1, cols] * shifted
        prev[part] = up[rows - halo:]
        return y

    acc = jnp.zeros((rows, d), F32)
    for j in range(FFN_NCHUNK):
        vc = slice(j * FFN_CHUNK, (j + 1) * FFN_CHUNK)
        gc = slice(D_FF + j * FFN_CHUNK, D_FF + (j + 1) * FFN_CHUNK)
        val = conv(_dot(hn, wup_ref[:, vc]), 2 * j, vc)
        gate = conv(_dot(hn, wup_ref[:, gc]), 2 * j + 1, gc)
        act = (gate * _sigmoid(gate) * val).astype(BF16)
        acc = acc + _dot(act, wdn_ref[vc, :])

    hi = acc.astype(BF16)
    lo = (acc - hi.astype(F32)).astype(BF16)
    for k in range(steps // batch):
        yb = _dot(swap, hi[k * n:(k + 1) * n]) + _dot(swap, lo[k * n:(k + 1) * n])
        ts = slice(k * batch, (k + 1) * batch)
        for b in range(batch):
            out = x_ref[ts, b * d:(b + 1) * d] + yb[b * batch:(b + 1) * batch]
            if final:
                o_ref[b, ts, :] = _rms(out, gf_ref[...])
            else:
                o_ref[ts, b * d:(b + 1) * d] = out


def _ffn(x, swap, g, wup, cw, cb, wdn, gf, B, S, final):
    rows = ROW_TILE
    steps = rows // B
    blk = pl.BlockSpec((steps, B * D_MODEL), lambda i: (i, 0))
    if final:
        out_spec = pl.BlockSpec((B, steps, D_MODEL), lambda i: (0, i, 0))
        out_shape = jax.ShapeDtypeStruct((B, S, D_MODEL), F32)
    else:
        out_spec, out_shape = blk, jax.ShapeDtypeStruct((S, B * D_MODEL), F32)
    return pl.pallas_call(
        functools.partial(_ffn_kernel, batch=B, steps=steps, final=final),
        grid=(S // steps,),
        in_specs=[blk, _whole(swap.shape), _whole(g.shape), _whole(wup.shape), _whole(cw.shape), _whole(cb.shape),
                  _whole(wdn.shape), _whole(gf.shape)],
        out_specs=out_spec,
        out_shape=out_shape,
        scratch_shapes=[pltpu.VMEM((2 * FFN_NCHUNK, (FFN_CONV - 1) * B, FFN_CHUNK), F32)],
        compiler_params=_params("arbitrary"),
        name="conv_ffn",
    )(x, swap, g, wup, cw, cb, wdn, gf)


def _row(v):
    return v.reshape(1, -1).astype(F32)


def _block_diag(blocks):
    n, r, c = blocks.shape
    eye = jnp.eye(n, dtype=blocks.dtype)
    return (blocks[:, :, None, :] * eye[:, None, :, None]).reshape(n * r, n * c)


def _s5_b_weights(bbr, bbi):
    per = SSM_GROUPS // SSM_LANE_BLOCKS
    out = []
    for j in range(SSM_LANE_BLOCKS):
        r = bbr[j * 128:(j + 1) * 128].reshape(per, SSM_GROUP, SSM_STATE)
        i = bbi[j * 128:(j + 1) * 128].reshape(per, SSM_GROUP, SSM_STATE)
        out.append(jnp.concatenate([_block_diag(r), _block_diag(i)], axis=1))
    return jnp.stack(out).astype(BF16)


def _s5_c_weights(c):
    per = SSM_GROUPS // SSM_LANE_BLOCKS
    ct = jnp.swapaxes(c, 1, 2)
    return jnp.stack([_block_diag(ct[j * per:(j + 1) * per]) for j in range(SSM_LANE_BLOCKS)]).astype(BF16)


def kernel(x, mem, positions, norm_mix_g, w_in, ssm_lambda_re, ssm_lambda_im, ssm_log_step, ssm_b_re, ssm_b_im, ssm_c_re, ssm_c_im, ssm_d, ssm_w_glu, ssm_b_glu, diff_lq1, diff_lk1, diff_lq2, diff_lk2, diff_subln_g, lru_conv_w, lru_conv_b, lru_wa, lru_ba, lru_wx, lru_bx, lru_lambda, w_br_ssm, w_br_attn, w_br_lru, w_out, norm_xattn_g, norm_mem_g, xattn_wq, xattn_wkv, xattn_wo, norm_ffn_g, ffn_w_up, ffn_conv_w, ffn_conv_b, ffn_w_down, final_norm_g):
    B, S, _ = x.shape
    depth = norm_mix_g.shape[0]
    mem_len = mem.shape[1]
    assert S % ROW_TILE == 0 and ROW_TILE % B == 0 and B % 8 == 0 and (B * mem_len) % ROW_TILE == 0

    cos, sin = _rope_tables(positions.astype(F32)[..., None], B, S)

    rep = lambda a: jnp.repeat(a, SSM_GROUP, axis=1)
    b_t = lambda a: jnp.swapaxes(a, 2, 3).reshape(depth, SSM_WIDTH, SSM_STATE)
    abr, abi, bbr, bbi = _s5_prep(rep(ssm_lambda_re), rep(ssm_lambda_im), rep(ssm_log_step[..., None]),
                                  b_t(ssm_b_re), b_t(ssm_b_im))
    mem_rows = mem.reshape(B * mem_len, D_MODEL)
    swap = _swap_matrix(B)

    xs = x
    batch_major = True
    for l in range(depth):
        lambda_init = 0.8 - 0.6 * math.exp(-0.3 * l)
        u, q, k, v, xr, gr, gt = _inproj(xs, batch_major, _row(norm_mix_g[l]), w_in[l].astype(BF16), cos, sin, B, S)

        y_ssm = _s5(u, swap, _s5_b_weights(bbr[l], bbi[l]),
                    abr[l, ::SSM_GROUP].reshape(1, SSM_NSTATE), abi[l, ::SSM_GROUP].reshape(1, SSM_NSTATE),
                    _s5_c_weights(ssm_c_re[l]), _s5_c_weights(ssm_c_im[l]), _row(ssm_d[l]),
                    ssm_w_glu[l].astype(BF16), _row(ssm_b_glu[l]), B)

        wax = jnp.concatenate([_block_diag(lru_wa[l]), _block_diag(lru_wx[l])], axis=1).astype(BF16)
        y_lru = _lru(xr, gr, swap, lru_conv_w[l].astype(F32), _row(lru_conv_b[l]), wax, _row(lru_ba[l]),
                     _row(lru_bx[l]), _row(lru_lambda[l]), B)

        y_att = _diff_attn(q, k, v, _row(diff_lq1[l]), _row(diff_lk1[l]), _row(diff_lq2[l]), _row(diff_lk2[l]),
                           _row(diff_subln_g[l]), lambda_init, B, S)

        kv = _memkv(mem_rows, _row(norm_mem_g[l]), xattn_wkv[l].astype(BF16))
        xs = _merge_xattn(xs, batch_major, y_ssm, y_att, y_lru, gt, w_br_ssm[l].astype(BF16),
                          w_br_attn[l].astype(BF16), w_br_lru[l].astype(BF16), w_out[l].astype(BF16),
                          _row(norm_xattn_g[l]), xattn_wq[l].astype(BF16), kv, xattn_wo[l].astype(BF16), mem_len, B, S)
        batch_major = False

        xs = _ffn(xs, swap, _row(norm_ffn_g[l]), ffn_w_up[l].astype(BF16), ffn_conv_w[l].astype(F32),
                  _row(ffn_conv_b[l]), ffn_w_down[l].astype(BF16), _row(final_norm_g), B, S, final=l == depth - 1)

    return xs
```

```python
import collections
import functools
import math

import jax
import jax.numpy as jnp
from jax import lax
from jax.experimental import pallas as pl
from jax.experimental.pallas import tpu as pltpu

F32 = jnp.float32
BF16 = jnp.bfloat16

EPS = 1e-6
D_MODEL = 1024
SSM_WIDTH = 384
SSM_GROUP = 16
SSM_GROUPS = 24
SSM_STATE = 64
SSM_NSTATE = SSM_GROUPS * SSM_STATE
SSM_LANE_BLOCKS = 3
DIFF_HEADS = 4
DIFF_HD = 64
DIFF_W = 512
ROPE_THETA = 10000.0
LRU_WIDTH = 512
LRU_HEADS = 8
LRU_HD = 64
LRU_CONV = 4
LRU_C = 8.0
XATTN_HEADS = 4
XATTN_HD = 256
D_FF = 2816
FFN_CONV = 3
FFN_CHUNK = 256
FFN_NCHUNK = D_FF // FFN_CHUNK
OFF_U, OFF_Q, OFF_K, OFF_V, OFF_XR, OFF_GR, OFF_G = 0, 384, 896, 1408, 1920, 2432, 2944
D_IN = 6016

ROW_TILE = 512
ATTN_TQ = 512
ATTN_MIN_ROW_SUM = 1e-25
VMEM_LIMIT = 56 * 1024 * 1024


def _dot(a, b):
    return jnp.dot(a, b, preferred_element_type=F32)


def _dot_nt(a, b):
    return lax.dot_general(a, b, (((1,), (1,)), ((), ())), preferred_element_type=F32)


def _rms(x, g):
    ms = jnp.mean(x * x, axis=-1, keepdims=True)
    return x * lax.rsqrt(ms + EPS) * g


def _sigmoid(x):
    return 1.0 / (1.0 + jnp.exp(-x))


def _gelu(x):
    return 0.5 * x * (1.0 + jnp.tanh(0.7978845608028654 * (x + 0.044715 * (x * x * x))))


def _params(*sem):
    return pltpu.CompilerParams(dimension_semantics=sem, vmem_limit_bytes=VMEM_LIMIT)


def _whole(shape):
    zeros = (0,) * len(shape)
    return pl.BlockSpec(shape, lambda *_: zeros, pipeline_mode=pl.Buffered(1))


_Layer = collections.namedtuple("_Layer", "stack index")


def _wspec(w):
    if isinstance(w, _Layer):
        shape = w.stack.shape[1:]
        idx = (w.index,) + (0,) * len(shape)
        return pl.BlockSpec((None,) + shape, lambda *_: idx, pipeline_mode=pl.Buffered(1))
    return _whole(w.shape)


def _wop(w):
    return w.stack if isinstance(w, _Layer) else w


def _x_spec(batch_major, rows, width):
    if batch_major:
        return pl.BlockSpec((None, rows, width), lambda b, i: (b, i, 0))
    return pl.BlockSpec((rows, width), lambda b, i: (i, b))


def _rope_kernel(pos_ref, cos_ref, sin_ref):
    lane = lax.broadcasted_iota(jnp.int32, (1, 128), 1)
    j = (lane & 31).astype(F32)
    inv = jnp.exp((-math.log(ROPE_THETA) * (2.0 * j)) / DIFF_HD)
    ang = pos_ref[...] * inv
    first_half = (lane & 63) < 32
    cos_ref[...] = jnp.cos(ang)
    s = jnp.sin(ang)
    sin_ref[...] = jnp.where(first_half, -s, s)


def _rope_tables(pos_f, B, S):
    rows = ROW_TILE
    spec = pl.BlockSpec((None, rows, 128), lambda b, i: (b, i, 0))
    return pl.pallas_call(
        _rope_kernel,
        grid=(B, S // rows),
        in_specs=[pl.BlockSpec((None, rows, 1), lambda b, i: (b, i, 0))],
        out_specs=[spec, spec],
        out_shape=[jax.ShapeDtypeStruct((B, S, 128), F32)] * 2,
        compiler_params=_params("parallel", "parallel"),
        name="rope_tables",
    )(pos_f)


def _s5_prep_kernel(lr_ref, li_ref, ls_ref, br_ref, bi_ref, abr_ref, abi_ref, bbr_ref, bbi_ref):
    lr = lr_ref[...]
    li = li_ref[...]
    dt = jnp.exp(ls_ref[...])
    mag = jnp.exp(lr * dt)
    ab_r = mag * jnp.cos(li * dt)
    ab_i = mag * jnp.sin(li * dt)
    den = lr * lr + li * li
    nr = ab_r - 1.0
    f_r = (nr * lr + ab_i * li) / den
    f_i = (ab_i * lr - nr * li) / den
    br = br_ref[...]
    bi = bi_ref[...]
    abr_ref[...] = ab_r
    abi_ref[...] = ab_i
    bbr_ref[...] = f_r * br - f_i * bi
    bbi_ref[...] = f_r * bi + f_i * br


def _s5_prep(lr_rep, li_rep, ls_rep, br_t, bi_t):
    depth = lr_rep.shape[0]
    spec = pl.BlockSpec((None, SSM_WIDTH, SSM_STATE), lambda l: (l, 0, 0))
    return pl.pallas_call(
        _s5_prep_kernel,
        grid=(depth,),
        in_specs=[spec, spec, pl.BlockSpec((None, SSM_WIDTH, 1), lambda l: (l, 0, 0)), spec, spec],
        out_specs=[spec] * 4,
        out_shape=[jax.ShapeDtypeStruct((depth, SSM_WIDTH, SSM_STATE), F32)] * 4,
        compiler_params=_params("parallel"),
        name="s5_prep",
    )(lr_rep, li_rep, ls_rep, br_t, bi_t)


def _rope(x, cos, sin_signed, first_half):
    back = pltpu.roll(x, 32, 1)
    fwd = pltpu.roll(x, DIFF_W - 32, 1)
    return x * cos + jnp.where(first_half, fwd, back) * sin_signed


def _inproj_kernel(x_ref, g_ref, w_ref, cos_ref, sin_ref,
                   u_ref, q_ref, k_ref, v_ref, xr_ref, gr_ref, gt_ref):
    hn = _rms(x_ref[...], g_ref[...]).astype(BF16)

    def seg(off, width):
        return _dot(hn, w_ref[:, off:off + width])

    cos = jnp.concatenate([cos_ref[...]] * 4, axis=1)
    sin = jnp.concatenate([sin_ref[...]] * 4, axis=1)
    lane = lax.broadcasted_iota(jnp.int32, (1, DIFF_W), 1)
    first_half = (lane & 63) < 32

    u_ref[...] = seg(OFF_U, SSM_WIDTH).astype(BF16)
    q_ref[...] = (_rope(seg(OFF_Q, DIFF_W), cos, sin, first_half) * (DIFF_HD ** -0.5)).astype(BF16)
    k_ref[...] = _rope(seg(OFF_K, DIFF_W), cos, sin, first_half).astype(BF16)
    v_ref[...] = seg(OFF_V, DIFF_W).astype(BF16)
    xr_ref[...] = seg(OFF_XR, LRU_WIDTH).astype(BF16)
    gr_ref[...] = seg(OFF_GR, LRU_WIDTH).astype(BF16)
    for c in range(6):
        gt_ref[:, c * 512:(c + 1) * 512] = _sigmoid(seg(OFF_G + c * 512, 512)).astype(BF16)


def _inproj(x, batch_major, g, w_in, cos, sin, B, S):
    rows = ROW_TILE

    def out(width):
        return pl.BlockSpec((rows, width), lambda b, i: (i, b))

    widths = (SSM_WIDTH, DIFF_W, DIFF_W, DIFF_W, LRU_WIDTH, LRU_WIDTH, 3 * D_MODEL)
    tab = pl.BlockSpec((None, rows, 128), lambda b, i: (b, i, 0))
    return pl.pallas_call(
        _inproj_kernel,
        grid=(B, S // rows),
        in_specs=[_x_spec(batch_major, rows, D_MODEL), _whole((1, D_MODEL)), _wspec(w_in), tab, tab],
        out_specs=[out(w) for w in widths],
        out_shape=[jax.ShapeDtypeStruct((S, B * w), BF16) for w in widths],
        compiler_params=_params("parallel", "parallel"),
        name="inproj",
    )(x, g, _wop(w_in), cos, sin)


def _swap_matrix(batch):
    idx = jnp.arange(batch * batch)
    return (idx[:, None] == (idx[None, :] % batch) * batch + idx[None, :] // batch).astype(BF16)


def _load_time_major(ref, width, batch, steps, swap):
    subs = []
    for k in range(steps // batch):
        rows = slice(k * batch, (k + 1) * batch)
        xb = jnp.concatenate([ref[rows, b * width:(b + 1) * width] for b in range(batch)], axis=0)
        subs.append(_dot(swap, xb).astype(BF16))
    return jnp.concatenate(subs, axis=0)


def _store_batch_major(ref, y, width, batch, steps, swap):
    n = batch * batch
    for k in range(steps // batch):
        yb = _dot(swap, y[k * n:(k + 1) * n]).astype(BF16)
        for b in range(batch):
            ref[k * batch:(k + 1) * batch, b * width:(b + 1) * width] = yb[b * batch:(b + 1) * batch]


def _s5_kernel(u_ref, swap_ref, wb_ref, abr_ref, abi_ref, cre_ref, cim_ref, d_ref, wglu_ref, bglu_ref,
               o_ref, sre, sim, st_re, st_im, *, batch, steps):
    @pl.when(pl.program_id(0) == 0)
    def _():
        st_re[...] = jnp.zeros_like(st_re)
        st_im[...] = jnp.zeros_like(st_im)

    swap = swap_ref[...]
    u = _load_time_major(u_ref, SSM_WIDTH, batch, steps, swap)
    nb = SSM_NSTATE // SSM_LANE_BLOCKS
    for j in range(SSM_LANE_BLOCKS):
        bu = _dot(u[:, j * 128:(j + 1) * 128], wb_ref[j])
        sre[:, j * nb:(j + 1) * nb] = bu[:, :nb]
        sim[:, j * nb:(j + 1) * nb] = bu[:, nb:]

    for j in range(SSM_LANE_BLOCKS):
        cols = slice(j * nb, (j + 1) * nb)
        ar = jnp.broadcast_to(abr_ref[:, cols], (batch, nb))
        ai = jnp.broadcast_to(abi_ref[:, cols], (batch, nb))
        sr = st_re[:, cols]
        si = st_im[:, cols]
        for t in range(steps):
            rows = slice(t * batch, (t + 1) * batch)
            nr = ar * sr - ai * si + sre[rows, cols]
            ni = ar * si + ai * sr + sim[rows, cols]
            sre[rows, cols] = nr
            sim[rows, cols] = ni
            sr, si = nr, ni
        st_re[:, cols] = sr
        st_im[:, cols] = si

    ys = []
    for j in range(SSM_LANE_BLOCKS):
        cols = slice(j * nb, (j + 1) * nb)
        ys.append(_dot(sre[:, cols].astype(BF16), cre_ref[j]) - _dot(sim[:, cols].astype(BF16), cim_ref[j]))
    y = jnp.concatenate(ys, axis=1) + d_ref[...] * u.astype(F32)
    y = _gelu(y)
    z = _dot(y.astype(BF16), wglu_ref[...]) + bglu_ref[...]
    out = (z[:, :SSM_WIDTH] * _sigmoid(z[:, SSM_WIDTH:])).astype(BF16)
    _store_batch_major(o_ref, out, SSM_WIDTH, batch, steps, swap)


def _s5(u, swap, wb, abr, abi, cre, cim, d, wglu, bglu, B):
    S = u.shape[0]
    rows = ROW_TILE
    steps = rows // B
    blk = pl.BlockSpec((steps, B * SSM_WIDTH), lambda i: (i, 0))
    return pl.pallas_call(
        functools.partial(_s5_kernel, batch=B, steps=steps),
        grid=(S // steps,),
        in_specs=[blk, _whole(swap.shape), _whole(wb.shape), _whole(abr.shape), _whole(abi.shape),
                  _whole(cre.shape), _whole(cim.shape), _whole(d.shape), _wspec(wglu), _whole(bglu.shape)],
        out_specs=blk,
        out_shape=jax.ShapeDtypeStruct((S, B * SSM_WIDTH), BF16),
        scratch_shapes=[pltpu.VMEM((rows, SSM_NSTATE), F32), pltpu.VMEM((rows, SSM_NSTATE), F32),
                        pltpu.VMEM((B, SSM_NSTATE), F32), pltpu.VMEM((B, SSM_NSTATE), F32)],
        compiler_params=_params("arbitrary"),
        name="s5_branch",
    )(u, swap, wb, abr, abi, cre, cim, d, _wop(wglu), bglu)


def _lru_kernel(xr_ref, gr_ref, swap_ref, cw_ref, cb_ref, wax_ref, ba_ref, bx_ref, lam_ref,
                o_ref, ext, a_s, b_s, h_st, *, batch, steps):
    rows = batch * steps
    halo = (LRU_CONV - 1) * batch

    @pl.when(pl.program_id(0) == 0)
    def _():
        ext[0:halo, :] = jnp.zeros((halo, LRU_WIDTH), F32)
        h_st[...] = jnp.zeros_like(h_st)

    swap = swap_ref[...]
    ext[halo:halo + rows, :] = _load_time_major(xr_ref, LRU_WIDTH, batch, steps, swap).astype(F32)
    xc = cb_ref[...] + cw_ref[LRU_CONV - 1:LRU_CONV, :] * ext[halo:halo + rows, :]
    for j in range(LRU_CONV - 1):
        xc = xc + cw_ref[j:j + 1, :] * ext[j * batch:j * batch + rows, :]
    ext[0:halo, :] = ext[rows:rows + halo, :]

    z = _dot(xc.astype(BF16), wax_ref[...])
    r = _sigmoid(z[:, :LRU_WIDTH] + ba_ref[...])
    ig = _sigmoid(z[:, LRU_WIDTH:] + bx_ref[...])
    softplus_neg_lam = jnp.log1p(jnp.exp(-lam_ref[...]))
    a = jnp.exp((-LRU_C) * r * softplus_neg_lam)
    a_s[...] = a
    y = 1.0 - a * a
    b_s[...] = jnp.where(y > 0.0, y * lax.rsqrt(y), 0.0) * (ig * xc)

    h = h_st[...]
    for t in range(steps):
        rs = slice(t * batch, (t + 1) * batch)
        h = a_s[rs, :] * h + b_s[rs, :]
        b_s[rs, :] = h
    h_st[...] = h
    gr = _load_time_major(gr_ref, LRU_WIDTH, batch, steps, swap).astype(F32)
    _store_batch_major(o_ref, (b_s[...] * _gelu(gr)).astype(BF16), LRU_WIDTH, batch, steps, swap)


def _lru(xr, gr, swap, cw, cb, wax, ba, bx, lam, B):
    S = xr.shape[0]
    rows = ROW_TILE
    steps = rows // B
    blk = pl.BlockSpec((steps, B * LRU_WIDTH), lambda i: (i, 0))
    halo = (LRU_CONV - 1) * B
    return pl.pallas_call(
        functools.partial(_lru_kernel, batch=B, steps=steps),
        grid=(S // steps,),
        in_specs=[blk, blk, _whole(swap.shape), _whole(cw.shape), _whole(cb.shape), _whole(wax.shape),
                  _whole(ba.shape), _whole(bx.shape), _whole(lam.shape)],
        out_specs=blk,
        out_shape=jax.ShapeDtypeStruct((S, B * LRU_WIDTH), BF16),
        scratch_shapes=[pltpu.VMEM((rows + halo, LRU_WIDTH), F32), pltpu.VMEM((rows, LRU_WIDTH), F32),
                        pltpu.VMEM((rows, LRU_WIDTH), F32), pltpu.VMEM((B, LRU_WIDTH), F32)],
        compiler_params=_params("arbitrary"),
        name="rglru_branch",
    )(xr, gr, swap, cw, cb, wax, ba, bx, lam)


def _diff_attn_kernel(q_ref, k_ref, v_ref, lq1_ref, lk1_ref, lq2_ref, lk2_ref, g_ref, o_ref,
                      qq, m_s, acc_s, kn_s, *, lambda_init, tq):
    qi = pl.program_id(1)
    hw = 2 * DIFF_HD
    lane = lax.broadcasted_iota(jnp.int32, (1, hw), 1)
    acc_s[...] = jnp.zeros_like(acc_s)
    for h in range(DIFF_HEADS):
        q = q_ref[:, h * hw:(h + 1) * hw]
        zero = jnp.zeros_like(q)
        qq[h, 0:tq, :] = jnp.where(lane < DIFF_HD, q, zero)
        qq[h, tq:2 * tq, :] = jnp.where(lane >= DIFF_HD, q, zero)

    @pl.when(qi == 0)
    def _():
        kn_s[...] = jnp.zeros_like(kn_s)

    new_keys = pl.multiple_of(qi * tq, tq)
    d_row = lax.broadcasted_iota(jnp.int32, (hw, 2 * hw), 0)
    d_col = lax.broadcasted_iota(jnp.int32, (hw, 2 * hw), 1)
    comp_sum = ((d_row < DIFF_HD) == (d_col < hw)).astype(BF16)
    bound = []
    for h in range(DIFF_HEADS):
        kb = k_ref[pl.ds(new_keys, tq), h * hw:(h + 1) * hw]
        kn = jnp.max(_dot(kb * kb, comp_sum), axis=0, keepdims=True)
        kn = jnp.maximum(kn_s[h:h + 1, :], kn)
        kn_s[h:h + 1, :] = kn
        q = q_ref[:, h * hw:(h + 1) * hw]
        qn = jnp.max(_dot(q * q, comp_sum), axis=0, keepdims=True)
        n2 = qn * kn
        b = n2 * lax.rsqrt(jnp.maximum(n2, 1e-30))
        bound.append((b[:, 0:1], b[:, hw:hw + 1]))

    def scores(j, h, masked):
        start = pl.multiple_of(j * tq, tq)
        s = _dot_nt(qq[h], k_ref[pl.ds(start, tq), h * hw:(h + 1) * hw])
        if masked:
            row = lax.broadcasted_iota(jnp.int32, s.shape, 0) & (tq - 1)
            col = lax.broadcasted_iota(jnp.int32, s.shape, 1)
            s = jnp.where(col <= row, s, -1e30)
        return s

    chunks = [slice(c * hw, (c + 1) * hw) for c in range(tq // hw)]

    def max_step(j, masked):
        for h in range(DIFF_HEADS):
            s = scores(j, h, masked)
            m = m_s[h]
            for c in chunks:
                m = jnp.maximum(m, s[:, c])
            m_s[h] = m

    def shift_bound(h, s):
        return jnp.concatenate([s[:tq] - bound[h][0], s[tq:] - bound[h][1]], axis=0)

    def shift_row_max(h, s):
        m = m_s[h]
        return jnp.concatenate([s[:, c] - m for c in chunks], axis=1)

    def acc_step(shift, j, masked):
        start = pl.multiple_of(j * tq, tq)
        ones = jnp.ones((tq, hw), BF16)
        for h in range(DIFF_HEADS):
            p = jnp.exp(shift(h, scores(j, h, masked))).astype(BF16)
            v_ext = jnp.concatenate([v_ref[pl.ds(start, tq), h * hw:(h + 1) * hw], ones], axis=1)
            acc_s[h] = acc_s[h] + _dot(p, v_ext)

    def loop(step):
        def body(j, c):
            step(j, False)
            return c
        lax.fori_loop(0, qi, body, 0)
        step(qi, True)

    loop(functools.partial(acc_step, shift_bound))
    row_sum_min = jnp.min(acc_s[0][:, hw:])
    for h in range(1, DIFF_HEADS):
        row_sum_min = jnp.minimum(row_sum_min, jnp.min(acc_s[h][:, hw:]))

    @pl.when(jnp.logical_not(row_sum_min >= ATTN_MIN_ROW_SUM))
    def _():
        m_s[...] = jnp.full(m_s.shape, -1e30, F32)
        acc_s[...] = jnp.zeros_like(acc_s)
        loop(max_step)
        for h in range(DIFF_HEADS):
            m_s[h] = jnp.broadcast_to(jnp.max(m_s[h], axis=-1, keepdims=True), (2 * tq, hw))
        loop(functools.partial(acc_step, shift_row_max))

    lam = (jnp.exp(jnp.sum(lq1_ref[...] * lk1_ref[...], axis=-1, keepdims=True))
           - jnp.exp(jnp.sum(lq2_ref[...] * lk2_ref[...], axis=-1, keepdims=True)) + lambda_init)
    for h in range(DIFF_HEADS):
        acc = acc_s[h]
        o = acc[:, :hw] / acc[:, hw:]
        o = o[:tq] - lam * o[tq:]
        o_ref[:, h * hw:(h + 1) * hw] = (_rms(o, g_ref[...]) * (1.0 - lambda_init)).astype(BF16)


def _diff_attn(q, k, v, lq1, lk1, lq2, lk2, g, lambda_init, B, S):
    tq = ATTN_TQ
    hw = 2 * DIFF_HD
    qspec = pl.BlockSpec((tq, DIFF_W), lambda b, i: (i, b))
    kspec = pl.BlockSpec((S, DIFF_W), lambda b, i: (0, b))
    small = pl.BlockSpec((1, DIFF_HD), lambda b, i: (0, 0))
    return pl.pallas_call(
        functools.partial(_diff_attn_kernel, lambda_init=lambda_init, tq=tq),
        grid=(B, S // tq),
        in_specs=[qspec, kspec, kspec, small, small, small, small,
                  pl.BlockSpec((1, hw), lambda b, i: (0, 0))],
        out_specs=qspec,
        out_shape=jax.ShapeDtypeStruct((S, B * DIFF_W), BF16),
        scratch_shapes=[pltpu.VMEM((DIFF_HEADS, 2 * tq, hw), BF16), pltpu.VMEM((DIFF_HEADS, 2 * tq, hw), F32),
                        pltpu.VMEM((DIFF_HEADS, 2 * tq, 2 * hw), F32), pltpu.VMEM((DIFF_HEADS, 2 * hw), F32)],
        compiler_params=_params("parallel", "arbitrary"),
        name="diff_attn",
    )(q, k, v, lq1, lk1, lq2, lk2, g)


def _merge_xattn_kernel(x_ref, ys_ref, ya_ref, yl_ref, gt_ref, wbs_ref, wba_ref, wbl_ref, wout_ref,
                        g_ref, wq_ref, k_ref, v_ref, wo_ref, o_ref):
    d = D_MODEL
    m = gt_ref[:, 0:d].astype(F32) * _dot(ys_ref[...], wbs_ref[...])
    m = m + gt_ref[:, d:2 * d].astype(F32) * _dot(ya_ref[...], wba_ref[...])
    m = m + gt_ref[:, 2 * d:3 * d].astype(F32) * _dot(yl_ref[...], wbl_ref[...])
    x = x_ref[...] + _dot(m.astype(BF16), wout_ref[...])

    hn = _rms(x, g_ref[...]).astype(BF16)
    q = (_dot(hn, wq_ref[...]) * (XATTN_HD ** -0.5)).astype(BF16)
    outs = []
    for h in range(XATTN_HEADS):
        cols = slice(h * XATTN_HD, (h + 1) * XATTN_HD)
        s = _dot_nt(q[:, cols], k_ref[:, cols])
        p = jnp.exp(s - jnp.max(s, axis=-1, keepdims=True))
        p = p / jnp.sum(p, axis=-1, keepdims=True)
        outs.append(_dot(p.astype(BF16), v_ref[:, cols]).astype(BF16))
    o_ref[...] = x + _dot(jnp.concatenate(outs, axis=1), wo_ref[...])


def _merge_xattn(x, batch_major, ys, ya, yl, gt, wbs, wba, wbl, wout, g, wq, kv, wo, mem_len, B, S):
    rows = ROW_TILE

    def blk(width):
        return pl.BlockSpec((rows, width), lambda b, i: (i, b))

    return pl.pallas_call(
        _merge_xattn_kernel,
        grid=(B, S // rows),
        in_specs=[_x_spec(batch_major, rows, D_MODEL), blk(SSM_WIDTH), blk(DIFF_W), blk(LRU_WIDTH),
                  blk(3 * D_MODEL), _wspec(wbs), _wspec(wba), _wspec(wbl), _wspec(wout),
                  _whole(g.shape), _wspec(wq),
                  pl.BlockSpec((mem_len, D_MODEL), lambda b, i: (b, 0)),
                  pl.BlockSpec((mem_len, D_MODEL), lambda b, i: (b, 1)),
                  _wspec(wo)],
        out_specs=blk(D_MODEL),
        out_shape=jax.ShapeDtypeStruct((S, B * D_MODEL), F32),
        compiler_params=_params("parallel", "parallel"),
        name="merge_xattn",
    )(x, ys, ya, yl, gt, _wop(wbs), _wop(wba), _wop(wbl), _wop(wout), g, _wop(wq), kv, kv, _wop(wo))


def _memkv_kernel(m_ref, g_ref, w_ref, o_ref):
    hn = _rms(m_ref[...], g_ref[...]).astype(BF16)
    o_ref[...] = _dot(hn, w_ref[...]).astype(BF16)


def _memkv(mem_rows, g, wkv):
    n = mem_rows.shape[0]
    rows = ROW_TILE
    return pl.pallas_call(
        _memkv_kernel,
        grid=(n // rows,),
        in_specs=[pl.BlockSpec((rows, D_MODEL), lambda i: (i, 0)), _whole(g.shape), _wspec(wkv)],
        out_specs=pl.BlockSpec((rows, 2 * D_MODEL), lambda i: (i, 0)),
        out_shape=jax.ShapeDtypeStruct((n, 2 * D_MODEL), BF16),
        compiler_params=_params("parallel"),
        name="mem_kv",
    )(mem_rows, g, _wop(wkv))


def _ffn_kernel(x_ref, swap_ref, g_ref, wup_ref, cw_ref, cb_ref, wdn_ref, gf_ref, o_ref, prev,
                *, batch, steps, final):
    rows = batch * steps
    d = D_MODEL
    halo = (FFN_CONV - 1) * batch
    n = batch * batch

    @pl.when(pl.program_id(0) == 0)
    def _():
        prev[...] = jnp.zeros_like(prev)

    swap = swap_ref[...]
    g = g_ref[...]
    hn_b = [_rms(x_ref[:, b * d:(b + 1) * d], g).astype(BF16) for b in range(batch)]
    subs = []
    for k in range(steps // batch):
        hb = jnp.concatenate([h[k * batch:(k + 1) * batch] for h in hn_b], axis=0)
        subs.append(_dot(swap, hb).astype(BF16))
    hn = jnp.concatenate(subs, axis=0)

    def conv(up, part, cols):
        hist = prev[part]
        y = cb_ref[:, cols] + cw_ref[FFN_CONV - 1:FFN_CONV, cols] * up
        for t in range(FFN_CONV - 1):
            back = (FFN_CONV - 1 - t) * batch
            shifted = jnp.concatenate([hist[halo - back:], up[:rows - back]], axis=0)
            y = y + cw_ref[t:t + 1, cols] * shifted
        prev[part] = up[rows - halo:]
        return y

    acc = jnp.zeros((rows, d), F32)
    for j in range(FFN_NCHUNK):
        vc = slice(j * FFN_CHUNK, (j + 1) * FFN_CHUNK)
        gc = slice(D_FF + j * FFN_CHUNK, D_FF + (j + 1) * FFN_CHUNK)
        val = conv(_dot(hn, wup_ref[:, vc]), 2 * j, vc)
        gate = conv(_dot(hn, wup_ref[:, gc]), 2 * j + 1, gc)
        act = (gate * _sigmoid(gate) * val).astype(BF16)
        acc = acc + _dot(act, wdn_ref[vc, :])

    hi = acc.astype(BF16)
    lo = (acc - hi.astype(F32)).astype(BF16)
    for k in range(steps // batch):
        yb = _dot(swap, hi[k * n:(k + 1) * n]) + _dot(swap, lo[k * n:(k + 1) * n])
        ts = slice(k * batch, (k + 1) * batch)
        for b in range(batch):
            out = x_ref[ts, b * d:(b + 1) * d] + yb[b * batch:(b + 1) * batch]
            if final:
                o_ref[b, ts, :] = _rms(out, gf_ref[...])
            else:
                o_ref[ts, b * d:(b + 1) * d] = out


def _ffn(x, swap, g, wup, cw, cb, wdn, gf, B, S, final):
    rows = ROW_TILE
    steps = rows // B
    blk = pl.BlockSpec((steps, B * D_MODEL), lambda i: (i, 0))
    if final:
        out_spec = pl.BlockSpec((B, steps, D_MODEL), lambda i: (0, i, 0))
        out_shape = jax.ShapeDtypeStruct((B, S, D_MODEL), F32)
    else:
        out_spec, out_shape = blk, jax.ShapeDtypeStruct((S, B * D_MODEL), F32)
    return pl.pallas_call(
        functools.partial(_ffn_kernel, batch=B, steps=steps, final=final),
        grid=(S // steps,),
        in_specs=[blk, _whole(swap.shape), _whole(g.shape), _wspec(wup), _whole(cw.shape), _whole(cb.shape),
                  _wspec(wdn), _whole(gf.shape)],
        out_specs=out_spec,
        out_shape=out_shape,
        scratch_shapes=[pltpu.VMEM((2 * FFN_NCHUNK, (FFN_CONV - 1) * B, FFN_CHUNK), F32)],
        compiler_params=_params("arbitrary"),
        name="conv_ffn",
    )(x, swap, g, _wop(wup), cw, cb, _wop(wdn), gf)


def _row(v):
    return v.reshape(1, -1).astype(F32)


def _block_diag(blocks):
    n, r, c = blocks.shape
    eye = jnp.eye(n, dtype=blocks.dtype)
    return (blocks[:, :, None, :] * eye[:, None, :, None]).reshape(n * r, n * c)


def _s5_b_weights(bbr, bbi):
    per = SSM_GROUPS // SSM_LANE_BLOCKS
    out = []
    for j in range(SSM_LANE_BLOCKS):
        r = bbr[j * 128:(j + 1) * 128].reshape(per, SSM_GROUP, SSM_STATE)
        i = bbi[j * 128:(j + 1) * 128].reshape(per, SSM_GROUP, SSM_STATE)
        out.append(jnp.concatenate([_block_diag(r), _block_diag(i)], axis=1))
    return jnp.stack(out).astype(BF16)


def _s5_c_weights(c):
    per = SSM_GROUPS // SSM_LANE_BLOCKS
    ct = jnp.swapaxes(c, 1, 2)
    return jnp.stack([_block_diag(ct[j * per:(j + 1) * per]) for j in range(SSM_LANE_BLOCKS)]).astype(BF16)


def kernel(x, mem, positions, norm_mix_g, w_in, ssm_lambda_re, ssm_lambda_im, ssm_log_step, ssm_b_re, ssm_b_im, ssm_c_re, ssm_c_im, ssm_d, ssm_w_glu, ssm_b_glu, diff_lq1, diff_lk1, diff_lq2, diff_lk2, diff_subln_g, lru_conv_w, lru_conv_b, lru_wa, lru_ba, lru_wx, lru_bx, lru_lambda, w_br_ssm, w_br_attn, w_br_lru, w_out, norm_xattn_g, norm_mem_g, xattn_wq, xattn_wkv, xattn_wo, norm_ffn_g, ffn_w_up, ffn_conv_w, ffn_conv_b, ffn_w_down, final_norm_g):
    B, S, _ = x.shape
    depth = norm_mix_g.shape[0]
    mem_len = mem.shape[1]
    assert S % ROW_TILE == 0 and ROW_TILE % B == 0 and B % 8 == 0 and (B * mem_len) % ROW_TILE == 0

    cos, sin = _rope_tables(positions.astype(F32)[..., None], B, S)

    rep = lambda a: jnp.repeat(a, SSM_GROUP, axis=1)
    b_t = lambda a: jnp.swapaxes(a, 2, 3).reshape(depth, SSM_WIDTH, SSM_STATE)
    abr, abi, bbr, bbi = _s5_prep(rep(ssm_lambda_re), rep(ssm_lambda_im), rep(ssm_log_step[..., None]),
                                  b_t(ssm_b_re), b_t(ssm_b_im))
    mem_rows = mem.reshape(B * mem_len, D_MODEL)
    swap = _swap_matrix(B)
    (w_in, ssm_w_glu, w_br_ssm, w_br_attn, w_br_lru, w_out, xattn_wq, xattn_wkv, xattn_wo, ffn_w_up,
     ffn_w_down) = (a.astype(BF16) for a in (w_in, ssm_w_glu, w_br_ssm, w_br_attn, w_br_lru, w_out, xattn_wq,
                                             xattn_wkv, xattn_wo, ffn_w_up, ffn_w_down))

    xs = x
    batch_major = True
    for l in range(depth):
        lambda_init = 0.8 - 0.6 * math.exp(-0.3 * l)
        u, q, k, v, xr, gr, gt = _inproj(xs, batch_major, _row(norm_mix_g[l]), _Layer(w_in, l), cos, sin, B, S)

        y_ssm = _s5(u, swap, _s5_b_weights(bbr[l], bbi[l]),
                    abr[l, ::SSM_GROUP].reshape(1, SSM_NSTATE), abi[l, ::SSM_GROUP].reshape(1, SSM_NSTATE),
                    _s5_c_weights(ssm_c_re[l]), _s5_c_weights(ssm_c_im[l]), _row(ssm_d[l]),
                    _Layer(ssm_w_glu, l), _row(ssm_b_glu[l]), B)

        wax = jnp.concatenate([_block_diag(lru_wa[l]), _block_diag(lru_wx[l])], axis=1).astype(BF16)
        y_lru = _lru(xr, gr, swap, lru_conv_w[l].astype(F32), _row(lru_conv_b[l]), wax, _row(lru_ba[l]),
                     _row(lru_bx[l]), _row(lru_lambda[l]), B)

        y_att = _diff_attn(q, k, v, _row(diff_lq1[l]), _row(diff_lk1[l]), _row(diff_lq2[l]), _row(diff_lk2[l]),
                           _row(diff_subln_g[l]), lambda_init, B, S)

        kv = _memkv(mem_rows, _row(norm_mem_g[l]), _Layer(xattn_wkv, l))
        xs = _merge_xattn(xs, batch_major, y_ssm, y_att, y_lru, gt, _Layer(w_br_ssm, l), _Layer(w_br_attn, l),
                          _Layer(w_br_lru, l), _Layer(w_out, l), _row(norm_xattn_g[l]), _Layer(xattn_wq, l), kv,
                          _Layer(xattn_wo, l), mem_len, B, S)
        batch_major = False

        xs = _ffn(xs, swap, _row(norm_ffn_g[l]), _Layer(ffn_w_up, l), ffn_conv_w[l].astype(F32),
                  _row(ffn_conv_b[l]), _Layer(ffn_w_down, l), _row(final_norm_g), B, S, final=l == depth - 1)

    return xs
```

```python
import collections
import functools
import math

import jax
import jax.numpy as jnp
from jax import lax
from jax.experimental import pallas as pl
from jax.experimental.pallas import tpu as pltpu

F32 = jnp.float32
BF16 = jnp.bfloat16

EPS = 1e-6
D_MODEL = 1024
SSM_WIDTH = 384
SSM_GROUP = 16
SSM_GROUPS = 24
SSM_STATE = 64
SSM_NSTATE = SSM_GROUPS * SSM_STATE
SSM_LANE_BLOCKS = 3
DIFF_HEADS = 4
DIFF_HD = 64
DIFF_W = 512
ROPE_THETA = 10000.0
LRU_WIDTH = 512
LRU_HEADS = 8
LRU_HD = 64
LRU_CONV = 4
LRU_C = 8.0
XATTN_HEADS = 4
XATTN_HD = 256
D_FF = 2816
FFN_CONV = 3
FFN_CHUNK = 256
FFN_NCHUNK = D_FF // FFN_CHUNK
OFF_U, OFF_Q, OFF_K, OFF_V, OFF_XR, OFF_GR, OFF_G = 0, 384, 896, 1408, 1920, 2432, 2944
D_IN = 6016

ROW_TILE = 512
FFN_ROW_TILE = 1024
ATTN_TK = 512
ATTN_MIN_ROW_SUM = 1e-25
VMEM_LIMIT = 56 * 1024 * 1024


def _dot(a, b):
    return jnp.dot(a, b, preferred_element_type=F32)


def _dot_nt(a, b):
    return lax.dot_general(a, b, (((1,), (1,)), ((), ())), preferred_element_type=F32)


def _rms(x, g):
    ms = jnp.mean(x * x, axis=-1, keepdims=True)
    return x * lax.rsqrt(ms + EPS) * g


def _sigmoid(x):
    return 1.0 / (1.0 + jnp.exp(-x))


def _gelu(x):
    return 0.5 * x * (1.0 + jnp.tanh(0.7978845608028654 * (x + 0.044715 * (x * x * x))))


def _params(*sem):
    return pltpu.CompilerParams(dimension_semantics=sem, vmem_limit_bytes=VMEM_LIMIT)


def _whole(shape):
    zeros = (0,) * len(shape)
    return pl.BlockSpec(shape, lambda *_: zeros, pipeline_mode=pl.Buffered(1))


_Layer = collections.namedtuple("_Layer", "stack index")


def _wspec(w):
    if isinstance(w, _Layer):
        shape = w.stack.shape[1:]
        idx = (w.index,) + (0,) * len(shape)
        return pl.BlockSpec((None,) + shape, lambda *_: idx, pipeline_mode=pl.Buffered(1))
    return _whole(w.shape)


def _wop(w):
    return w.stack if isinstance(w, _Layer) else w


def _x_spec(batch_major, rows, width):
    if batch_major:
        return pl.BlockSpec((None, rows, width), lambda b, i: (b, i, 0))
    return pl.BlockSpec((rows, width), lambda b, i: (i, b))


def _rope_kernel(pos_ref, cos_ref, sin_ref):
    lane = lax.broadcasted_iota(jnp.int32, (1, 128), 1)
    j = (lane & 31).astype(F32)
    inv = jnp.exp((-math.log(ROPE_THETA) * (2.0 * j)) / DIFF_HD)
    ang = pos_ref[...] * inv
    first_half = (lane & 63) < 32
    cos_ref[...] = jnp.cos(ang)
    s = jnp.sin(ang)
    sin_ref[...] = jnp.where(first_half, -s, s)


def _rope_tables(pos_f, B, S):
    rows = ROW_TILE
    spec = pl.BlockSpec((None, rows, 128), lambda b, i: (b, i, 0))
    return pl.pallas_call(
        _rope_kernel,
        grid=(B, S // rows),
        in_specs=[pl.BlockSpec((None, rows, 1), lambda b, i: (b, i, 0))],
        out_specs=[spec, spec],
        out_shape=[jax.ShapeDtypeStruct((B, S, 128), F32)] * 2,
        compiler_params=_params("parallel", "parallel"),
        name="rope_tables",
    )(pos_f)


def _s5_prep_kernel(lr_ref, li_ref, ls_ref, br_ref, bi_ref, abr_ref, abi_ref, bbr_ref, bbi_ref):
    lr = lr_ref[...]
    li = li_ref[...]
    dt = jnp.exp(ls_ref[...])
    mag = jnp.exp(lr * dt)
    ab_r = mag * jnp.cos(li * dt)
    ab_i = mag * jnp.sin(li * dt)
    den = lr * lr + li * li
    nr = ab_r - 1.0
    f_r = (nr * lr + ab_i * li) / den
    f_i = (ab_i * lr - nr * li) / den
    br = br_ref[...]
    bi = bi_ref[...]
    abr_ref[...] = ab_r
    abi_ref[...] = ab_i
    bbr_ref[...] = f_r * br - f_i * bi
    bbi_ref[...] = f_r * bi + f_i * br


def _s5_prep(lr_rep, li_rep, ls_rep, br_t, bi_t):
    depth = lr_rep.shape[0]
    spec = pl.BlockSpec((None, SSM_WIDTH, SSM_STATE), lambda l: (l, 0, 0))
    return pl.pallas_call(
        _s5_prep_kernel,
        grid=(depth,),
        in_specs=[spec, spec, pl.BlockSpec((None, SSM_WIDTH, 1), lambda l: (l, 0, 0)), spec, spec],
        out_specs=[spec] * 4,
        out_shape=[jax.ShapeDtypeStruct((depth, SSM_WIDTH, SSM_STATE), F32)] * 4,
        compiler_params=_params("parallel"),
        name="s5_prep",
    )(lr_rep, li_rep, ls_rep, br_t, bi_t)


def _rope(x, cos, sin_signed, first_half):
    back = pltpu.roll(x, 32, 1)
    fwd = pltpu.roll(x, DIFF_W - 32, 1)
    return x * cos + jnp.where(first_half, fwd, back) * sin_signed


def _inproj_kernel(x_ref, g_ref, w_ref, cos_ref, sin_ref,
                   u_ref, q_ref, k_ref, v_ref, xr_ref, gr_ref, gt_ref):
    hn = _rms(x_ref[...], g_ref[...]).astype(BF16)

    def seg(off, width):
        return _dot(hn, w_ref[:, off:off + width])

    cos = jnp.concatenate([cos_ref[...]] * 4, axis=1)
    sin = jnp.concatenate([sin_ref[...]] * 4, axis=1)
    lane = lax.broadcasted_iota(jnp.int32, (1, DIFF_W), 1)
    first_half = (lane & 63) < 32

    u_ref[...] = seg(OFF_U, SSM_WIDTH).astype(BF16)
    q_ref[...] = (_rope(seg(OFF_Q, DIFF_W), cos, sin, first_half) * (DIFF_HD ** -0.5)).astype(BF16)
    k_ref[...] = _rope(seg(OFF_K, DIFF_W), cos, sin, first_half).astype(BF16)
    v_ref[...] = seg(OFF_V, DIFF_W).astype(BF16)
    xr_ref[...] = seg(OFF_XR, LRU_WIDTH).astype(BF16)
    gr_ref[...] = seg(OFF_GR, LRU_WIDTH).astype(BF16)
    for c in range(6):
        gt_ref[:, c * 512:(c + 1) * 512] = _sigmoid(seg(OFF_G + c * 512, 512)).astype(BF16)


def _inproj(x, batch_major, g, w_in, cos, sin, B, S):
    rows = ROW_TILE

    def out(width):
        return pl.BlockSpec((rows, width), lambda b, i: (i, b))

    widths = (SSM_WIDTH, DIFF_W, DIFF_W, DIFF_W, LRU_WIDTH, LRU_WIDTH, 3 * D_MODEL)
    tab = pl.BlockSpec((None, rows, 128), lambda b, i: (b, i, 0))
    return pl.pallas_call(
        _inproj_kernel,
        grid=(B, S // rows),
        in_specs=[_x_spec(batch_major, rows, D_MODEL), _whole((1, D_MODEL)), _wspec(w_in), tab, tab],
        out_specs=[out(w) for w in widths],
        out_shape=[jax.ShapeDtypeStruct((S, B * w), BF16) for w in widths],
        compiler_params=_params("parallel", "parallel"),
        name="inproj",
    )(x, g, _wop(w_in), cos, sin)


def _swap_matrix(batch):
    idx = jnp.arange(batch * batch)
    return (idx[:, None] == (idx[None, :] % batch) * batch + idx[None, :] // batch).astype(BF16)


def _load_time_major(ref, width, batch, steps, swap):
    subs = []
    for k in range(steps // batch):
        rows = slice(k * batch, (k + 1) * batch)
        xb = jnp.concatenate([ref[rows, b * width:(b + 1) * width] for b in range(batch)], axis=0)
        subs.append(_dot(swap, xb).astype(BF16))
    return jnp.concatenate(subs, axis=0)


def _store_batch_major(ref, y, width, batch, steps, swap):
    n = batch * batch
    for k in range(steps // batch):
        yb = _dot(swap, y[k * n:(k + 1) * n]).astype(BF16)
        for b in range(batch):
            ref[k * batch:(k + 1) * batch, b * width:(b + 1) * width] = yb[b * batch:(b + 1) * batch]


def _s5_kernel(u_ref, swap_ref, wb_ref, abr_ref, abi_ref, cre_ref, cim_ref, d_ref, wglu_ref, bglu_ref,
               o_ref, sre, sim, st_re, st_im, *, batch, steps):
    @pl.when(pl.program_id(0) == 0)
    def _():
        st_re[...] = jnp.zeros_like(st_re)
        st_im[...] = jnp.zeros_like(st_im)

    swap = swap_ref[...]
    u = _load_time_major(u_ref, SSM_WIDTH, batch, steps, swap)
    nb = SSM_NSTATE // SSM_LANE_BLOCKS
    for j in range(SSM_LANE_BLOCKS):
        bu = _dot(u[:, j * 128:(j + 1) * 128], wb_ref[j])
        sre[:, j * nb:(j + 1) * nb] = bu[:, :nb]
        sim[:, j * nb:(j + 1) * nb] = bu[:, nb:]

    for j in range(SSM_LANE_BLOCKS):
        cols = slice(j * nb, (j + 1) * nb)
        ar = jnp.broadcast_to(abr_ref[:, cols], (batch, nb))
        ai = jnp.broadcast_to(abi_ref[:, cols], (batch, nb))
        sr = st_re[:, cols]
        si = st_im[:, cols]
        for t in range(steps):
            rows = slice(t * batch, (t + 1) * batch)
            nr = ar * sr - ai * si + sre[rows, cols]
            ni = ar * si + ai * sr + sim[rows, cols]
            sre[rows, cols] = nr
            sim[rows, cols] = ni
            sr, si = nr, ni
        st_re[:, cols] = sr
        st_im[:, cols] = si

    ys = []
    for j in range(SSM_LANE_BLOCKS):
        cols = slice(j * nb, (j + 1) * nb)
        ys.append(_dot(sre[:, cols].astype(BF16), cre_ref[j]) - _dot(sim[:, cols].astype(BF16), cim_ref[j]))
    y = jnp.concatenate(ys, axis=1) + d_ref[...] * u.astype(F32)
    y = _gelu(y)
    z = _dot(y.astype(BF16), wglu_ref[...]) + bglu_ref[...]
    out = (z[:, :SSM_WIDTH] * _sigmoid(z[:, SSM_WIDTH:])).astype(BF16)
    _store_batch_major(o_ref, out, SSM_WIDTH, batch, steps, swap)


def _s5(u, swap, wb, abr, abi, cre, cim, d, wglu, bglu, B):
    S = u.shape[0]
    rows = ROW_TILE
    steps = rows // B
    blk = pl.BlockSpec((steps, B * SSM_WIDTH), lambda i: (i, 0))
    return pl.pallas_call(
        functools.partial(_s5_kernel, batch=B, steps=steps),
        grid=(S // steps,),
        in_specs=[blk, _whole(swap.shape), _whole(wb.shape), _whole(abr.shape), _whole(abi.shape),
                  _whole(cre.shape), _whole(cim.shape), _whole(d.shape), _wspec(wglu), _whole(bglu.shape)],
        out_specs=blk,
        out_shape=jax.ShapeDtypeStruct((S, B * SSM_WIDTH), BF16),
        scratch_shapes=[pltpu.VMEM((rows, SSM_NSTATE), F32), pltpu.VMEM((rows, SSM_NSTATE), F32),
                        pltpu.VMEM((B, SSM_NSTATE), F32), pltpu.VMEM((B, SSM_NSTATE), F32)],
        compiler_params=_params("arbitrary"),
        name="s5_branch",
    )(u, swap, wb, abr, abi, cre, cim, d, _wop(wglu), bglu)


def _lru_kernel(xr_ref, gr_ref, swap_ref, cw_ref, cb_ref, wax_ref, ba_ref, bx_ref, lam_ref,
                o_ref, ext, a_s, b_s, h_st, *, batch, steps):
    rows = batch * steps
    halo = (LRU_CONV - 1) * batch

    @pl.when(pl.program_id(0) == 0)
    def _():
        ext[0:halo, :] = jnp.zeros((halo, LRU_WIDTH), F32)
        h_st[...] = jnp.zeros_like(h_st)

    swap = swap_ref[...]
    ext[halo:halo + rows, :] = _load_time_major(xr_ref, LRU_WIDTH, batch, steps, swap).astype(F32)
    xc = cb_ref[...] + cw_ref[LRU_CONV - 1:LRU_CONV, :] * ext[halo:halo + rows, :]
    for j in range(LRU_CONV - 1):
        xc = xc + cw_ref[j:j + 1, :] * ext[j * batch:j * batch + rows, :]
    ext[0:halo, :] = ext[rows:rows + halo, :]

    z = _dot(xc.astype(BF16), wax_ref[...])
    r = _sigmoid(z[:, :LRU_WIDTH] + ba_ref[...])
    ig = _sigmoid(z[:, LRU_WIDTH:] + bx_ref[...])
    softplus_neg_lam = jnp.log1p(jnp.exp(-lam_ref[...]))
    a = jnp.exp((-LRU_C) * r * softplus_neg_lam)
    a_s[...] = a
    y = 1.0 - a * a
    b_s[...] = jnp.where(y > 0.0, y * lax.rsqrt(y), 0.0) * (ig * xc)

    h = h_st[...]
    for t in range(steps):
        rs = slice(t * batch, (t + 1) * batch)
        h = a_s[rs, :] * h + b_s[rs, :]
        b_s[rs, :] = h
    h_st[...] = h
    gr = _load_time_major(gr_ref, LRU_WIDTH, batch, steps, swap).astype(F32)
    _store_batch_major(o_ref, (b_s[...] * _gelu(gr)).astype(BF16), LRU_WIDTH, batch, steps, swap)


def _lru(xr, gr, swap, cw, cb, wax, ba, bx, lam, B):
    S = xr.shape[0]
    rows = ROW_TILE
    steps = rows // B
    blk = pl.BlockSpec((steps, B * LRU_WIDTH), lambda i: (i, 0))
    halo = (LRU_CONV - 1) * B
    return pl.pallas_call(
        functools.partial(_lru_kernel, batch=B, steps=steps),
        grid=(S // steps,),
        in_specs=[blk, blk, _whole(swap.shape), _whole(cw.shape), _whole(cb.shape), _whole(wax.shape),
                  _whole(ba.shape), _whole(bx.shape), _whole(lam.shape)],
        out_specs=blk,
        out_shape=jax.ShapeDtypeStruct((S, B * LRU_WIDTH), BF16),
        scratch_shapes=[pltpu.VMEM((rows + halo, LRU_WIDTH), F32), pltpu.VMEM((rows, LRU_WIDTH), F32),
                        pltpu.VMEM((rows, LRU_WIDTH), F32), pltpu.VMEM((B, LRU_WIDTH), F32)],
        compiler_params=_params("arbitrary"),
        name="rglru_branch",
    )(xr, gr, swap, cw, cb, wax, ba, bx, lam)


def _diff_attn_kernel(q_ref, k_ref, v_ref, lq1_ref, lk1_ref, lq2_ref, lk2_ref, g_ref, o_ref,
                      qq, m_s, acc_s, kn_s, *, lambda_init, tk):
    qi = pl.program_id(1)
    hw = 2 * DIFF_HD
    tq = 2 * tk
    every = slice(0, 2 * tq)
    upper = slice(tq, 2 * tq)
    lane = lax.broadcasted_iota(jnp.int32, (1, hw), 1)
    acc_s[...] = jnp.zeros_like(acc_s)
    for h in range(DIFF_HEADS):
        for half in range(2):
            q = q_ref[half * tk:(half + 1) * tk, h * hw:(h + 1) * hw]
            zero = jnp.zeros_like(q)
            qq[h, half * tq:half * tq + tk, :] = jnp.where(lane < DIFF_HD, q, zero)
            qq[h, half * tq + tk:(half + 1) * tq, :] = jnp.where(lane >= DIFF_HD, q, zero)

    @pl.when(qi == 0)
    def _():
        kn_s[...] = jnp.zeros_like(kn_s)

    new_keys = pl.multiple_of(qi * tq, tq)
    d_row = lax.broadcasted_iota(jnp.int32, (hw, 2 * hw), 0)
    d_col = lax.broadcasted_iota(jnp.int32, (hw, 2 * hw), 1)
    comp_sum = ((d_row < DIFF_HD) == (d_col < hw)).astype(BF16)
    bound = []
    for h in range(DIFF_HEADS):
        kb = k_ref[pl.ds(new_keys, tq), h * hw:(h + 1) * hw]
        kn = jnp.max(_dot(kb * kb, comp_sum), axis=0, keepdims=True)
        kn = jnp.maximum(kn_s[h:h + 1, :], kn)
        kn_s[h:h + 1, :] = kn
        q = q_ref[:, h * hw:(h + 1) * hw]
        qn = jnp.max(_dot(q * q, comp_sum), axis=0, keepdims=True)
        n2 = qn * kn
        b = n2 * lax.rsqrt(jnp.maximum(n2, 1e-30))
        bound.append((b[:, 0:1], b[:, hw:hw + 1]))

    def scores(j, h, rows, diagonal):
        start = pl.multiple_of(j * tk, tk)
        s = _dot_nt(qq[h, rows, :], k_ref[pl.ds(start, tk), h * hw:(h + 1) * hw])
        if diagonal:
            row = lax.broadcasted_iota(jnp.int32, s.shape, 0)
            col = lax.broadcasted_iota(jnp.int32, s.shape, 1)
            seen = col <= (row & (tk - 1))
            if s.shape[0] == 2 * tq:
                seen = seen | (row >= tq)
            s = jnp.where(seen, s, -1e30)
        return s

    chunks = [slice(c * hw, (c + 1) * hw) for c in range(tk // hw)]

    def max_step(j, rows, diagonal):
        for h in range(DIFF_HEADS):
            s = scores(j, h, rows, diagonal)
            m = m_s[h, rows, :]
            for c in chunks:
                m = jnp.maximum(m, s[:, c])
            m_s[h, rows, :] = m

    def shift_bound(h, rows, s):
        return jnp.concatenate([s[i * tk:(i + 1) * tk] - bound[h][i % 2] for i in range(s.shape[0] // tk)], axis=0)

    def shift_row_max(h, rows, s):
        m = m_s[h, rows, :]
        return jnp.concatenate([s[:, c] - m for c in chunks], axis=1)

    def acc_step(shift, j, rows, diagonal):
        start = pl.multiple_of(j * tk, tk)
        ones = jnp.ones((tk, hw), BF16)
        for h in range(DIFF_HEADS):
            p = jnp.exp(shift(h, rows, scores(j, h, rows, diagonal))).astype(BF16)
            v_ext = jnp.concatenate([v_ref[pl.ds(start, tk), h * hw:(h + 1) * hw], ones], axis=1)
            acc_s[h, rows, :] = acc_s[h, rows, :] + _dot(p, v_ext)

    def loop(step):
        def body(j, c):
            step(j, every, False)
            return c
        lax.fori_loop(0, 2 * qi, body, 0)
        step(2 * qi, every, True)
        step(2 * qi + 1, upper, True)

    loop(functools.partial(acc_step, shift_bound))
    row_sum_min = jnp.min(acc_s[0][:, hw:])
    for h in range(1, DIFF_HEADS):
        row_sum_min = jnp.minimum(row_sum_min, jnp.min(acc_s[h][:, hw:]))

    @pl.when(jnp.logical_not(row_sum_min >= ATTN_MIN_ROW_SUM))
    def _():
        m_s[...] = jnp.full(m_s.shape, -1e30, F32)
        acc_s[...] = jnp.zeros_like(acc_s)
        loop(max_step)
        for h in range(DIFF_HEADS):
            m_s[h] = jnp.broadcast_to(jnp.max(m_s[h], axis=-1, keepdims=True), (2 * tq, hw))
        loop(functools.partial(acc_step, shift_row_max))

    lam = (jnp.exp(jnp.sum(lq1_ref[...] * lk1_ref[...], axis=-1, keepdims=True))
           - jnp.exp(jnp.sum(lq2_ref[...] * lk2_ref[...], axis=-1, keepdims=True)) + lambda_init)
    for h in range(DIFF_HEADS):
        for half in range(2):
            acc = acc_s[h, half * tq:(half + 1) * tq, :]
            o = acc[:, :hw] / acc[:, hw:]
            o = o[:tk] - lam * o[tk:]
            o_ref[half * tk:(half + 1) * tk, h * hw:(h + 1) * hw] = (
                _rms(o, g_ref[...]) * (1.0 - lambda_init)).astype(BF16)


def _diff_attn(q, k, v, lq1, lk1, lq2, lk2, g, lambda_init, B, S):
    tk = ATTN_TK
    tq = 2 * tk
    hw = 2 * DIFF_HD
    qspec = pl.BlockSpec((tq, DIFF_W), lambda b, i: (i, b))
    kspec = pl.BlockSpec((S, DIFF_W), lambda b, i: (0, b))
    small = pl.BlockSpec((1, DIFF_HD), lambda b, i: (0, 0))
    return pl.pallas_call(
        functools.partial(_diff_attn_kernel, lambda_init=lambda_init, tk=tk),
        grid=(B, S // tq),
        in_specs=[qspec, kspec, kspec, small, small, small, small,
                  pl.BlockSpec((1, hw), lambda b, i: (0, 0))],
        out_specs=qspec,
        out_shape=jax.ShapeDtypeStruct((S, B * DIFF_W), BF16),
        scratch_shapes=[pltpu.VMEM((DIFF_HEADS, 2 * tq, hw), BF16), pltpu.VMEM((DIFF_HEADS, 2 * tq, hw), F32),
                        pltpu.VMEM((DIFF_HEADS, 2 * tq, 2 * hw), F32), pltpu.VMEM((DIFF_HEADS, 2 * hw), F32)],
        compiler_params=_params("parallel", "arbitrary"),
        name="diff_attn",
    )(q, k, v, lq1, lk1, lq2, lk2, g)


def _merge_xattn_kernel(x_ref, ys_ref, ya_ref, yl_ref, gt_ref, wbs_ref, wba_ref, wbl_ref, wout_ref,
                        g_ref, wq_ref, k_ref, v_ref, wo_ref, o_ref):
    d = D_MODEL
    m = gt_ref[:, 0:d].astype(F32) * _dot(ys_ref[...], wbs_ref[...])
    m = m + gt_ref[:, d:2 * d].astype(F32) * _dot(ya_ref[...], wba_ref[...])
    m = m + gt_ref[:, 2 * d:3 * d].astype(F32) * _dot(yl_ref[...], wbl_ref[...])
    x = x_ref[...] + _dot(m.astype(BF16), wout_ref[...])

    hn = _rms(x, g_ref[...]).astype(BF16)
    q = (_dot(hn, wq_ref[...]) * (XATTN_HD ** -0.5)).astype(BF16)
    outs = []
    for h in range(XATTN_HEADS):
        cols = slice(h * XATTN_HD, (h + 1) * XATTN_HD)
        s = _dot_nt(q[:, cols], k_ref[:, cols])
        p = jnp.exp(s - jnp.max(s, axis=-1, keepdims=True))
        p = p / jnp.sum(p, axis=-1, keepdims=True)
        outs.append(_dot(p.astype(BF16), v_ref[:, cols]).astype(BF16))
    o_ref[...] = x + _dot(jnp.concatenate(outs, axis=1), wo_ref[...])


def _merge_xattn(x, batch_major, ys, ya, yl, gt, wbs, wba, wbl, wout, g, wq, kv, wo, mem_len, B, S):
    rows = ROW_TILE

    def blk(width):
        return pl.BlockSpec((rows, width), lambda b, i: (i, b))

    return pl.pallas_call(
        _merge_xattn_kernel,
        grid=(B, S // rows),
        in_specs=[_x_spec(batch_major, rows, D_MODEL), blk(SSM_WIDTH), blk(DIFF_W), blk(LRU_WIDTH),
                  blk(3 * D_MODEL), _wspec(wbs), _wspec(wba), _wspec(wbl), _wspec(wout),
                  _whole(g.shape), _wspec(wq),
                  pl.BlockSpec((mem_len, D_MODEL), lambda b, i: (b, 0)),
                  pl.BlockSpec((mem_len, D_MODEL), lambda b, i: (b, 1)),
                  _wspec(wo)],
        out_specs=blk(D_MODEL),
        out_shape=jax.ShapeDtypeStruct((S, B * D_MODEL), F32),
        compiler_params=_params("parallel", "parallel"),
        name="merge_xattn",
    )(x, ys, ya, yl, gt, _wop(wbs), _wop(wba), _wop(wbl), _wop(wout), g, _wop(wq), kv, kv, _wop(wo))


def _memkv_kernel(m_ref, g_ref, w_ref, o_ref):
    hn = _rms(m_ref[...], g_ref[...]).astype(BF16)
    o_ref[...] = _dot(hn, w_ref[...]).astype(BF16)


def _memkv(mem_rows, g, wkv):
    n = mem_rows.shape[0]
    rows = ROW_TILE
    return pl.pallas_call(
        _memkv_kernel,
        grid=(n // rows,),
        in_specs=[pl.BlockSpec((rows, D_MODEL), lambda i: (i, 0)), _whole(g.shape), _wspec(wkv)],
        out_specs=pl.BlockSpec((rows, 2 * D_MODEL), lambda i: (i, 0)),
        out_shape=jax.ShapeDtypeStruct((n, 2 * D_MODEL), BF16),
        compiler_params=_params("parallel"),
        name="mem_kv",
    )(mem_rows, g, _wop(wkv))


def _ffn_kernel(x_ref, swap_ref, g_ref, wup_ref, cw_ref, cb_ref, wdn_ref, gf_ref, o_ref, prev,
                *, batch, steps, final):
    rows = batch * steps
    d = D_MODEL
    halo = (FFN_CONV - 1) * batch
    n = batch * batch

    @pl.when(pl.program_id(0) == 0)
    def _():
        prev[...] = jnp.zeros_like(prev)

    swap = swap_ref[...]
    g = g_ref[...]
    hn_b = [_rms(x_ref[:, b * d:(b + 1) * d], g).astype(BF16) for b in range(batch)]
    subs = []
    for k in range(steps // batch):
        hb = jnp.concatenate([h[k * batch:(k + 1) * batch] for h in hn_b], axis=0)
        subs.append(_dot(swap, hb).astype(BF16))
    hn = jnp.concatenate(subs, axis=0)

    def conv(up, part, cols):
        hist = prev[part]
        y = cb_ref[:, cols] + cw_ref[FFN_CONV - 1:FFN_CONV, cols] * up
        for t in range(FFN_CONV - 1):
            back = (FFN_CONV - 1 - t) * batch
            shifted = jnp.concatenate([hist[halo - back:], up[:rows - back]], axis=0)
            y = y + cw_ref[t:t + 1, cols] * shifted
        prev[part] = up[rows - halo:]
        return y

    acc = jnp.zeros((rows, d), F32)
    for j in range(FFN_NCHUNK):
        vc = slice(j * FFN_CHUNK, (j + 1) * FFN_CHUNK)
        gc = slice(D_FF + j * FFN_CHUNK, D_FF + (j + 1) * FFN_CHUNK)
        val = conv(_dot(hn, wup_ref[:, vc]), 2 * j, vc)
        gate = conv(_dot(hn, wup_ref[:, gc]), 2 * j + 1, gc)
        act = (gate * _sigmoid(gate) * val).astype(BF16)
        acc = acc + _dot(act, wdn_ref[vc, :])

    hi = acc.astype(BF16)
    lo = (acc - hi.astype(F32)).astype(BF16)
    for k in range(steps // batch):
        yb = _dot(swap, hi[k * n:(k + 1) * n]) + _dot(swap, lo[k * n:(k + 1) * n])
        ts = slice(k * batch, (k + 1) * batch)
        for b in range(batch):
            out = x_ref[ts, b * d:(b + 1) * d] + yb[b * batch:(b + 1) * batch]
            if final:
                o_ref[b, ts, :] = _rms(out, gf_ref[...])
            else:
                o_ref[ts, b * d:(b + 1) * d] = out


def _ffn(x, swap, g, wup, cw, cb, wdn, gf, B, S, final):
    rows = FFN_ROW_TILE
    steps = rows // B
    blk = pl.BlockSpec((steps, B * D_MODEL), lambda i: (i, 0))
    if final:
        out_spec = pl.BlockSpec((B, steps, D_MODEL), lambda i: (0, i, 0))
        out_shape = jax.ShapeDtypeStruct((B, S, D_MODEL), F32)
    else:
        out_spec, out_shape = blk, jax.ShapeDtypeStruct((S, B * D_MODEL), F32)
    return pl.pallas_call(
        functools.partial(_ffn_kernel, batch=B, steps=steps, final=final),
        grid=(S // steps,),
        in_specs=[blk, _whole(swap.shape), _whole(g.shape), _wspec(wup), _whole(cw.shape), _whole(cb.shape),
                  _wspec(wdn), _whole(gf.shape)],
        out_specs=out_spec,
        out_shape=out_shape,
        scratch_shapes=[pltpu.VMEM((2 * FFN_NCHUNK, (FFN_CONV - 1) * B, FFN_CHUNK), F32)],
        compiler_params=_params("arbitrary"),
        name="conv_ffn",
    )(x, swap, g, _wop(wup), cw, cb, _wop(wdn), gf)


def _row(v):
    return v.reshape(1, -1).astype(F32)


def _block_diag(blocks):
    n, r, c = blocks.shape
    eye = jnp.eye(n, dtype=blocks.dtype)
    return (blocks[:, :, None, :] * eye[:, None, :, None]).reshape(n * r, n * c)


def _s5_b_weights(bbr, bbi):
    per = SSM_GROUPS // SSM_LANE_BLOCKS
    out = []
    for j in range(SSM_LANE_BLOCKS):
        r = bbr[j * 128:(j + 1) * 128].reshape(per, SSM_GROUP, SSM_STATE)
        i = bbi[j * 128:(j + 1) * 128].reshape(per, SSM_GROUP, SSM_STATE)
        out.append(jnp.concatenate([_block_diag(r), _block_diag(i)], axis=1))
    return jnp.stack(out).astype(BF16)


def _s5_c_weights(c):
    per = SSM_GROUPS // SSM_LANE_BLOCKS
    ct = jnp.swapaxes(c, 1, 2)
    return jnp.stack([_block_diag(ct[j * per:(j + 1) * per]) for j in range(SSM_LANE_BLOCKS)]).astype(BF16)


def kernel(x, mem, positions, norm_mix_g, w_in, ssm_lambda_re, ssm_lambda_im, ssm_log_step, ssm_b_re, ssm_b_im, ssm_c_re, ssm_c_im, ssm_d, ssm_w_glu, ssm_b_glu, diff_lq1, diff_lk1, diff_lq2, diff_lk2, diff_subln_g, lru_conv_w, lru_conv_b, lru_wa, lru_ba, lru_wx, lru_bx, lru_lambda, w_br_ssm, w_br_attn, w_br_lru, w_out, norm_xattn_g, norm_mem_g, xattn_wq, xattn_wkv, xattn_wo, norm_ffn_g, ffn_w_up, ffn_conv_w, ffn_conv_b, ffn_w_down, final_norm_g):
    B, S, _ = x.shape
    depth = norm_mix_g.shape[0]
    mem_len = mem.shape[1]
    assert S % ROW_TILE == 0 and ROW_TILE % B == 0 and B % 8 == 0 and (B * mem_len) % ROW_TILE == 0
    assert S % (2 * ATTN_TK) == 0 and FFN_ROW_TILE % (B * B) == 0 and (S * B) % FFN_ROW_TILE == 0

    cos, sin = _rope_tables(positions.astype(F32)[..., None], B, S)

    rep = lambda a: jnp.repeat(a, SSM_GROUP, axis=1)
    b_t = lambda a: jnp.swapaxes(a, 2, 3).reshape(depth, SSM_WIDTH, SSM_STATE)
    abr, abi, bbr, bbi = _s5_prep(rep(ssm_lambda_re), rep(ssm_lambda_im), rep(ssm_log_step[..., None]),
                                  b_t(ssm_b_re), b_t(ssm_b_im))
    mem_rows = mem.reshape(B * mem_len, D_MODEL)
    swap = _swap_matrix(B)
    (w_in, ssm_w_glu, w_br_ssm, w_br_attn, w_br_lru, w_out, xattn_wq, xattn_wkv, xattn_wo, ffn_w_up,
     ffn_w_down) = (a.astype(BF16) for a in (w_in, ssm_w_glu, w_br_ssm, w_br_attn, w_br_lru, w_out, xattn_wq,
                                             xattn_wkv, xattn_wo, ffn_w_up, ffn_w_down))

    xs = x
    batch_major = True
    for l in range(depth):
        lambda_init = 0.8 - 0.6 * math.exp(-0.3 * l)
        u, q, k, v, xr, gr, gt = _inproj(xs, batch_major, _row(norm_mix_g[l]), _Layer(w_in, l), cos, sin, B, S)

        y_ssm = _s5(u, swap, _s5_b_weights(bbr[l], bbi[l]),
                    abr[l, ::SSM_GROUP].reshape(1, SSM_NSTATE), abi[l, ::SSM_GROUP].reshape(1, SSM_NSTATE),
                    _s5_c_weights(ssm_c_re[l]), _s5_c_weights(ssm_c_im[l]), _row(ssm_d[l]),
                    _Layer(ssm_w_glu, l), _row(ssm_b_glu[l]), B)

        wax = jnp.concatenate([_block_diag(lru_wa[l]), _block_diag(lru_wx[l])], axis=1).astype(BF16)
        y_lru = _lru(xr, gr, swap, lru_conv_w[l].astype(F32), _row(lru_conv_b[l]), wax, _row(lru_ba[l]),
                     _row(lru_bx[l]), _row(lru_lambda[l]), B)

        y_att = _diff_attn(q, k, v, _row(diff_lq1[l]), _row(diff_lk1[l]), _row(diff_lq2[l]), _row(diff_lk2[l]),
                           _row(diff_subln_g[l]), lambda_init, B, S)

        kv = _memkv(mem_rows, _row(norm_mem_g[l]), _Layer(xattn_wkv, l))
        xs = _merge_xattn(xs, batch_major, y_ssm, y_att, y_lru, gt, _Layer(w_br_ssm, l), _Layer(w_br_attn, l),
                          _Layer(w_br_lru, l), _Layer(w_out, l), _row(norm_xattn_g[l]), _Layer(xattn_wq, l), kv,
                          _Layer(xattn_wo, l), mem_len, B, S)
        batch_major = False

        xs = _ffn(xs, swap, _row(norm_ffn_g[l]), _Layer(ffn_w_up, l), ffn_conv_w[l].astype(F32),
                  _row(ffn_conv_b[l]), _Layer(ffn_w_down, l), _row(final_norm_g), B, S, final=l == depth - 1)

    return xs
```

```python
import collections
import functools
import math

import jax
import jax.numpy as jnp
from jax import lax
from jax.experimental import pallas as pl
from jax.experimental.pallas import tpu as pltpu

F32 = jnp.float32
BF16 = jnp.bfloat16

EPS = 1e-6
D_MODEL = 1024
SSM_WIDTH = 384
SSM_GROUP = 16
SSM_GROUPS = 24
SSM_STATE = 64
SSM_NSTATE = SSM_GROUPS * SSM_STATE
SSM_LANE_BLOCKS = 3
DIFF_HEADS = 4
DIFF_HD = 64
DIFF_W = 512
ROPE_THETA = 10000.0
LRU_WIDTH = 512
LRU_HEADS = 8
LRU_HD = 64
LRU_CONV = 4
LRU_C = 8.0
XATTN_HEADS = 4
XATTN_HD = 256
D_FF = 2816
FFN_CONV = 3
FFN_CHUNK = 256
FFN_NCHUNK = D_FF // FFN_CHUNK
OFF_U, OFF_Q, OFF_K, OFF_V, OFF_XR, OFF_GR, OFF_G = 0, 384, 896, 1408, 1920, 2432, 2944
D_IN = 6016

ROW_TILE = 512
FFN_ROW_TILE = 1024
REC_ROW_TILE = 1024
ATTN_TK = 512
ATTN_MIN_ROW_SUM = 1e-25
VMEM_LIMIT = 56 * 1024 * 1024


def _dot(a, b):
    return jnp.dot(a, b, preferred_element_type=F32)


def _dot_nt(a, b):
    return lax.dot_general(a, b, (((1,), (1,)), ((), ())), preferred_element_type=F32)


def _rms(x, g):
    ms = jnp.mean(x * x, axis=-1, keepdims=True)
    return x * lax.rsqrt(ms + EPS) * g


def _sigmoid(x):
    return 1.0 / (1.0 + jnp.exp(-x))


def _gelu(x):
    return 0.5 * x * (1.0 + jnp.tanh(0.7978845608028654 * (x + 0.044715 * (x * x * x))))


def _params(*sem):
    return pltpu.CompilerParams(dimension_semantics=sem, vmem_limit_bytes=VMEM_LIMIT)


def _whole(shape):
    zeros = (0,) * len(shape)
    return pl.BlockSpec(shape, lambda *_: zeros, pipeline_mode=pl.Buffered(1))


_Layer = collections.namedtuple("_Layer", "stack index")


def _wspec(w):
    if isinstance(w, _Layer):
        shape = w.stack.shape[1:]
        idx = (w.index,) + (0,) * len(shape)
        return pl.BlockSpec((None,) + shape, lambda *_: idx, pipeline_mode=pl.Buffered(1))
    return _whole(w.shape)


def _wop(w):
    return w.stack if isinstance(w, _Layer) else w


def _x_spec(batch_major, rows, width):
    if batch_major:
        return pl.BlockSpec((None, rows, width), lambda b, i: (b, i, 0))
    return pl.BlockSpec((rows, width), lambda b, i: (i, b))


def _rope_kernel(pos_ref, cos_ref, sin_ref):
    lane = lax.broadcasted_iota(jnp.int32, (1, 128), 1)
    j = (lane & 31).astype(F32)
    inv = jnp.exp((-math.log(ROPE_THETA) * (2.0 * j)) / DIFF_HD)
    ang = pos_ref[...] * inv
    first_half = (lane & 63) < 32
    cos_ref[...] = jnp.cos(ang)
    s = jnp.sin(ang)
    sin_ref[...] = jnp.where(first_half, -s, s)


def _rope_tables(pos_f, B, S):
    rows = ROW_TILE
    spec = pl.BlockSpec((None, rows, 128), lambda b, i: (b, i, 0))
    return pl.pallas_call(
        _rope_kernel,
        grid=(B, S // rows),
        in_specs=[pl.BlockSpec((None, rows, 1), lambda b, i: (b, i, 0))],
        out_specs=[spec, spec],
        out_shape=[jax.ShapeDtypeStruct((B, S, 128), F32)] * 2,
        compiler_params=_params("parallel", "parallel"),
        name="rope_tables",
    )(pos_f)


def _s5_prep_kernel(lr_ref, li_ref, ls_ref, br_ref, bi_ref, abr_ref, abi_ref, bbr_ref, bbi_ref):
    lr = lr_ref[...]
    li = li_ref[...]
    dt = jnp.exp(ls_ref[...])
    mag = jnp.exp(lr * dt)
    ab_r = mag * jnp.cos(li * dt)
    ab_i = mag * jnp.sin(li * dt)
    den = lr * lr + li * li
    nr = ab_r - 1.0
    f_r = (nr * lr + ab_i * li) / den
    f_i = (ab_i * lr - nr * li) / den
    br = br_ref[...]
    bi = bi_ref[...]
    abr_ref[...] = ab_r
    abi_ref[...] = ab_i
    bbr_ref[...] = f_r * br - f_i * bi
    bbi_ref[...] = f_r * bi + f_i * br


def _s5_prep(lr_rep, li_rep, ls_rep, br_t, bi_t):
    depth = lr_rep.shape[0]
    spec = pl.BlockSpec((None, SSM_WIDTH, SSM_STATE), lambda l: (l, 0, 0))
    return pl.pallas_call(
        _s5_prep_kernel,
        grid=(depth,),
        in_specs=[spec, spec, pl.BlockSpec((None, SSM_WIDTH, 1), lambda l: (l, 0, 0)), spec, spec],
        out_specs=[spec] * 4,
        out_shape=[jax.ShapeDtypeStruct((depth, SSM_WIDTH, SSM_STATE), F32)] * 4,
        compiler_params=_params("parallel"),
        name="s5_prep",
    )(lr_rep, li_rep, ls_rep, br_t, bi_t)


def _rope(x, cos, sin_signed, first_half):
    back = pltpu.roll(x, 32, 1)
    fwd = pltpu.roll(x, DIFF_W - 32, 1)
    return x * cos + jnp.where(first_half, fwd, back) * sin_signed


def _inproj_kernel(x_ref, g_ref, w_ref, cos_ref, sin_ref,
                   u_ref, q_ref, k_ref, v_ref, xr_ref, gr_ref, gt_ref):
    hn = _rms(x_ref[...], g_ref[...]).astype(BF16)

    def seg(off, width):
        return _dot(hn, w_ref[:, off:off + width])

    cos = jnp.concatenate([cos_ref[...]] * 4, axis=1)
    sin = jnp.concatenate([sin_ref[...]] * 4, axis=1)
    lane = lax.broadcasted_iota(jnp.int32, (1, DIFF_W), 1)
    first_half = (lane & 63) < 32

    u_ref[...] = seg(OFF_U, SSM_WIDTH).astype(BF16)
    q_ref[...] = (_rope(seg(OFF_Q, DIFF_W), cos, sin, first_half) * (DIFF_HD ** -0.5)).astype(BF16)
    k_ref[...] = _rope(seg(OFF_K, DIFF_W), cos, sin, first_half).astype(BF16)
    v_ref[...] = seg(OFF_V, DIFF_W).astype(BF16)
    xr_ref[...] = seg(OFF_XR, LRU_WIDTH).astype(BF16)
    gr_ref[...] = seg(OFF_GR, LRU_WIDTH).astype(BF16)
    for c in range(6):
        gt_ref[:, c * 512:(c + 1) * 512] = _sigmoid(seg(OFF_G + c * 512, 512)).astype(BF16)


def _inproj(x, batch_major, g, w_in, cos, sin, B, S):
    rows = ROW_TILE

    def out(width):
        return pl.BlockSpec((rows, width), lambda b, i: (i, b))

    widths = (SSM_WIDTH, DIFF_W, DIFF_W, DIFF_W, LRU_WIDTH, LRU_WIDTH, 3 * D_MODEL)
    tab = pl.BlockSpec((None, rows, 128), lambda b, i: (b, i, 0))
    return pl.pallas_call(
        _inproj_kernel,
        grid=(B, S // rows),
        in_specs=[_x_spec(batch_major, rows, D_MODEL), _whole((1, D_MODEL)), _wspec(w_in), tab, tab],
        out_specs=[out(w) for w in widths],
        out_shape=[jax.ShapeDtypeStruct((S, B * w), BF16) for w in widths],
        compiler_params=_params("parallel", "parallel"),
        name="inproj",
    )(x, g, _wop(w_in), cos, sin)


def _swap_matrix(batch):
    idx = jnp.arange(batch * batch)
    return (idx[:, None] == (idx[None, :] % batch) * batch + idx[None, :] // batch).astype(BF16)


def _load_time_major(ref, width, batch, steps, swap):
    subs = []
    for k in range(steps // batch):
        rows = slice(k * batch, (k + 1) * batch)
        xb = jnp.concatenate([ref[rows, b * width:(b + 1) * width] for b in range(batch)], axis=0)
        subs.append(_dot(swap, xb).astype(BF16))
    return jnp.concatenate(subs, axis=0)


def _store_batch_major(ref, y, width, batch, steps, swap):
    n = batch * batch
    for k in range(steps // batch):
        yb = _dot(swap, y[k * n:(k + 1) * n]).astype(BF16)
        for b in range(batch):
            ref[k * batch:(k + 1) * batch, b * width:(b + 1) * width] = yb[b * batch:(b + 1) * batch]


def _s5_kernel(u_ref, swap_ref, wb_ref, abr_ref, abi_ref, cre_ref, cim_ref, d_ref, wglu_ref, bglu_ref,
               o_ref, sre, sim, st_re, st_im, *, batch, steps):
    @pl.when(pl.program_id(0) == 0)
    def _():
        st_re[...] = jnp.zeros_like(st_re)
        st_im[...] = jnp.zeros_like(st_im)

    swap = swap_ref[...]
    u = _load_time_major(u_ref, SSM_WIDTH, batch, steps, swap)
    nb = SSM_NSTATE // SSM_LANE_BLOCKS
    for j in range(SSM_LANE_BLOCKS):
        bu = _dot(u[:, j * 128:(j + 1) * 128], wb_ref[j])
        sre[:, j * nb:(j + 1) * nb] = bu[:, :nb]
        sim[:, j * nb:(j + 1) * nb] = bu[:, nb:]

    for j in range(SSM_LANE_BLOCKS):
        cols = slice(j * nb, (j + 1) * nb)
        ar = jnp.broadcast_to(abr_ref[:, cols], (batch, nb))
        ai = jnp.broadcast_to(abi_ref[:, cols], (batch, nb))
        sr = st_re[:, cols]
        si = st_im[:, cols]
        for t in range(steps):
            rows = slice(t * batch, (t + 1) * batch)
            nr = ar * sr - ai * si + sre[rows, cols]
            ni = ar * si + ai * sr + sim[rows, cols]
            sre[rows, cols] = nr
            sim[rows, cols] = ni
            sr, si = nr, ni
        st_re[:, cols] = sr
        st_im[:, cols] = si

    ys = []
    for j in range(SSM_LANE_BLOCKS):
        cols = slice(j * nb, (j + 1) * nb)
        ys.append(_dot(sre[:, cols].astype(BF16), cre_ref[j]) - _dot(sim[:, cols].astype(BF16), cim_ref[j]))
    y = jnp.concatenate(ys, axis=1) + d_ref[...] * u.astype(F32)
    y = _gelu(y)
    z = _dot(y.astype(BF16), wglu_ref[...]) + bglu_ref[...]
    out = (z[:, :SSM_WIDTH] * _sigmoid(z[:, SSM_WIDTH:])).astype(BF16)
    _store_batch_major(o_ref, out, SSM_WIDTH, batch, steps, swap)


def _s5(u, swap, wb, abr, abi, cre, cim, d, wglu, bglu, B):
    S = u.shape[0]
    rows = REC_ROW_TILE
    steps = rows // B
    blk = pl.BlockSpec((steps, B * SSM_WIDTH), lambda i: (i, 0))
    return pl.pallas_call(
        functools.partial(_s5_kernel, batch=B, steps=steps),
        grid=(S // steps,),
        in_specs=[blk, _whole(swap.shape), _whole(wb.shape), _whole(abr.shape), _whole(abi.shape),
                  _whole(cre.shape), _whole(cim.shape), _whole(d.shape), _wspec(wglu), _whole(bglu.shape)],
        out_specs=blk,
        out_shape=jax.ShapeDtypeStruct((S, B * SSM_WIDTH), BF16),
        scratch_shapes=[pltpu.VMEM((rows, SSM_NSTATE), F32), pltpu.VMEM((rows, SSM_NSTATE), F32),
                        pltpu.VMEM((B, SSM_NSTATE), F32), pltpu.VMEM((B, SSM_NSTATE), F32)],
        compiler_params=_params("arbitrary"),
        name="s5_branch",
    )(u, swap, wb, abr, abi, cre, cim, d, _wop(wglu), bglu)


def _lru_kernel(xr_ref, gr_ref, swap_ref, cw_ref, cb_ref, wax_ref, ba_ref, bx_ref, lam_ref,
                o_ref, ext, a_s, b_s, h_st, *, batch, steps):
    rows = batch * steps
    halo = (LRU_CONV - 1) * batch

    @pl.when(pl.program_id(0) == 0)
    def _():
        ext[0:halo, :] = jnp.zeros((halo, LRU_WIDTH), F32)
        h_st[...] = jnp.zeros_like(h_st)

    swap = swap_ref[...]
    ext[halo:halo + rows, :] = _load_time_major(xr_ref, LRU_WIDTH, batch, steps, swap).astype(F32)
    xc = cb_ref[...] + cw_ref[LRU_CONV - 1:LRU_CONV, :] * ext[halo:halo + rows, :]
    for j in range(LRU_CONV - 1):
        xc = xc + cw_ref[j:j + 1, :] * ext[j * batch:j * batch + rows, :]
    ext[0:halo, :] = ext[rows:rows + halo, :]

    z = _dot(xc.astype(BF16), wax_ref[...])
    r = _sigmoid(z[:, :LRU_WIDTH] + ba_ref[...])
    ig = _sigmoid(z[:, LRU_WIDTH:] + bx_ref[...])
    softplus_neg_lam = jnp.log1p(jnp.exp(-lam_ref[...]))
    a = jnp.exp((-LRU_C) * r * softplus_neg_lam)
    a_s[...] = a
    y = 1.0 - a * a
    b_s[...] = jnp.where(y > 0.0, y * lax.rsqrt(y), 0.0) * (ig * xc)

    h = h_st[...]
    for t in range(steps):
        rs = slice(t * batch, (t + 1) * batch)
        h = a_s[rs, :] * h + b_s[rs, :]
        b_s[rs, :] = h
    h_st[...] = h
    gr = _load_time_major(gr_ref, LRU_WIDTH, batch, steps, swap).astype(F32)
    _store_batch_major(o_ref, (b_s[...] * _gelu(gr)).astype(BF16), LRU_WIDTH, batch, steps, swap)


def _lru(xr, gr, swap, cw, cb, wax, ba, bx, lam, B):
    S = xr.shape[0]
    rows = REC_ROW_TILE
    steps = rows // B
    blk = pl.BlockSpec((steps, B * LRU_WIDTH), lambda i: (i, 0))
    halo = (LRU_CONV - 1) * B
    return pl.pallas_call(
        functools.partial(_lru_kernel, batch=B, steps=steps),
        grid=(S // steps,),
        in_specs=[blk, blk, _whole(swap.shape), _whole(cw.shape), _whole(cb.shape), _whole(wax.shape),
                  _whole(ba.shape), _whole(bx.shape), _whole(lam.shape)],
        out_specs=blk,
        out_shape=jax.ShapeDtypeStruct((S, B * LRU_WIDTH), BF16),
        scratch_shapes=[pltpu.VMEM((rows + halo, LRU_WIDTH), F32), pltpu.VMEM((rows, LRU_WIDTH), F32),
                        pltpu.VMEM((rows, LRU_WIDTH), F32), pltpu.VMEM((B, LRU_WIDTH), F32)],
        compiler_params=_params("arbitrary"),
        name="rglru_branch",
    )(xr, gr, swap, cw, cb, wax, ba, bx, lam)


def _diff_attn_kernel(q_ref, k_ref, v_ref, lq1_ref, lk1_ref, lq2_ref, lk2_ref, g_ref, o_ref,
                      qq, m_s, acc_s, kn_s, *, lambda_init, tk):
    qi = pl.program_id(1)
    hw = 2 * DIFF_HD
    tq = 2 * tk
    every = slice(0, 2 * tq)
    upper = slice(tq, 2 * tq)
    lane = lax.broadcasted_iota(jnp.int32, (1, hw), 1)
    acc_s[...] = jnp.zeros_like(acc_s)
    for h in range(DIFF_HEADS):
        for half in range(2):
            q = q_ref[half * tk:(half + 1) * tk, h * hw:(h + 1) * hw]
            zero = jnp.zeros_like(q)
            qq[h, half * tq:half * tq + tk, :] = jnp.where(lane < DIFF_HD, q, zero)
            qq[h, half * tq + tk:(half + 1) * tq, :] = jnp.where(lane >= DIFF_HD, q, zero)

    @pl.when(qi == 0)
    def _():
        kn_s[...] = jnp.zeros_like(kn_s)

    new_keys = pl.multiple_of(qi * tq, tq)
    d_row = lax.broadcasted_iota(jnp.int32, (hw, 2 * hw), 0)
    d_col = lax.broadcasted_iota(jnp.int32, (hw, 2 * hw), 1)
    comp_sum = ((d_row < DIFF_HD) == (d_col < hw)).astype(BF16)
    bound = []
    for h in range(DIFF_HEADS):
        kb = k_ref[pl.ds(new_keys, tq), h * hw:(h + 1) * hw]
        kn = jnp.max(_dot(kb * kb, comp_sum), axis=0, keepdims=True)
        kn = jnp.maximum(kn_s[h:h + 1, :], kn)
        kn_s[h:h + 1, :] = kn
        q = q_ref[:, h * hw:(h + 1) * hw]
        qn = jnp.max(_dot(q * q, comp_sum), axis=0, keepdims=True)
        n2 = qn * kn
        b = n2 * lax.rsqrt(jnp.maximum(n2, 1e-30))
        bound.append((b[:, 0:1], b[:, hw:hw + 1]))

    def scores(j, h, rows, diagonal):
        start = pl.multiple_of(j * tk, tk)
        s = _dot_nt(qq[h, rows, :], k_ref[pl.ds(start, tk), h * hw:(h + 1) * hw])
        if diagonal:
            row = lax.broadcasted_iota(jnp.int32, s.shape, 0)
            col = lax.broadcasted_iota(jnp.int32, s.shape, 1)
            seen = col <= (row & (tk - 1))
            if s.shape[0] == 2 * tq:
                seen = seen | (row >= tq)
            s = jnp.where(seen, s, -1e30)
        return s

    chunks = [slice(c * hw, (c + 1) * hw) for c in range(tk // hw)]

    def max_step(j, rows, diagonal):
        for h in range(DIFF_HEADS):
            s = scores(j, h, rows, diagonal)
            m = m_s[h, rows, :]
            for c in chunks:
                m = jnp.maximum(m, s[:, c])
            m_s[h, rows, :] = m

    def shift_bound(h, rows, s):
        return jnp.concatenate([s[i * tk:(i + 1) * tk] - bound[h][i % 2] for i in range(s.shape[0] // tk)], axis=0)

    def shift_row_max(h, rows, s):
        m = m_s[h, rows, :]
        return jnp.concatenate([s[:, c] - m for c in chunks], axis=1)

    def acc_step(shift, j, rows, diagonal):
        start = pl.multiple_of(j * tk, tk)
        ones = jnp.ones((tk, hw), BF16)
        for h in range(DIFF_HEADS):
            p = jnp.exp(shift(h, rows, scores(j, h, rows, diagonal))).astype(BF16)
            v_ext = jnp.concatenate([v_ref[pl.ds(start, tk), h * hw:(h + 1) * hw], ones], axis=1)
            acc_s[h, rows, :] = acc_s[h, rows, :] + _dot(p, v_ext)

    def loop(step):
        def body(j, c):
            step(j, every, False)
            return c
        lax.fori_loop(0, 2 * qi, body, 0)
        step(2 * qi, every, True)
        step(2 * qi + 1, upper, True)

    loop(functools.partial(acc_step, shift_bound))
    row_sum_min = jnp.min(acc_s[0][:, hw:])
    for h in range(1, DIFF_HEADS):
        row_sum_min = jnp.minimum(row_sum_min, jnp.min(acc_s[h][:, hw:]))

    @pl.when(jnp.logical_not(row_sum_min >= ATTN_MIN_ROW_SUM))
    def _():
        m_s[...] = jnp.full(m_s.shape, -1e30, F32)
        acc_s[...] = jnp.zeros_like(acc_s)
        loop(max_step)
        for h in range(DIFF_HEADS):
            m_s[h] = jnp.broadcast_to(jnp.max(m_s[h], axis=-1, keepdims=True), (2 * tq, hw))
        loop(functools.partial(acc_step, shift_row_max))

    lam = (jnp.exp(jnp.sum(lq1_ref[...] * lk1_ref[...], axis=-1, keepdims=True))
           - jnp.exp(jnp.sum(lq2_ref[...] * lk2_ref[...], axis=-1, keepdims=True)) + lambda_init)
    for h in range(DIFF_HEADS):
        for half in range(2):
            acc = acc_s[h, half * tq:(half + 1) * tq, :]
            o = acc[:, :hw] / acc[:, hw:]
            o = o[:tk] - lam * o[tk:]
            o_ref[half * tk:(half + 1) * tk, h * hw:(h + 1) * hw] = (
                _rms(o, g_ref[...]) * (1.0 - lambda_init)).astype(BF16)


def _diff_attn(q, k, v, lq1, lk1, lq2, lk2, g, lambda_init, B, S):
    tk = ATTN_TK
    tq = 2 * tk
    hw = 2 * DIFF_HD
    qspec = pl.BlockSpec((tq, DIFF_W), lambda b, i: (i, b))
    kspec = pl.BlockSpec((S, DIFF_W), lambda b, i: (0, b))
    small = pl.BlockSpec((1, DIFF_HD), lambda b, i: (0, 0))
    return pl.pallas_call(
        functools.partial(_diff_attn_kernel, lambda_init=lambda_init, tk=tk),
        grid=(B, S // tq),
        in_specs=[qspec, kspec, kspec, small, small, small, small,
                  pl.BlockSpec((1, hw), lambda b, i: (0, 0))],
        out_specs=qspec,
        out_shape=jax.ShapeDtypeStruct((S, B * DIFF_W), BF16),
        scratch_shapes=[pltpu.VMEM((DIFF_HEADS, 2 * tq, hw), BF16), pltpu.VMEM((DIFF_HEADS, 2 * tq, hw), F32),
                        pltpu.VMEM((DIFF_HEADS, 2 * tq, 2 * hw), F32), pltpu.VMEM((DIFF_HEADS, 2 * hw), F32)],
        compiler_params=_params("parallel", "arbitrary"),
        name="diff_attn",
    )(q, k, v, lq1, lk1, lq2, lk2, g)


def _merge_xattn_kernel(x_ref, ys_ref, ya_ref, yl_ref, gt_ref, wbs_ref, wba_ref, wbl_ref, wout_ref,
                        g_ref, wq_ref, k_ref, v_ref, wo_ref, o_ref):
    d = D_MODEL
    m = gt_ref[:, 0:d].astype(F32) * _dot(ys_ref[...], wbs_ref[...])
    m = m + gt_ref[:, d:2 * d].astype(F32) * _dot(ya_ref[...], wba_ref[...])
    m = m + gt_ref[:, 2 * d:3 * d].astype(F32) * _dot(yl_ref[...], wbl_ref[...])
    x = x_ref[...] + _dot(m.astype(BF16), wout_ref[...])

    hn = _rms(x, g_ref[...]).astype(BF16)
    q = (_dot(hn, wq_ref[...]) * (XATTN_HD ** -0.5)).astype(BF16)
    outs = []
    for h in range(XATTN_HEADS):
        cols = slice(h * XATTN_HD, (h + 1) * XATTN_HD)
        s = _dot_nt(q[:, cols], k_ref[:, cols])
        p = jnp.exp(s - jnp.max(s, axis=-1, keepdims=True))
        p = p / jnp.sum(p, axis=-1, keepdims=True)
        outs.append(_dot(p.astype(BF16), v_ref[:, cols]).astype(BF16))
    o_ref[...] = x + _dot(jnp.concatenate(outs, axis=1), wo_ref[...])


def _merge_xattn(x, batch_major, ys, ya, yl, gt, wbs, wba, wbl, wout, g, wq, kv, wo, mem_len, B, S):
    rows = ROW_TILE

    def blk(width):
        return pl.BlockSpec((rows, width), lambda b, i: (i, b))

    return pl.pallas_call(
        _merge_xattn_kernel,
        grid=(B, S // rows),
        in_specs=[_x_spec(batch_major, rows, D_MODEL), blk(SSM_WIDTH), blk(DIFF_W), blk(LRU_WIDTH),
                  blk(3 * D_MODEL), _wspec(wbs), _wspec(wba), _wspec(wbl), _wspec(wout),
                  _whole(g.shape), _wspec(wq),
                  pl.BlockSpec((mem_len, D_MODEL), lambda b, i: (b, 0)),
                  pl.BlockSpec((mem_len, D_MODEL), lambda b, i: (b, 1)),
                  _wspec(wo)],
        out_specs=blk(D_MODEL),
        out_shape=jax.ShapeDtypeStruct((S, B * D_MODEL), F32),
        compiler_params=_params("parallel", "parallel"),
        name="merge_xattn",
    )(x, ys, ya, yl, gt, _wop(wbs), _wop(wba), _wop(wbl), _wop(wout), g, _wop(wq), kv, kv, _wop(wo))


def _memkv_kernel(m_ref, g_ref, w_ref, o_ref):
    hn = _rms(m_ref[...], g_ref[...]).astype(BF16)
    o_ref[...] = _dot(hn, w_ref[...]).astype(BF16)


def _memkv(mem_rows, g, wkv):
    n = mem_rows.shape[0]
    rows = ROW_TILE
    return pl.pallas_call(
        _memkv_kernel,
        grid=(n // rows,),
        in_specs=[pl.BlockSpec((rows, D_MODEL), lambda i: (i, 0)), _whole(g.shape), _wspec(wkv)],
        out_specs=pl.BlockSpec((rows, 2 * D_MODEL), lambda i: (i, 0)),
        out_shape=jax.ShapeDtypeStruct((n, 2 * D_MODEL), BF16),
        compiler_params=_params("parallel"),
        name="mem_kv",
    )(mem_rows, g, _wop(wkv))


def _ffn_kernel(x_ref, swap_ref, g_ref, wup_ref, cw_ref, cb_ref, wdn_ref, gf_ref, o_ref, prev, act_s,
                *, batch, steps, final):
    rows = batch * steps
    d = D_MODEL
    halo = (FFN_CONV - 1) * batch
    n = batch * batch

    @pl.when(pl.program_id(0) == 0)
    def _():
        prev[...] = jnp.zeros_like(prev)

    swap = swap_ref[...]
    g = g_ref[...]
    hn_b = [_rms(x_ref[:, b * d:(b + 1) * d], g).astype(BF16) for b in range(batch)]
    subs = []
    for k in range(steps // batch):
        hb = jnp.concatenate([h[k * batch:(k + 1) * batch] for h in hn_b], axis=0)
        subs.append(_dot(swap, hb).astype(BF16))
    hn = jnp.concatenate(subs, axis=0)

    def conv(up, part, cols):
        hist = prev[part]
        y = cb_ref[:, cols] + cw_ref[FFN_CONV - 1:FFN_CONV, cols] * up
        for t in range(FFN_CONV - 1):
            back = (FFN_CONV - 1 - t) * batch
            shifted = jnp.concatenate([hist[halo - back:], up[:rows - back]], axis=0)
            y = y + cw_ref[t:t + 1, cols] * shifted
        prev[part] = up[rows - halo:]
        return y

    for j in range(FFN_NCHUNK):
        vc = slice(j * FFN_CHUNK, (j + 1) * FFN_CHUNK)
        gc = slice(D_FF + j * FFN_CHUNK, D_FF + (j + 1) * FFN_CHUNK)
        val = conv(_dot(hn, wup_ref[:, vc]), 2 * j, vc)
        gate = conv(_dot(hn, wup_ref[:, gc]), 2 * j + 1, gc)
        act_s[:, vc] = (gate * _sigmoid(gate) * val).astype(BF16)
    acc = _dot(act_s[...], wdn_ref[...])

    hi = acc.astype(BF16)
    lo = (acc - hi.astype(F32)).astype(BF16)
    for k in range(steps // batch):
        yb = _dot(swap, hi[k * n:(k + 1) * n]) + _dot(swap, lo[k * n:(k + 1) * n])
        ts = slice(k * batch, (k + 1) * batch)
        for b in range(batch):
            out = x_ref[ts, b * d:(b + 1) * d] + yb[b * batch:(b + 1) * batch]
            if final:
                o_ref[b, ts, :] = _rms(out, gf_ref[...])
            else:
                o_ref[ts, b * d:(b + 1) * d] = out


def _ffn(x, swap, g, wup, cw, cb, wdn, gf, B, S, final):
    rows = FFN_ROW_TILE
    steps = rows // B
    blk = pl.BlockSpec((steps, B * D_MODEL), lambda i: (i, 0))
    if final:
        out_spec = pl.BlockSpec((B, steps, D_MODEL), lambda i: (0, i, 0))
        out_shape = jax.ShapeDtypeStruct((B, S, D_MODEL), F32)
    else:
        out_spec, out_shape = blk, jax.ShapeDtypeStruct((S, B * D_MODEL), F32)
    return pl.pallas_call(
        functools.partial(_ffn_kernel, batch=B, steps=steps, final=final),
        grid=(S // steps,),
        in_specs=[blk, _whole(swap.shape), _whole(g.shape), _wspec(wup), _whole(cw.shape), _whole(cb.shape),
                  _wspec(wdn), _whole(gf.shape)],
        out_specs=out_spec,
        out_shape=out_shape,
        scratch_shapes=[pltpu.VMEM((2 * FFN_NCHUNK, (FFN_CONV - 1) * B, FFN_CHUNK), F32),
                        pltpu.VMEM((rows, D_FF), BF16)],
        compiler_params=_params("arbitrary"),
        name="conv_ffn",
    )(x, swap, g, _wop(wup), cw, cb, _wop(wdn), gf)


def _row(v):
    return v.reshape(1, -1).astype(F32)


def _block_diag(blocks):
    n, r, c = blocks.shape
    eye = jnp.eye(n, dtype=blocks.dtype)
    return (blocks[:, :, None, :] * eye[:, None, :, None]).reshape(n * r, n * c)


def _s5_b_weights(bbr, bbi):
    per = SSM_GROUPS // SSM_LANE_BLOCKS
    out = []
    for j in range(SSM_LANE_BLOCKS):
        r = bbr[j * 128:(j + 1) * 128].reshape(per, SSM_GROUP, SSM_STATE)
        i = bbi[j * 128:(j + 1) * 128].reshape(per, SSM_GROUP, SSM_STATE)
        out.append(jnp.concatenate([_block_diag(r), _block_diag(i)], axis=1))
    return jnp.stack(out).astype(BF16)


def _s5_c_weights(c):
    per = SSM_GROUPS // SSM_LANE_BLOCKS
    ct = jnp.swapaxes(c, 1, 2)
    return jnp.stack([_block_diag(ct[j * per:(j + 1) * per]) for j in range(SSM_LANE_BLOCKS)]).astype(BF16)


def kernel(x, mem, positions, norm_mix_g, w_in, ssm_lambda_re, ssm_lambda_im, ssm_log_step, ssm_b_re, ssm_b_im, ssm_c_re, ssm_c_im, ssm_d, ssm_w_glu, ssm_b_glu, diff_lq1, diff_lk1, diff_lq2, diff_lk2, diff_subln_g, lru_conv_w, lru_conv_b, lru_wa, lru_ba, lru_wx, lru_bx, lru_lambda, w_br_ssm, w_br_attn, w_br_lru, w_out, norm_xattn_g, norm_mem_g, xattn_wq, xattn_wkv, xattn_wo, norm_ffn_g, ffn_w_up, ffn_conv_w, ffn_conv_b, ffn_w_down, final_norm_g):
    B, S, _ = x.shape
    depth = norm_mix_g.shape[0]
    mem_len = mem.shape[1]
    assert S % ROW_TILE == 0 and ROW_TILE % B == 0 and B % 8 == 0 and (B * mem_len) % ROW_TILE == 0
    assert S % (2 * ATTN_TK) == 0 and FFN_ROW_TILE % (B * B) == 0 and (S * B) % FFN_ROW_TILE == 0

    cos, sin = _rope_tables(positions.astype(F32)[..., None], B, S)

    rep = lambda a: jnp.repeat(a, SSM_GROUP, axis=1)
    b_t = lambda a: jnp.swapaxes(a, 2, 3).reshape(depth, SSM_WIDTH, SSM_STATE)
    abr, abi, bbr, bbi = _s5_prep(rep(ssm_lambda_re), rep(ssm_lambda_im), rep(ssm_log_step[..., None]),
                                  b_t(ssm_b_re), b_t(ssm_b_im))
    mem_rows = mem.reshape(B * mem_len, D_MODEL)
    swap = _swap_matrix(B)
    (w_in, ssm_w_glu, w_br_ssm, w_br_attn, w_br_lru, w_out, xattn_wq, xattn_wkv, xattn_wo, ffn_w_up,
     ffn_w_down) = (a.astype(BF16) for a in (w_in, ssm_w_glu, w_br_ssm, w_br_attn, w_br_lru, w_out, xattn_wq,
                                             xattn_wkv, xattn_wo, ffn_w_up, ffn_w_down))

    xs = x
    batch_major = True
    for l in range(depth):
        lambda_init = 0.8 - 0.6 * math.exp(-0.3 * l)
        u, q, k, v, xr, gr, gt = _inproj(xs, batch_major, _row(norm_mix_g[l]), _Layer(w_in, l), cos, sin, B, S)

        y_ssm = _s5(u, swap, _s5_b_weights(bbr[l], bbi[l]),
                    abr[l, ::SSM_GROUP].reshape(1, SSM_NSTATE), abi[l, ::SSM_GROUP].reshape(1, SSM_NSTATE),
                    _s5_c_weights(ssm_c_re[l]), _s5_c_weights(ssm_c_im[l]), _row(ssm_d[l]),
                    _Layer(ssm_w_glu, l), _row(ssm_b_glu[l]), B)

        wax = jnp.concatenate([_block_diag(lru_wa[l]), _block_diag(lru_wx[l])], axis=1).astype(BF16)
        y_lru = _lru(xr, gr, swap, lru_conv_w[l].astype(F32), _row(lru_conv_b[l]), wax, _row(lru_ba[l]),
                     _row(lru_bx[l]), _row(lru_lambda[l]), B)

        y_att = _diff_attn(q, k, v, _row(diff_lq1[l]), _row(diff_lk1[l]), _row(diff_lq2[l]), _row(diff_lk2[l]),
                           _row(diff_subln_g[l]), lambda_init, B, S)

        kv = _memkv(mem_rows, _row(norm_mem_g[l]), _Layer(xattn_wkv, l))
        xs = _merge_xattn(xs, batch_major, y_ssm, y_att, y_lru, gt, _Layer(w_br_ssm, l), _Layer(w_br_attn, l),
                          _Layer(w_br_lru, l), _Layer(w_out, l), _row(norm_xattn_g[l]), _Layer(xattn_wq, l), kv,
                          _Layer(xattn_wo, l), mem_len, B, S)
        batch_major = False

        xs = _ffn(xs, swap, _row(norm_ffn_g[l]), _Layer(ffn_w_up, l), ffn_conv_w[l].astype(F32),
                  _row(ffn_conv_b[l]), _Layer(ffn_w_down, l), _row(final_norm_g), B, S, final=l == depth - 1)

    return xs
```

```python
import collections
import functools
import math

import jax
import jax.numpy as jnp
from jax import lax
from jax.experimental import pallas as pl
from jax.experimental.pallas import tpu as pltpu

F32 = jnp.float32
BF16 = jnp.bfloat16

EPS = 1e-6
D_MODEL = 1024
SSM_WIDTH = 384
SSM_GROUP = 16
SSM_GROUPS = 24
SSM_STATE = 64
SSM_NSTATE = SSM_GROUPS * SSM_STATE
SSM_LANE_BLOCKS = 3
DIFF_HEADS = 4
DIFF_HD = 64
DIFF_W = 512
ROPE_THETA = 10000.0
LRU_WIDTH = 512
LRU_HEADS = 8
LRU_HD = 64
LRU_CONV = 4
LRU_C = 8.0
XATTN_HEADS = 4
XATTN_HD = 256
D_FF = 2816
FFN_CONV = 3
FFN_CHUNK = 256
FFN_NCHUNK = D_FF // FFN_CHUNK
OFF_U, OFF_Q, OFF_K, OFF_V, OFF_XR, OFF_GR, OFF_G = 0, 384, 896, 1408, 1920, 2432, 2944
D_IN = 6016

ROW_TILE = 512
FFN_ROW_TILE = 1024
REC_ROW_TILE = 1024
ATTN_TK = 512
ATTN_NORM_ROWS = 128
ATTN_MIN_ROW_SUM = 1e-25
VMEM_LIMIT = 56 * 1024 * 1024


def _dot(a, b):
    return jnp.dot(a, b, preferred_element_type=F32)


def _dot_nt(a, b):
    return lax.dot_general(a, b, (((1,), (1,)), ((), ())), preferred_element_type=F32)


def _rms(x, g):
    ms = jnp.mean(x * x, axis=-1, keepdims=True)
    return x * lax.rsqrt(ms + EPS) * g


def _sigmoid(x):
    return 1.0 / (1.0 + jnp.exp(-x))


def _gelu(x):
    return 0.5 * x * (1.0 + jnp.tanh(0.7978845608028654 * (x + 0.044715 * (x * x * x))))


def _params(*sem):
    return pltpu.CompilerParams(dimension_semantics=sem, vmem_limit_bytes=VMEM_LIMIT)


def _whole(shape):
    zeros = (0,) * len(shape)
    return pl.BlockSpec(shape, lambda *_: zeros, pipeline_mode=pl.Buffered(1))


_Layer = collections.namedtuple("_Layer", "stack index")


def _wspec(w):
    if isinstance(w, _Layer):
        shape = w.stack.shape[1:]
        idx = (w.index,) + (0,) * len(shape)
        return pl.BlockSpec((None,) + shape, lambda *_: idx, pipeline_mode=pl.Buffered(1))
    return _whole(w.shape)


def _wop(w):
    return w.stack if isinstance(w, _Layer) else w


def _x_spec(batch_major, rows, width):
    if batch_major:
        return pl.BlockSpec((None, rows, width), lambda b, i: (b, i, 0))
    return pl.BlockSpec((rows, width), lambda b, i: (i, b))


def _rope_kernel(pos_ref, cos_ref, sin_ref):
    lane = lax.broadcasted_iota(jnp.int32, (1, 128), 1)
    j = (lane & 31).astype(F32)
    inv = jnp.exp((-math.log(ROPE_THETA) * (2.0 * j)) / DIFF_HD)
    ang = pos_ref[...] * inv
    first_half = (lane & 63) < 32
    cos_ref[...] = jnp.cos(ang)
    s = jnp.sin(ang)
    sin_ref[...] = jnp.where(first_half, -s, s)


def _rope_tables(pos_f, B, S):
    rows = ROW_TILE
    spec = pl.BlockSpec((None, rows, 128), lambda b, i: (b, i, 0))
    return pl.pallas_call(
        _rope_kernel,
        grid=(B, S // rows),
        in_specs=[pl.BlockSpec((None, rows, 1), lambda b, i: (b, i, 0))],
        out_specs=[spec, spec],
        out_shape=[jax.ShapeDtypeStruct((B, S, 128), F32)] * 2,
        compiler_params=_params("parallel", "parallel"),
        name="rope_tables",
    )(pos_f)


def _s5_prep_kernel(lr_ref, li_ref, ls_ref, br_ref, bi_ref, abr_ref, abi_ref, bbr_ref, bbi_ref):
    lr = lr_ref[...]
    li = li_ref[...]
    dt = jnp.exp(ls_ref[...])
    mag = jnp.exp(lr * dt)
    ab_r = mag * jnp.cos(li * dt)
    ab_i = mag * jnp.sin(li * dt)
    den = lr * lr + li * li
    nr = ab_r - 1.0
    f_r = (nr * lr + ab_i * li) / den
    f_i = (ab_i * lr - nr * li) / den
    br = br_ref[...]
    bi = bi_ref[...]
    abr_ref[...] = ab_r
    abi_ref[...] = ab_i
    bbr_ref[...] = f_r * br - f_i * bi
    bbi_ref[...] = f_r * bi + f_i * br


def _s5_prep(lr_rep, li_rep, ls_rep, br_t, bi_t):
    depth = lr_rep.shape[0]
    spec = pl.BlockSpec((None, SSM_WIDTH, SSM_STATE), lambda l: (l, 0, 0))
    return pl.pallas_call(
        _s5_prep_kernel,
        grid=(depth,),
        in_specs=[spec, spec, pl.BlockSpec((None, SSM_WIDTH, 1), lambda l: (l, 0, 0)), spec, spec],
        out_specs=[spec] * 4,
        out_shape=[jax.ShapeDtypeStruct((depth, SSM_WIDTH, SSM_STATE), F32)] * 4,
        compiler_params=_params("parallel"),
        name="s5_prep",
    )(lr_rep, li_rep, ls_rep, br_t, bi_t)


def _rope(x, cos, sin_signed, first_half):
    back = pltpu.roll(x, 32, 1)
    fwd = pltpu.roll(x, DIFF_W - 32, 1)
    return x * cos + jnp.where(first_half, fwd, back) * sin_signed


def _inproj_kernel(x_ref, g_ref, w_ref, cos_ref, sin_ref,
                   u_ref, q_ref, k_ref, v_ref, xr_ref, gr_ref, gt_ref):
    hn = _rms(x_ref[...], g_ref[...]).astype(BF16)

    def seg(off, width):
        return _dot(hn, w_ref[:, off:off + width])

    cos = jnp.concatenate([cos_ref[...]] * 4, axis=1)
    sin = jnp.concatenate([sin_ref[...]] * 4, axis=1)
    lane = lax.broadcasted_iota(jnp.int32, (1, DIFF_W), 1)
    first_half = (lane & 63) < 32

    u_ref[...] = seg(OFF_U, SSM_WIDTH).astype(BF16)
    q_ref[...] = (_rope(seg(OFF_Q, DIFF_W), cos, sin, first_half) * (DIFF_HD ** -0.5)).astype(BF16)
    k_ref[...] = _rope(seg(OFF_K, DIFF_W), cos, sin, first_half).astype(BF16)
    v_ref[...] = seg(OFF_V, DIFF_W).astype(BF16)
    xr_ref[...] = seg(OFF_XR, LRU_WIDTH).astype(BF16)
    gr_ref[...] = seg(OFF_GR, LRU_WIDTH).astype(BF16)
    for c in range(6):
        gt_ref[:, c * 512:(c + 1) * 512] = _sigmoid(seg(OFF_G + c * 512, 512)).astype(BF16)


def _inproj(x, batch_major, g, w_in, cos, sin, B, S):
    rows = ROW_TILE

    def out(width):
        return pl.BlockSpec((rows, width), lambda b, i: (i, b))

    widths = (SSM_WIDTH, DIFF_W, DIFF_W, DIFF_W, LRU_WIDTH, LRU_WIDTH, 3 * D_MODEL)
    tab = pl.BlockSpec((None, rows, 128), lambda b, i: (b, i, 0))
    return pl.pallas_call(
        _inproj_kernel,
        grid=(B, S // rows),
        in_specs=[_x_spec(batch_major, rows, D_MODEL), _whole((1, D_MODEL)), _wspec(w_in), tab, tab],
        out_specs=[out(w) for w in widths],
        out_shape=[jax.ShapeDtypeStruct((S, B * w), BF16) for w in widths],
        compiler_params=_params("parallel", "parallel"),
        name="inproj",
    )(x, g, _wop(w_in), cos, sin)


def _swap_matrix(batch):
    idx = jnp.arange(batch * batch)
    return (idx[:, None] == (idx[None, :] % batch) * batch + idx[None, :] // batch).astype(BF16)


def _load_time_major(ref, width, batch, steps, swap):
    subs = []
    for k in range(steps // batch):
        rows = slice(k * batch, (k + 1) * batch)
        xb = jnp.concatenate([ref[rows, b * width:(b + 1) * width] for b in range(batch)], axis=0)
        subs.append(_dot(swap, xb).astype(BF16))
    return jnp.concatenate(subs, axis=0)


def _store_batch_major(ref, y, width, batch, steps, swap):
    n = batch * batch
    for k in range(steps // batch):
        yb = _dot(swap, y[k * n:(k + 1) * n]).astype(BF16)
        for b in range(batch):
            ref[k * batch:(k + 1) * batch, b * width:(b + 1) * width] = yb[b * batch:(b + 1) * batch]


def _s5_kernel(u_ref, swap_ref, wb_ref, abr_ref, abi_ref, cre_ref, cim_ref, d_ref, wglu_ref, bglu_ref,
               o_ref, sre, sim, st_re, st_im, *, batch, steps):
    @pl.when(pl.program_id(0) == 0)
    def _():
        st_re[...] = jnp.zeros_like(st_re)
        st_im[...] = jnp.zeros_like(st_im)

    swap = swap_ref[...]
    u = _load_time_major(u_ref, SSM_WIDTH, batch, steps, swap)
    nb = SSM_NSTATE // SSM_LANE_BLOCKS
    for j in range(SSM_LANE_BLOCKS):
        bu = _dot(u[:, j * 128:(j + 1) * 128], wb_ref[j])
        sre[:, j * nb:(j + 1) * nb] = bu[:, :nb]
        sim[:, j * nb:(j + 1) * nb] = bu[:, nb:]

    for j in range(SSM_LANE_BLOCKS):
        cols = slice(j * nb, (j + 1) * nb)
        ar = jnp.broadcast_to(abr_ref[:, cols], (batch, nb))
        ai = jnp.broadcast_to(abi_ref[:, cols], (batch, nb))
        sr = st_re[:, cols]
        si = st_im[:, cols]
        for t in range(steps):
            rows = slice(t * batch, (t + 1) * batch)
            nr = ar * sr - ai * si + sre[rows, cols]
            ni = ar * si + ai * sr + sim[rows, cols]
            sre[rows, cols] = nr
            sim[rows, cols] = ni
            sr, si = nr, ni
        st_re[:, cols] = sr
        st_im[:, cols] = si

    ys = []
    for j in range(SSM_LANE_BLOCKS):
        cols = slice(j * nb, (j + 1) * nb)
        ys.append(_dot(sre[:, cols].astype(BF16), cre_ref[j]) - _dot(sim[:, cols].astype(BF16), cim_ref[j]))
    y = jnp.concatenate(ys, axis=1) + d_ref[...] * u.astype(F32)
    y = _gelu(y)
    z = _dot(y.astype(BF16), wglu_ref[...]) + bglu_ref[...]
    out = (z[:, :SSM_WIDTH] * _sigmoid(z[:, SSM_WIDTH:])).astype(BF16)
    _store_batch_major(o_ref, out, SSM_WIDTH, batch, steps, swap)


def _s5(u, swap, wb, abr, abi, cre, cim, d, wglu, bglu, B):
    S = u.shape[0]
    rows = REC_ROW_TILE
    steps = rows // B
    blk = pl.BlockSpec((steps, B * SSM_WIDTH), lambda i: (i, 0))
    return pl.pallas_call(
        functools.partial(_s5_kernel, batch=B, steps=steps),
        grid=(S // steps,),
        in_specs=[blk, _whole(swap.shape), _whole(wb.shape), _whole(abr.shape), _whole(abi.shape),
                  _whole(cre.shape), _whole(cim.shape), _whole(d.shape), _wspec(wglu), _whole(bglu.shape)],
        out_specs=blk,
        out_shape=jax.ShapeDtypeStruct((S, B * SSM_WIDTH), BF16),
        scratch_shapes=[pltpu.VMEM((rows, SSM_NSTATE), F32), pltpu.VMEM((rows, SSM_NSTATE), F32),
                        pltpu.VMEM((B, SSM_NSTATE), F32), pltpu.VMEM((B, SSM_NSTATE), F32)],
        compiler_params=_params("arbitrary"),
        name="s5_branch",
    )(u, swap, wb, abr, abi, cre, cim, d, _wop(wglu), bglu)


def _lru_kernel(xr_ref, gr_ref, swap_ref, cw_ref, cb_ref, wax_ref, ba_ref, bx_ref, lam_ref,
                o_ref, ext, a_s, b_s, h_st, *, batch, steps):
    rows = batch * steps
    halo = (LRU_CONV - 1) * batch

    @pl.when(pl.program_id(0) == 0)
    def _():
        ext[0:halo, :] = jnp.zeros((halo, LRU_WIDTH), F32)
        h_st[...] = jnp.zeros_like(h_st)

    swap = swap_ref[...]
    ext[halo:halo + rows, :] = _load_time_major(xr_ref, LRU_WIDTH, batch, steps, swap).astype(F32)
    xc = cb_ref[...] + cw_ref[LRU_CONV - 1:LRU_CONV, :] * ext[halo:halo + rows, :]
    for j in range(LRU_CONV - 1):
        xc = xc + cw_ref[j:j + 1, :] * ext[j * batch:j * batch + rows, :]
    ext[0:halo, :] = ext[rows:rows + halo, :]

    z = _dot(xc.astype(BF16), wax_ref[...])
    r = _sigmoid(z[:, :LRU_WIDTH] + ba_ref[...])
    ig = _sigmoid(z[:, LRU_WIDTH:] + bx_ref[...])
    softplus_neg_lam = jnp.log1p(jnp.exp(-lam_ref[...]))
    a = jnp.exp((-LRU_C) * r * softplus_neg_lam)
    a_s[...] = a
    y = 1.0 - a * a
    b_s[...] = jnp.where(y > 0.0, y * lax.rsqrt(y), 0.0) * (ig * xc)

    h = h_st[...]
    for t in range(steps):
        rs = slice(t * batch, (t + 1) * batch)
        h = a_s[rs, :] * h + b_s[rs, :]
        b_s[rs, :] = h
    h_st[...] = h
    gr = _load_time_major(gr_ref, LRU_WIDTH, batch, steps, swap).astype(F32)
    _store_batch_major(o_ref, (b_s[...] * _gelu(gr)).astype(BF16), LRU_WIDTH, batch, steps, swap)


def _lru(xr, gr, swap, cw, cb, wax, ba, bx, lam, B):
    S = xr.shape[0]
    rows = REC_ROW_TILE
    steps = rows // B
    blk = pl.BlockSpec((steps, B * LRU_WIDTH), lambda i: (i, 0))
    halo = (LRU_CONV - 1) * B
    return pl.pallas_call(
        functools.partial(_lru_kernel, batch=B, steps=steps),
        grid=(S // steps,),
        in_specs=[blk, blk, _whole(swap.shape), _whole(cw.shape), _whole(cb.shape), _whole(wax.shape),
                  _whole(ba.shape), _whole(bx.shape), _whole(lam.shape)],
        out_specs=blk,
        out_shape=jax.ShapeDtypeStruct((S, B * LRU_WIDTH), BF16),
        scratch_shapes=[pltpu.VMEM((rows + halo, LRU_WIDTH), F32), pltpu.VMEM((rows, LRU_WIDTH), F32),
                        pltpu.VMEM((rows, LRU_WIDTH), F32), pltpu.VMEM((B, LRU_WIDTH), F32)],
        compiler_params=_params("arbitrary"),
        name="rglru_branch",
    )(xr, gr, swap, cw, cb, wax, ba, bx, lam)


def _diff_attn_kernel(q_ref, k_ref, v_ref, lq1_ref, lk1_ref, lq2_ref, lk2_ref, g_ref, o_ref,
                      qq, m_s, acc_s, kn_s, *, lambda_init, tk):
    qi = pl.program_id(1)
    hw = 2 * DIFF_HD
    tq = 2 * tk
    every = slice(0, 2 * tq)
    upper = slice(tq, 2 * tq)
    lane = lax.broadcasted_iota(jnp.int32, (1, hw), 1)

    @pl.when((pl.program_id(0) == 0) & (qi == 0))
    def _():
        acc_s[...] = jnp.zeros_like(acc_s)

    for h in range(DIFF_HEADS):
        for half in range(2):
            q = q_ref[half * tk:(half + 1) * tk, h * hw:(h + 1) * hw]
            zero = jnp.zeros_like(q)
            qq[h, half * tq:half * tq + tk, :] = jnp.where(lane < DIFF_HD, q, zero)
            qq[h, half * tq + tk:(half + 1) * tq, :] = jnp.where(lane >= DIFF_HD, q, zero)

    @pl.when(qi == 0)
    def _():
        kn_s[...] = jnp.zeros_like(kn_s)

    new_keys = pl.multiple_of(qi * tq, tq)
    d_row = lax.broadcasted_iota(jnp.int32, (hw, 2 * hw), 0)
    d_col = lax.broadcasted_iota(jnp.int32, (hw, 2 * hw), 1)
    comp_sum = ((d_row < DIFF_HD) == (d_col < hw)).astype(BF16)

    def max_sq_norm(x):
        sq = x * x
        fold = sq[0:ATTN_NORM_ROWS]
        for r in range(ATTN_NORM_ROWS, tq, ATTN_NORM_ROWS):
            fold = jnp.maximum(fold, sq[r:r + ATTN_NORM_ROWS])
        return jnp.max(_dot(fold, comp_sum), axis=0, keepdims=True)

    bound = []
    for h in range(DIFF_HEADS):
        kn = jnp.maximum(kn_s[h:h + 1, :], max_sq_norm(k_ref[pl.ds(new_keys, tq), h * hw:(h + 1) * hw]))
        kn_s[h:h + 1, :] = kn
        qn = max_sq_norm(q_ref[:, h * hw:(h + 1) * hw])
        n2 = qn * kn
        b = n2 * lax.rsqrt(jnp.maximum(n2, 1e-30))
        bound.append((b[:, 0:1], b[:, hw:hw + 1]))

    def scores(j, h, rows, diagonal):
        start = pl.multiple_of(j * tk, tk)
        s = _dot_nt(qq[h, rows, :], k_ref[pl.ds(start, tk), h * hw:(h + 1) * hw])
        if diagonal:
            row = lax.broadcasted_iota(jnp.int32, s.shape, 0)
            col = lax.broadcasted_iota(jnp.int32, s.shape, 1)
            seen = col <= (row & (tk - 1))
            if s.shape[0] == 2 * tq:
                seen = seen | (row >= tq)
            s = jnp.where(seen, s, -1e30)
        return s

    chunks = [slice(c * hw, (c + 1) * hw) for c in range(tk // hw)]

    def max_step(j, rows, diagonal):
        for h in range(DIFF_HEADS):
            s = scores(j, h, rows, diagonal)
            m = m_s[h, rows, :]
            for c in chunks:
                m = jnp.maximum(m, s[:, c])
            m_s[h, rows, :] = m

    def shift_bound(h, rows, s):
        return jnp.concatenate([s[i * tk:(i + 1) * tk] - bound[h][i % 2] for i in range(s.shape[0] // tk)], axis=0)

    def shift_row_max(h, rows, s):
        m = m_s[h, rows, :]
        return jnp.concatenate([s[:, c] - m for c in chunks], axis=1)

    def acc_step(shift, j, rows, diagonal):
        start = pl.multiple_of(j * tk, tk)
        ones = jnp.ones((tk, hw), BF16)
        for h in range(DIFF_HEADS):
            p = jnp.exp(shift(h, rows, scores(j, h, rows, diagonal))).astype(BF16)
            v_ext = jnp.concatenate([v_ref[pl.ds(start, tk), h * hw:(h + 1) * hw], ones], axis=1)
            acc_s[h, rows, :] = acc_s[h, rows, :] + _dot(p, v_ext)

    def loop(step):
        def body(j, c):
            step(j, every, False)
            return c
        lax.fori_loop(0, 2 * qi, body, 0)
        step(2 * qi, every, True)
        step(2 * qi + 1, upper, True)

    loop(functools.partial(acc_step, shift_bound))
    row_sum_min = jnp.min(acc_s[0][:, hw:])
    for h in range(1, DIFF_HEADS):
        row_sum_min = jnp.minimum(row_sum_min, jnp.min(acc_s[h][:, hw:]))

    @pl.when(jnp.logical_not(row_sum_min >= ATTN_MIN_ROW_SUM))
    def _():
        m_s[...] = jnp.full(m_s.shape, -1e30, F32)
        acc_s[...] = jnp.zeros_like(acc_s)
        loop(max_step)
        for h in range(DIFF_HEADS):
            m_s[h] = jnp.broadcast_to(jnp.max(m_s[h], axis=-1, keepdims=True), (2 * tq, hw))
        loop(functools.partial(acc_step, shift_row_max))

    lam = (jnp.exp(jnp.sum(lq1_ref[...] * lk1_ref[...], axis=-1, keepdims=True))
           - jnp.exp(jnp.sum(lq2_ref[...] * lk2_ref[...], axis=-1, keepdims=True)) + lambda_init)
    for h in range(DIFF_HEADS):
        for half in range(2):
            acc = acc_s[h, half * tq:(half + 1) * tq, :]
            acc_s[h, half * tq:(half + 1) * tq, :] = jnp.zeros_like(acc)
            o = acc[:, :hw] / acc[:, hw:]
            o = o[:tk] - lam * o[tk:]
            o_ref[half * tk:(half + 1) * tk, h * hw:(h + 1) * hw] = (
                _rms(o, g_ref[...]) * (1.0 - lambda_init)).astype(BF16)


def _diff_attn(q, k, v, lq1, lk1, lq2, lk2, g, lambda_init, B, S):
    tk = ATTN_TK
    tq = 2 * tk
    hw = 2 * DIFF_HD
    qspec = pl.BlockSpec((tq, DIFF_W), lambda b, i: (i, b))
    kspec = pl.BlockSpec((S, DIFF_W), lambda b, i: (0, b))
    small = pl.BlockSpec((1, DIFF_HD), lambda b, i: (0, 0))
    return pl.pallas_call(
        functools.partial(_diff_attn_kernel, lambda_init=lambda_init, tk=tk),
        grid=(B, S // tq),
        in_specs=[qspec, kspec, kspec, small, small, small, small,
                  pl.BlockSpec((1, hw), lambda b, i: (0, 0))],
        out_specs=qspec,
        out_shape=jax.ShapeDtypeStruct((S, B * DIFF_W), BF16),
        scratch_shapes=[pltpu.VMEM((DIFF_HEADS, 2 * tq, hw), BF16), pltpu.VMEM((DIFF_HEADS, 2 * tq, hw), F32),
                        pltpu.VMEM((DIFF_HEADS, 2 * tq, 2 * hw), F32), pltpu.VMEM((DIFF_HEADS, 2 * hw), F32)],
        compiler_params=_params("arbitrary", "arbitrary"),
        name="diff_attn",
    )(q, k, v, lq1, lk1, lq2, lk2, g)


def _merge_xattn_kernel(x_ref, ys_ref, ya_ref, yl_ref, gt_ref, wbs_ref, wba_ref, wbl_ref, wout_ref,
                        g_ref, wq_ref, k_ref, v_ref, wo_ref, o_ref):
    d = D_MODEL
    m = gt_ref[:, 0:d].astype(F32) * _dot(ys_ref[...], wbs_ref[...])
    m = m + gt_ref[:, d:2 * d].astype(F32) * _dot(ya_ref[...], wba_ref[...])
    m = m + gt_ref[:, 2 * d:3 * d].astype(F32) * _dot(yl_ref[...], wbl_ref[...])
    x = x_ref[...] + _dot(m.astype(BF16), wout_ref[...])

    hn = _rms(x, g_ref[...]).astype(BF16)
    q = (_dot(hn, wq_ref[...]) * (XATTN_HD ** -0.5)).astype(BF16)
    outs = []
    for h in range(XATTN_HEADS):
        cols = slice(h * XATTN_HD, (h + 1) * XATTN_HD)
        s = _dot_nt(q[:, cols], k_ref[:, cols])
        p = jnp.exp(s - jnp.max(s, axis=-1, keepdims=True))
        p = p / jnp.sum(p, axis=-1, keepdims=True)
        outs.append(_dot(p.astype(BF16), v_ref[:, cols]).astype(BF16))
    o_ref[...] = x + _dot(jnp.concatenate(outs, axis=1), wo_ref[...])


def _merge_xattn(x, batch_major, ys, ya, yl, gt, wbs, wba, wbl, wout, g, wq, kv, wo, mem_len, B, S):
    rows = ROW_TILE

    def blk(width):
        return pl.BlockSpec((rows, width), lambda b, i: (i, b))

    return pl.pallas_call(
        _merge_xattn_kernel,
        grid=(B, S // rows),
        in_specs=[_x_spec(batch_major, rows, D_MODEL), blk(SSM_WIDTH), blk(DIFF_W), blk(LRU_WIDTH),
                  blk(3 * D_MODEL), _wspec(wbs), _wspec(wba), _wspec(wbl), _wspec(wout),
                  _whole(g.shape), _wspec(wq),
                  pl.BlockSpec((mem_len, D_MODEL), lambda b, i: (b, 0)),
                  pl.BlockSpec((mem_len, D_MODEL), lambda b, i: (b, 1)),
                  _wspec(wo)],
        out_specs=blk(D_MODEL),
        out_shape=jax.ShapeDtypeStruct((S, B * D_MODEL), F32),
        compiler_params=_params("parallel", "parallel"),
        name="merge_xattn",
    )(x, ys, ya, yl, gt, _wop(wbs), _wop(wba), _wop(wbl), _wop(wout), g, _wop(wq), kv, kv, _wop(wo))


def _memkv_kernel(m_ref, g_ref, w_ref, o_ref):
    hn = _rms(m_ref[...], g_ref[...]).astype(BF16)
    o_ref[...] = _dot(hn, w_ref[...]).astype(BF16)


def _memkv(mem_rows, g, wkv):
    n = mem_rows.shape[0]
    rows = ROW_TILE
    return pl.pallas_call(
        _memkv_kernel,
        grid=(n // rows,),
        in_specs=[pl.BlockSpec((rows, D_MODEL), lambda i: (i, 0)), _whole(g.shape), _wspec(wkv)],
        out_specs=pl.BlockSpec((rows, 2 * D_MODEL), lambda i: (i, 0)),
        out_shape=jax.ShapeDtypeStruct((n, 2 * D_MODEL), BF16),
        compiler_params=_params("parallel"),
        name="mem_kv",
    )(mem_rows, g, _wop(wkv))


def _ffn_kernel(x_ref, swap_ref, g_ref, wup_ref, cw_ref, cb_ref, wdn_ref, gf_ref, o_ref, prev, act_s,
                *, batch, steps, final):
    rows = batch * steps
    d = D_MODEL
    halo = (FFN_CONV - 1) * batch
    n = batch * batch

    @pl.when(pl.program_id(0) == 0)
    def _():
        prev[...] = jnp.zeros_like(prev)

    swap = swap_ref[...]
    g = g_ref[...]
    hn_b = [_rms(x_ref[:, b * d:(b + 1) * d], g).astype(BF16) for b in range(batch)]
    subs = []
    for k in range(steps // batch):
        hb = jnp.concatenate([h[k * batch:(k + 1) * batch] for h in hn_b], axis=0)
        subs.append(_dot(swap, hb).astype(BF16))
    hn = jnp.concatenate(subs, axis=0)

    def conv(up, part, cols):
        hist = prev[part]
        y = cb_ref[:, cols] + cw_ref[FFN_CONV - 1:FFN_CONV, cols] * up
        for t in range(FFN_CONV - 1):
            back = (FFN_CONV - 1 - t) * batch
            shifted = jnp.concatenate([hist[halo - back:], up[:rows - back]], axis=0)
            y = y + cw_ref[t:t + 1, cols] * shifted
        prev[part] = up[rows - halo:]
        return y

    for j in range(FFN_NCHUNK):
        vc = slice(j * FFN_CHUNK, (j + 1) * FFN_CHUNK)
        gc = slice(D_FF + j * FFN_CHUNK, D_FF + (j + 1) * FFN_CHUNK)
        val = conv(_dot(hn, wup_ref[:, vc]), 2 * j, vc)
        gate = conv(_dot(hn, wup_ref[:, gc]), 2 * j + 1, gc)
        act_s[:, vc] = (gate * _sigmoid(gate) * val).astype(BF16)
    acc = _dot(act_s[...], wdn_ref[...])

    hi = acc.astype(BF16)
    lo = (acc - hi.astype(F32)).astype(BF16)
    for k in range(steps // batch):
        yb = _dot(swap, hi[k * n:(k + 1) * n]) + _dot(swap, lo[k * n:(k + 1) * n])
        ts = slice(k * batch, (k + 1) * batch)
        for b in range(batch):
            out = x_ref[ts, b * d:(b + 1) * d] + yb[b * batch:(b + 1) * batch]
            if final:
                o_ref[b, ts, :] = _rms(out, gf_ref[...])
            else:
                o_ref[ts, b * d:(b + 1) * d] = out


def _ffn(x, swap, g, wup, cw, cb, wdn, gf, B, S, final):
    rows = FFN_ROW_TILE
    steps = rows // B
    blk = pl.BlockSpec((steps, B * D_MODEL), lambda i: (i, 0))
    if final:
        out_spec = pl.BlockSpec((B, steps, D_MODEL), lambda i: (0, i, 0))
        out_shape = jax.ShapeDtypeStruct((B, S, D_MODEL), F32)
    else:
        out_spec, out_shape = blk, jax.ShapeDtypeStruct((S, B * D_MODEL), F32)
    return pl.pallas_call(
        functools.partial(_ffn_kernel, batch=B, steps=steps, final=final),
        grid=(S // steps,),
        in_specs=[blk, _whole(swap.shape), _whole(g.shape), _wspec(wup), _whole(cw.shape), _whole(cb.shape),
                  _wspec(wdn), _whole(gf.shape)],
        out_specs=out_spec,
        out_shape=out_shape,
        scratch_shapes=[pltpu.VMEM((2 * FFN_NCHUNK, (FFN_CONV - 1) * B, FFN_CHUNK), F32),
                        pltpu.VMEM((rows, D_FF), BF16)],
        compiler_params=_params("arbitrary"),
        name="conv_ffn",
    )(x, swap, g, _wop(wup), cw, cb, _wop(wdn), gf)


def _row(v):
    return v.reshape(1, -1).astype(F32)


def _block_diag(blocks):
    n, r, c = blocks.shape
    eye = jnp.eye(n, dtype=blocks.dtype)
    return (blocks[:, :, None, :] * eye[:, None, :, None]).reshape(n * r, n * c)


def _s5_b_weights(bbr, bbi):
    per = SSM_GROUPS // SSM_LANE_BLOCKS
    out = []
    for j in range(SSM_LANE_BLOCKS):
        r = bbr[j * 128:(j + 1) * 128].reshape(per, SSM_GROUP, SSM_STATE)
        i = bbi[j * 128:(j + 1) * 128].reshape(per, SSM_GROUP, SSM_STATE)
        out.append(jnp.concatenate([_block_diag(r), _block_diag(i)], axis=1))
    return jnp.stack(out).astype(BF16)


def _s5_c_weights(c):
    per = SSM_GROUPS // SSM_LANE_BLOCKS
    ct = jnp.swapaxes(c, 1, 2)
    return jnp.stack([_block_diag(ct[j * per:(j + 1) * per]) for j in range(SSM_LANE_BLOCKS)]).astype(BF16)


def kernel(x, mem, positions, norm_mix_g, w_in, ssm_lambda_re, ssm_lambda_im, ssm_log_step, ssm_b_re, ssm_b_im, ssm_c_re, ssm_c_im, ssm_d, ssm_w_glu, ssm_b_glu, diff_lq1, diff_lk1, diff_lq2, diff_lk2, diff_subln_g, lru_conv_w, lru_conv_b, lru_wa, lru_ba, lru_wx, lru_bx, lru_lambda, w_br_ssm, w_br_attn, w_br_lru, w_out, norm_xattn_g, norm_mem_g, xattn_wq, xattn_wkv, xattn_wo, norm_ffn_g, ffn_w_up, ffn_conv_w, ffn_conv_b, ffn_w_down, final_norm_g):
    B, S, _ = x.shape
    depth = norm_mix_g.shape[0]
    mem_len = mem.shape[1]
    assert S % ROW_TILE == 0 and ROW_TILE % B == 0 and B % 8 == 0 and (B * mem_len) % ROW_TILE == 0
    assert S % (2 * ATTN_TK) == 0 and FFN_ROW_TILE % (B * B) == 0 and (S * B) % FFN_ROW_TILE == 0
    assert REC_ROW_TILE % (B * B) == 0 and (S * B) % REC_ROW_TILE == 0

    cos, sin = _rope_tables(positions.astype(F32)[..., None], B, S)

    rep = lambda a: jnp.repeat(a, SSM_GROUP, axis=1)
    b_t = lambda a: jnp.swapaxes(a, 2, 3).reshape(depth, SSM_WIDTH, SSM_STATE)
    abr, abi, bbr, bbi = _s5_prep(rep(ssm_lambda_re), rep(ssm_lambda_im), rep(ssm_log_step[..., None]),
                                  b_t(ssm_b_re), b_t(ssm_b_im))
    mem_rows = mem.reshape(B * mem_len, D_MODEL)
    swap = _swap_matrix(B)
    (w_in, ssm_w_glu, w_br_ssm, w_br_attn, w_br_lru, w_out, xattn_wq, xattn_wkv, xattn_wo, ffn_w_up,
     ffn_w_down) = (a.astype(BF16) for a in (w_in, ssm_w_glu, w_br_ssm, w_br_attn, w_br_lru, w_out, xattn_wq,
                                             xattn_wkv, xattn_wo, ffn_w_up, ffn_w_down))

    xs = x
    batch_major = True
    for l in range(depth):
        lambda_init = 0.8 - 0.6 * math.exp(-0.3 * l)
        u, q, k, v, xr, gr, gt = _inproj(xs, batch_major, _row(norm_mix_g[l]), _Layer(w_in, l), cos, sin, B, S)

        y_ssm = _s5(u, swap, _s5_b_weights(bbr[l], bbi[l]),
                    abr[l, ::SSM_GROUP].reshape(1, SSM_NSTATE), abi[l, ::SSM_GROUP].reshape(1, SSM_NSTATE),
                    _s5_c_weights(ssm_c_re[l]), _s5_c_weights(ssm_c_im[l]), _row(ssm_d[l]),
                    _Layer(ssm_w_glu, l), _row(ssm_b_glu[l]), B)

        wax = jnp.concatenate([_block_diag(lru_wa[l]), _block_diag(lru_wx[l])], axis=1).astype(BF16)
        y_lru = _lru(xr, gr, swap, lru_conv_w[l].astype(F32), _row(lru_conv_b[l]), wax, _row(lru_ba[l]),
                     _row(lru_bx[l]), _row(lru_lambda[l]), B)

        y_att = _diff_attn(q, k, v, _row(diff_lq1[l]), _row(diff_lk1[l]), _row(diff_lq2[l]), _row(diff_lk2[l]),
                           _row(diff_subln_g[l]), lambda_init, B, S)

        kv = _memkv(mem_rows, _row(norm_mem_g[l]), _Layer(xattn_wkv, l))
        xs = _merge_xattn(xs, batch_major, y_ssm, y_att, y_lru, gt, _Layer(w_br_ssm, l), _Layer(w_br_attn, l),
                          _Layer(w_br_lru, l), _Layer(w_out, l), _row(norm_xattn_g[l]), _Layer(xattn_wq, l), kv,
                          _Layer(xattn_wo, l), mem_len, B, S)
        batch_major = False

        xs = _ffn(xs, swap, _row(norm_ffn_g[l]), _Layer(ffn_w_up, l), ffn_conv_w[l].astype(F32),
                  _row(ffn_conv_b[l]), _Layer(ffn_w_down, l), _row(final_norm_g), B, S, final=l == depth - 1)

    return xs
```

```python
import collections
import functools
import math

import jax
import jax.numpy as jnp
from jax import lax
from jax.experimental import pallas as pl
from jax.experimental.pallas import tpu as pltpu

F32 = jnp.float32
BF16 = jnp.bfloat16

EPS = 1e-6
D_MODEL = 1024
SSM_WIDTH = 384
SSM_GROUP = 16
SSM_GROUPS = 24
SSM_STATE = 64
SSM_NSTATE = SSM_GROUPS * SSM_STATE
SSM_LANE_BLOCKS = 3
DIFF_HEADS = 4
DIFF_HD = 64
DIFF_W = 512
ROPE_THETA = 10000.0
LRU_WIDTH = 512
LRU_HEADS = 8
LRU_HD = 64
LRU_CONV = 4
LRU_C = 8.0
XATTN_HEADS = 4
XATTN_HD = 256
D_FF = 2816
FFN_CONV = 3
FFN_CHUNK = 256
FFN_NCHUNK = D_FF // FFN_CHUNK
OFF_U, OFF_Q, OFF_K, OFF_V, OFF_XR, OFF_GR, OFF_G = 0, 384, 896, 1408, 1920, 2432, 2944
D_IN = 6016

ROW_TILE = 512
PROJ_ROW_TILE = 1024
FFN_ROW_TILE = 1024
REC_ROW_TILE = 1024
ATTN_TK = 512
ATTN_NORM_ROWS = 128
ATTN_MIN_ROW_SUM = 1e-25
VMEM_LIMIT = 56 * 1024 * 1024


def _dot(a, b):
    return jnp.dot(a, b, preferred_element_type=F32)


def _dot_nt(a, b):
    return lax.dot_general(a, b, (((1,), (1,)), ((), ())), preferred_element_type=F32)


def _rms(x, g):
    ms = jnp.mean(x * x, axis=-1, keepdims=True)
    return x * lax.rsqrt(ms + EPS) * g


def _sigmoid(x):
    return 1.0 / (1.0 + jnp.exp(-x))


def _gelu(x):
    return 0.5 * x * (1.0 + jnp.tanh(0.7978845608028654 * (x + 0.044715 * (x * x * x))))


def _params(*sem):
    return pltpu.CompilerParams(dimension_semantics=sem, vmem_limit_bytes=VMEM_LIMIT)


def _whole(shape):
    zeros = (0,) * len(shape)
    return pl.BlockSpec(shape, lambda *_: zeros, pipeline_mode=pl.Buffered(1))


_Layer = collections.namedtuple("_Layer", "stack index")


def _wspec(w):
    if isinstance(w, _Layer):
        shape = w.stack.shape[1:]
        idx = (w.index,) + (0,) * len(shape)
        return pl.BlockSpec((None,) + shape, lambda *_: idx, pipeline_mode=pl.Buffered(1))
    return _whole(w.shape)


def _wop(w):
    return w.stack if isinstance(w, _Layer) else w


def _x_spec(batch_major, rows, width):
    if batch_major:
        return pl.BlockSpec((None, rows, width), lambda b, i: (b, i, 0))
    return pl.BlockSpec((rows, width), lambda b, i: (i, b))


def _rope_kernel(pos_ref, cos_ref, sin_ref):
    lane = lax.broadcasted_iota(jnp.int32, (1, 128), 1)
    j = (lane & 31).astype(F32)
    inv = jnp.exp((-math.log(ROPE_THETA) * (2.0 * j)) / DIFF_HD)
    ang = pos_ref[...] * inv
    first_half = (lane & 63) < 32
    cos_ref[...] = jnp.cos(ang)
    s = jnp.sin(ang)
    sin_ref[...] = jnp.where(first_half, -s, s)


def _rope_tables(pos_f, B, S):
    rows = ROW_TILE
    spec = pl.BlockSpec((None, rows, 128), lambda b, i: (b, i, 0))
    return pl.pallas_call(
        _rope_kernel,
        grid=(B, S // rows),
        in_specs=[pl.BlockSpec((None, rows, 1), lambda b, i: (b, i, 0))],
        out_specs=[spec, spec],
        out_shape=[jax.ShapeDtypeStruct((B, S, 128), F32)] * 2,
        compiler_params=_params("parallel", "parallel"),
        name="rope_tables",
    )(pos_f)


def _s5_prep_kernel(lr_ref, li_ref, ls_ref, br_ref, bi_ref, abr_ref, abi_ref, bbr_ref, bbi_ref):
    lr = lr_ref[...]
    li = li_ref[...]
    dt = jnp.exp(ls_ref[...])
    mag = jnp.exp(lr * dt)
    ab_r = mag * jnp.cos(li * dt)
    ab_i = mag * jnp.sin(li * dt)
    den = lr * lr + li * li
    nr = ab_r - 1.0
    f_r = (nr * lr + ab_i * li) / den
    f_i = (ab_i * lr - nr * li) / den
    br = br_ref[...]
    bi = bi_ref[...]
    abr_ref[...] = ab_r
    abi_ref[...] = ab_i
    bbr_ref[...] = f_r * br - f_i * bi
    bbi_ref[...] = f_r * bi + f_i * br


def _s5_prep(lr_rep, li_rep, ls_rep, br_t, bi_t):
    depth = lr_rep.shape[0]
    spec = pl.BlockSpec((None, SSM_WIDTH, SSM_STATE), lambda l: (l, 0, 0))
    return pl.pallas_call(
        _s5_prep_kernel,
        grid=(depth,),
        in_specs=[spec, spec, pl.BlockSpec((None, SSM_WIDTH, 1), lambda l: (l, 0, 0)), spec, spec],
        out_specs=[spec] * 4,
        out_shape=[jax.ShapeDtypeStruct((depth, SSM_WIDTH, SSM_STATE), F32)] * 4,
        compiler_params=_params("parallel"),
        name="s5_prep",
    )(lr_rep, li_rep, ls_rep, br_t, bi_t)


def _rope(x, cos, sin_signed, first_half):
    back = pltpu.roll(x, 32, 1)
    fwd = pltpu.roll(x, DIFF_W - 32, 1)
    return x * cos + jnp.where(first_half, fwd, back) * sin_signed


def _inproj_kernel(x_ref, g_ref, w_ref, cos_ref, sin_ref,
                   u_ref, q_ref, k_ref, v_ref, xr_ref, gr_ref, gt_ref):
    hn = _rms(x_ref[...], g_ref[...]).astype(BF16)

    def seg(off, width):
        return _dot(hn, w_ref[:, off:off + width])

    cos = jnp.concatenate([cos_ref[...]] * 4, axis=1)
    sin = jnp.concatenate([sin_ref[...]] * 4, axis=1)
    lane = lax.broadcasted_iota(jnp.int32, (1, DIFF_W), 1)
    first_half = (lane & 63) < 32

    u_ref[...] = seg(OFF_U, SSM_WIDTH).astype(BF16)
    q_ref[...] = (_rope(seg(OFF_Q, DIFF_W), cos, sin, first_half) * (DIFF_HD ** -0.5)).astype(BF16)
    k_ref[...] = _rope(seg(OFF_K, DIFF_W), cos, sin, first_half).astype(BF16)
    v_ref[...] = seg(OFF_V, DIFF_W).astype(BF16)
    xr_ref[...] = seg(OFF_XR, LRU_WIDTH).astype(BF16)
    gr_ref[...] = seg(OFF_GR, LRU_WIDTH).astype(BF16)
    for c in range(6):
        gt_ref[:, c * 512:(c + 1) * 512] = _sigmoid(seg(OFF_G + c * 512, 512)).astype(BF16)


def _inproj(x, batch_major, g, w_in, cos, sin, B, S):
    rows = PROJ_ROW_TILE

    def out(width):
        return pl.BlockSpec((rows, width), lambda b, i: (i, b))

    widths = (SSM_WIDTH, DIFF_W, DIFF_W, DIFF_W, LRU_WIDTH, LRU_WIDTH, 3 * D_MODEL)
    tab = pl.BlockSpec((None, rows, 128), lambda b, i: (b, i, 0))
    return pl.pallas_call(
        _inproj_kernel,
        grid=(B, S // rows),
        in_specs=[_x_spec(batch_major, rows, D_MODEL), _whole((1, D_MODEL)), _wspec(w_in), tab, tab],
        out_specs=[out(w) for w in widths],
        out_shape=[jax.ShapeDtypeStruct((S, B * w), BF16) for w in widths],
        compiler_params=_params("parallel", "parallel"),
        name="inproj",
    )(x, g, _wop(w_in), cos, sin)


def _swap_matrix(batch):
    idx = jnp.arange(batch * batch)
    return (idx[:, None] == (idx[None, :] % batch) * batch + idx[None, :] // batch).astype(BF16)


def _load_time_major(ref, width, batch, steps, swap):
    subs = []
    for k in range(steps // batch):
        rows = slice(k * batch, (k + 1) * batch)
        xb = jnp.concatenate([ref[rows, b * width:(b + 1) * width] for b in range(batch)], axis=0)
        subs.append(_dot(swap, xb).astype(BF16))
    return jnp.concatenate(subs, axis=0)


def _store_batch_major(ref, y, width, batch, steps, swap):
    n = batch * batch
    for k in range(steps // batch):
        yb = _dot(swap, y[k * n:(k + 1) * n]).astype(BF16)
        for b in range(batch):
            ref[k * batch:(k + 1) * batch, b * width:(b + 1) * width] = yb[b * batch:(b + 1) * batch]


def _s5_kernel(u_ref, swap_ref, wb_ref, abr_ref, abi_ref, cre_ref, cim_ref, d_ref, wglu_ref, bglu_ref,
               o_ref, sre, sim, st_re, st_im, *, batch, steps):
    @pl.when(pl.program_id(0) == 0)
    def _():
        st_re[...] = jnp.zeros_like(st_re)
        st_im[...] = jnp.zeros_like(st_im)

    swap = swap_ref[...]
    u = _load_time_major(u_ref, SSM_WIDTH, batch, steps, swap)
    nb = SSM_NSTATE // SSM_LANE_BLOCKS
    for j in range(SSM_LANE_BLOCKS):
        bu = _dot(u[:, j * 128:(j + 1) * 128], wb_ref[j])
        sre[:, j * nb:(j + 1) * nb] = bu[:, :nb]
        sim[:, j * nb:(j + 1) * nb] = bu[:, nb:]

    for j in range(SSM_LANE_BLOCKS):
        cols = slice(j * nb, (j + 1) * nb)
        ar = jnp.broadcast_to(abr_ref[:, cols], (batch, nb))
        ai = jnp.broadcast_to(abi_ref[:, cols], (batch, nb))
        sr = st_re[:, cols]
        si = st_im[:, cols]
        for t in range(steps):
            rows = slice(t * batch, (t + 1) * batch)
            nr = ar * sr - ai * si + sre[rows, cols]
            ni = ar * si + ai * sr + sim[rows, cols]
            sre[rows, cols] = nr
            sim[rows, cols] = ni
            sr, si = nr, ni
        st_re[:, cols] = sr
        st_im[:, cols] = si

    ys = []
    for j in range(SSM_LANE_BLOCKS):
        cols = slice(j * nb, (j + 1) * nb)
        ys.append(_dot(sre[:, cols].astype(BF16), cre_ref[j]) - _dot(sim[:, cols].astype(BF16), cim_ref[j]))
    y = jnp.concatenate(ys, axis=1) + d_ref[...] * u.astype(F32)
    y = _gelu(y)
    z = _dot(y.astype(BF16), wglu_ref[...]) + bglu_ref[...]
    out = (z[:, :SSM_WIDTH] * _sigmoid(z[:, SSM_WIDTH:])).astype(BF16)
    _store_batch_major(o_ref, out, SSM_WIDTH, batch, steps, swap)


def _s5(u, swap, wb, abr, abi, cre, cim, d, wglu, bglu, B):
    S = u.shape[0]
    rows = REC_ROW_TILE
    steps = rows // B
    blk = pl.BlockSpec((steps, B * SSM_WIDTH), lambda i: (i, 0))
    return pl.pallas_call(
        functools.partial(_s5_kernel, batch=B, steps=steps),
        grid=(S // steps,),
        in_specs=[blk, _whole(swap.shape), _whole(wb.shape), _whole(abr.shape), _whole(abi.shape),
                  _whole(cre.shape), _whole(cim.shape), _whole(d.shape), _wspec(wglu), _whole(bglu.shape)],
        out_specs=blk,
        out_shape=jax.ShapeDtypeStruct((S, B * SSM_WIDTH), BF16),
        scratch_shapes=[pltpu.VMEM((rows, SSM_NSTATE), F32), pltpu.VMEM((rows, SSM_NSTATE), F32),
                        pltpu.VMEM((B, SSM_NSTATE), F32), pltpu.VMEM((B, SSM_NSTATE), F32)],
        compiler_params=_params("arbitrary"),
        name="s5_branch",
    )(u, swap, wb, abr, abi, cre, cim, d, _wop(wglu), bglu)


def _lru_kernel(xr_ref, gr_ref, swap_ref, cw_ref, cb_ref, wax_ref, ba_ref, bx_ref, lam_ref,
                o_ref, ext, a_s, b_s, h_st, *, batch, steps):
    rows = batch * steps
    halo = (LRU_CONV - 1) * batch

    @pl.when(pl.program_id(0) == 0)
    def _():
        ext[0:halo, :] = jnp.zeros((halo, LRU_WIDTH), F32)
        h_st[...] = jnp.zeros_like(h_st)

    swap = swap_ref[...]
    ext[halo:halo + rows, :] = _load_time_major(xr_ref, LRU_WIDTH, batch, steps, swap).astype(F32)
    xc = cb_ref[...] + cw_ref[LRU_CONV - 1:LRU_CONV, :] * ext[halo:halo + rows, :]
    for j in range(LRU_CONV - 1):
        xc = xc + cw_ref[j:j + 1, :] * ext[j * batch:j * batch + rows, :]
    ext[0:halo, :] = ext[rows:rows + halo, :]

    z = _dot(xc.astype(BF16), wax_ref[...])
    r = _sigmoid(z[:, :LRU_WIDTH] + ba_ref[...])
    ig = _sigmoid(z[:, LRU_WIDTH:] + bx_ref[...])
    softplus_neg_lam = jnp.log1p(jnp.exp(-lam_ref[...]))
    a = jnp.exp((-LRU_C) * r * softplus_neg_lam)
    a_s[...] = a
    y = 1.0 - a * a
    b_s[...] = jnp.where(y > 0.0, y * lax.rsqrt(y), 0.0) * (ig * xc)

    h = h_st[...]
    for t in range(steps):
        rs = slice(t * batch, (t + 1) * batch)
        h = a_s[rs, :] * h + b_s[rs, :]
        b_s[rs, :] = h
    h_st[...] = h
    gr = _load_time_major(gr_ref, LRU_WIDTH, batch, steps, swap).astype(F32)
    _store_batch_major(o_ref, (b_s[...] * _gelu(gr)).astype(BF16), LRU_WIDTH, batch, steps, swap)


def _lru(xr, gr, swap, cw, cb, wax, ba, bx, lam, B):
    S = xr.shape[0]
    rows = REC_ROW_TILE
    steps = rows // B
    blk = pl.BlockSpec((steps, B * LRU_WIDTH), lambda i: (i, 0))
    halo = (LRU_CONV - 1) * B
    return pl.pallas_call(
        functools.partial(_lru_kernel, batch=B, steps=steps),
        grid=(S // steps,),
        in_specs=[blk, blk, _whole(swap.shape), _whole(cw.shape), _whole(cb.shape), _whole(wax.shape),
                  _whole(ba.shape), _whole(bx.shape), _whole(lam.shape)],
        out_specs=blk,
        out_shape=jax.ShapeDtypeStruct((S, B * LRU_WIDTH), BF16),
        scratch_shapes=[pltpu.VMEM((rows + halo, LRU_WIDTH), F32), pltpu.VMEM((rows, LRU_WIDTH), F32),
                        pltpu.VMEM((rows, LRU_WIDTH), F32), pltpu.VMEM((B, LRU_WIDTH), F32)],
        compiler_params=_params("arbitrary"),
        name="rglru_branch",
    )(xr, gr, swap, cw, cb, wax, ba, bx, lam)


def _diff_attn_kernel(q_ref, k_ref, v_ref, lq1_ref, lk1_ref, lq2_ref, lk2_ref, g_ref, o_ref,
                      qq, m_s, acc_s, kn_s, *, lambda_init, tk):
    qi = pl.program_id(1)
    hw = 2 * DIFF_HD
    tq = 2 * tk
    every = slice(0, 2 * tq)
    upper = slice(tq, 2 * tq)
    lane = lax.broadcasted_iota(jnp.int32, (1, hw), 1)

    @pl.when((pl.program_id(0) == 0) & (qi == 0))
    def _():
        acc_s[...] = jnp.zeros_like(acc_s)

    for h in range(DIFF_HEADS):
        for half in range(2):
            q = q_ref[half * tk:(half + 1) * tk, h * hw:(h + 1) * hw]
            zero = jnp.zeros_like(q)
            qq[h, half * tq:half * tq + tk, :] = jnp.where(lane < DIFF_HD, q, zero)
            qq[h, half * tq + tk:(half + 1) * tq, :] = jnp.where(lane >= DIFF_HD, q, zero)

    @pl.when(qi == 0)
    def _():
        kn_s[...] = jnp.zeros_like(kn_s)

    new_keys = pl.multiple_of(qi * tq, tq)
    d_row = lax.broadcasted_iota(jnp.int32, (hw, 2 * hw), 0)
    d_col = lax.broadcasted_iota(jnp.int32, (hw, 2 * hw), 1)
    comp_sum = ((d_row < DIFF_HD) == (d_col < hw)).astype(BF16)

    def max_sq_norm(x):
        sq = x * x
        fold = sq[0:ATTN_NORM_ROWS]
        for r in range(ATTN_NORM_ROWS, tq, ATTN_NORM_ROWS):
            fold = jnp.maximum(fold, sq[r:r + ATTN_NORM_ROWS])
        return jnp.max(_dot(fold, comp_sum), axis=0, keepdims=True)

    bound = []
    for h in range(DIFF_HEADS):
        kn = jnp.maximum(kn_s[h:h + 1, :], max_sq_norm(k_ref[pl.ds(new_keys, tq), h * hw:(h + 1) * hw]))
        kn_s[h:h + 1, :] = kn
        qn = max_sq_norm(q_ref[:, h * hw:(h + 1) * hw])
        n2 = qn * kn
        b = n2 * lax.rsqrt(jnp.maximum(n2, 1e-30))
        bound.append((b[:, 0:1], b[:, hw:hw + 1]))

    def scores(j, h, rows, diagonal):
        start = pl.multiple_of(j * tk, tk)
        s = _dot_nt(qq[h, rows, :], k_ref[pl.ds(start, tk), h * hw:(h + 1) * hw])
        if diagonal:
            row = lax.broadcasted_iota(jnp.int32, s.shape, 0)
            col = lax.broadcasted_iota(jnp.int32, s.shape, 1)
            seen = col <= (row & (tk - 1))
            if s.shape[0] == 2 * tq:
                seen = seen | (row >= tq)
            s = jnp.where(seen, s, -1e30)
        return s

    chunks = [slice(c * hw, (c + 1) * hw) for c in range(tk // hw)]

    def max_step(j, rows, diagonal):
        for h in range(DIFF_HEADS):
            s = scores(j, h, rows, diagonal)
            m = m_s[h, rows, :]
            for c in chunks:
                m = jnp.maximum(m, s[:, c])
            m_s[h, rows, :] = m

    def shift_bound(h, rows, s):
        return jnp.concatenate([s[i * tk:(i + 1) * tk] - bound[h][i % 2] for i in range(s.shape[0] // tk)], axis=0)

    def shift_row_max(h, rows, s):
        m = m_s[h, rows, :]
        return jnp.concatenate([s[:, c] - m for c in chunks], axis=1)

    def acc_step(shift, j, rows, diagonal):
        start = pl.multiple_of(j * tk, tk)
        ones = jnp.ones((tk, hw), BF16)
        for h in range(DIFF_HEADS):
            p = jnp.exp(shift(h, rows, scores(j, h, rows, diagonal))).astype(BF16)
            v_ext = jnp.concatenate([v_ref[pl.ds(start, tk), h * hw:(h + 1) * hw], ones], axis=1)
            acc_s[h, rows, :] = acc_s[h, rows, :] + _dot(p, v_ext)

    def loop(step):
        def body(j, c):
            step(j, every, False)
            return c
        lax.fori_loop(0, 2 * qi, body, 0)
        step(2 * qi, every, True)
        step(2 * qi + 1, upper, True)

    loop(functools.partial(acc_step, shift_bound))
    row_sum_min = jnp.min(acc_s[0][:, hw:])
    for h in range(1, DIFF_HEADS):
        row_sum_min = jnp.minimum(row_sum_min, jnp.min(acc_s[h][:, hw:]))

    @pl.when(jnp.logical_not(row_sum_min >= ATTN_MIN_ROW_SUM))
    def _():
        m_s[...] = jnp.full(m_s.shape, -1e30, F32)
        acc_s[...] = jnp.zeros_like(acc_s)
        loop(max_step)
        for h in range(DIFF_HEADS):
            m_s[h] = jnp.broadcast_to(jnp.max(m_s[h], axis=-1, keepdims=True), (2 * tq, hw))
        loop(functools.partial(acc_step, shift_row_max))

    lam = (jnp.exp(jnp.sum(lq1_ref[...] * lk1_ref[...], axis=-1, keepdims=True))
           - jnp.exp(jnp.sum(lq2_ref[...] * lk2_ref[...], axis=-1, keepdims=True)) + lambda_init)
    for h in range(DIFF_HEADS):
        for half in range(2):
            acc = acc_s[h, half * tq:(half + 1) * tq, :]
            acc_s[h, half * tq:(half + 1) * tq, :] = jnp.zeros_like(acc)
            o = acc[:, :hw] / acc[:, hw:]
            o = o[:tk] - lam * o[tk:]
            o_ref[half * tk:(half + 1) * tk, h * hw:(h + 1) * hw] = (
                _rms(o, g_ref[...]) * (1.0 - lambda_init)).astype(BF16)


def _diff_attn(q, k, v, lq1, lk1, lq2, lk2, g, lambda_init, B, S):
    tk = ATTN_TK
    tq = 2 * tk
    hw = 2 * DIFF_HD
    qspec = pl.BlockSpec((tq, DIFF_W), lambda b, i: (i, b))
    kspec = pl.BlockSpec((S, DIFF_W), lambda b, i: (0, b))
    small = pl.BlockSpec((1, DIFF_HD), lambda b, i: (0, 0))
    return pl.pallas_call(
        functools.partial(_diff_attn_kernel, lambda_init=lambda_init, tk=tk),
        grid=(B, S // tq),
        in_specs=[qspec, kspec, kspec, small, small, small, small,
                  pl.BlockSpec((1, hw), lambda b, i: (0, 0))],
        out_specs=qspec,
        out_shape=jax.ShapeDtypeStruct((S, B * DIFF_W), BF16),
        scratch_shapes=[pltpu.VMEM((DIFF_HEADS, 2 * tq, hw), BF16), pltpu.VMEM((DIFF_HEADS, 2 * tq, hw), F32),
                        pltpu.VMEM((DIFF_HEADS, 2 * tq, 2 * hw), F32), pltpu.VMEM((DIFF_HEADS, 2 * hw), F32)],
        compiler_params=_params("arbitrary", "arbitrary"),
        name="diff_attn",
    )(q, k, v, lq1, lk1, lq2, lk2, g)


def _merge_xattn_kernel(x_ref, ys_ref, ya_ref, yl_ref, gt_ref, wbs_ref, wba_ref, wbl_ref, wout_ref,
                        g_ref, wq_ref, k_ref, v_ref, wo_ref, o_ref):
    d = D_MODEL
    m = gt_ref[:, 0:d].astype(F32) * _dot(ys_ref[...], wbs_ref[...])
    m = m + gt_ref[:, d:2 * d].astype(F32) * _dot(ya_ref[...], wba_ref[...])
    m = m + gt_ref[:, 2 * d:3 * d].astype(F32) * _dot(yl_ref[...], wbl_ref[...])
    x = x_ref[...] + _dot(m.astype(BF16), wout_ref[...])

    hn = _rms(x, g_ref[...]).astype(BF16)
    q = (_dot(hn, wq_ref[...]) * (XATTN_HD ** -0.5)).astype(BF16)
    outs = []
    for h in range(XATTN_HEADS):
        cols = slice(h * XATTN_HD, (h + 1) * XATTN_HD)
        s = _dot_nt(q[:, cols], k_ref[:, cols])
        p = jnp.exp(s - jnp.max(s, axis=-1, keepdims=True))
        p = p / jnp.sum(p, axis=-1, keepdims=True)
        outs.append(_dot(p.astype(BF16), v_ref[:, cols]).astype(BF16))
    o_ref[...] = x + _dot(jnp.concatenate(outs, axis=1), wo_ref[...])


def _merge_xattn(x, batch_major, ys, ya, yl, gt, wbs, wba, wbl, wout, g, wq, kv, wo, mem_len, B, S):
    rows = PROJ_ROW_TILE

    def blk(width):
        return pl.BlockSpec((rows, width), lambda b, i: (i, b))

    return pl.pallas_call(
        _merge_xattn_kernel,
        grid=(B, S // rows),
        in_specs=[_x_spec(batch_major, rows, D_MODEL), blk(SSM_WIDTH), blk(DIFF_W), blk(LRU_WIDTH),
                  blk(3 * D_MODEL), _wspec(wbs), _wspec(wba), _wspec(wbl), _wspec(wout),
                  _whole(g.shape), _wspec(wq),
                  pl.BlockSpec((mem_len, D_MODEL), lambda b, i: (b, 0)),
                  pl.BlockSpec((mem_len, D_MODEL), lambda b, i: (b, 1)),
                  _wspec(wo)],
        out_specs=blk(D_MODEL),
        out_shape=jax.ShapeDtypeStruct((S, B * D_MODEL), F32),
        compiler_params=_params("parallel", "parallel"),
        name="merge_xattn",
    )(x, ys, ya, yl, gt, _wop(wbs), _wop(wba), _wop(wbl), _wop(wout), g, _wop(wq), kv, kv, _wop(wo))


def _memkv_kernel(m_ref, g_ref, w_ref, o_ref):
    hn = _rms(m_ref[...], g_ref[...]).astype(BF16)
    o_ref[...] = _dot(hn, w_ref[...]).astype(BF16)


def _memkv(mem_rows, g, wkv):
    n = mem_rows.shape[0]
    rows = ROW_TILE
    return pl.pallas_call(
        _memkv_kernel,
        grid=(n // rows,),
        in_specs=[pl.BlockSpec((rows, D_MODEL), lambda i: (i, 0)), _whole(g.shape), _wspec(wkv)],
        out_specs=pl.BlockSpec((rows, 2 * D_MODEL), lambda i: (i, 0)),
        out_shape=jax.ShapeDtypeStruct((n, 2 * D_MODEL), BF16),
        compiler_params=_params("parallel"),
        name="mem_kv",
    )(mem_rows, g, _wop(wkv))


def _ffn_kernel(x_ref, swap_ref, g_ref, wup_ref, cw_ref, cb_ref, wdn_ref, gf_ref, o_ref, prev, act_s,
                *, batch, steps, final):
    rows = batch * steps
    d = D_MODEL
    halo = (FFN_CONV - 1) * batch
    n = batch * batch

    @pl.when(pl.program_id(0) == 0)
    def _():
        prev[...] = jnp.zeros_like(prev)

    swap = swap_ref[...]
    g = g_ref[...]
    hn_b = [_rms(x_ref[:, b * d:(b + 1) * d], g).astype(BF16) for b in range(batch)]
    subs = []
    for k in range(steps // batch):
        hb = jnp.concatenate([h[k * batch:(k + 1) * batch] for h in hn_b], axis=0)
        subs.append(_dot(swap, hb).astype(BF16))
    hn = jnp.concatenate(subs, axis=0)

    def conv(up, part, cols):
        hist = prev[part]
        y = cb_ref[:, cols] + cw_ref[FFN_CONV - 1:FFN_CONV, cols] * up
        for t in range(FFN_CONV - 1):
            back = (FFN_CONV - 1 - t) * batch
            shifted = jnp.concatenate([hist[halo - back:], up[:rows - back]], axis=0)
            y = y + cw_ref[t:t + 1, cols] * shifted
        prev[part] = up[rows - halo:]
        return y

    for j in range(FFN_NCHUNK):
        vc = slice(j * FFN_CHUNK, (j + 1) * FFN_CHUNK)
        gc = slice(D_FF + j * FFN_CHUNK, D_FF + (j + 1) * FFN_CHUNK)
        val = conv(_dot(hn, wup_ref[:, vc]), 2 * j, vc)
        gate = conv(_dot(hn, wup_ref[:, gc]), 2 * j + 1, gc)
        act_s[:, vc] = (gate * _sigmoid(gate) * val).astype(BF16)
    acc = _dot(act_s[...], wdn_ref[...])

    hi = acc.astype(BF16)
    lo = (acc - hi.astype(F32)).astype(BF16)
    for k in range(steps // batch):
        yb = _dot(swap, hi[k * n:(k + 1) * n]) + _dot(swap, lo[k * n:(k + 1) * n])
        ts = slice(k * batch, (k + 1) * batch)
        for b in range(batch):
            out = x_ref[ts, b * d:(b + 1) * d] + yb[b * batch:(b + 1) * batch]
            if final:
                o_ref[b, ts, :] = _rms(out, gf_ref[...])
            else:
                o_ref[ts, b * d:(b + 1) * d] = out


def _ffn(x, swap, g, wup, cw, cb, wdn, gf, B, S, final):
    rows = FFN_ROW_TILE
    steps = rows // B
    blk = pl.BlockSpec((steps, B * D_MODEL), lambda i: (i, 0))
    if final:
        out_spec = pl.BlockSpec((B, steps, D_MODEL), lambda i: (0, i, 0))
        out_shape = jax.ShapeDtypeStruct((B, S, D_MODEL), F32)
    else:
        out_spec, out_shape = blk, jax.ShapeDtypeStruct((S, B * D_MODEL), F32)
    return pl.pallas_call(
        functools.partial(_ffn_kernel, batch=B, steps=steps, final=final),
        grid=(S // steps,),
        in_specs=[blk, _whole(swap.shape), _whole(g.shape), _wspec(wup), _whole(cw.shape), _whole(cb.shape),
                  _wspec(wdn), _whole(gf.shape)],
        out_specs=out_spec,
        out_shape=out_shape,
        scratch_shapes=[pltpu.VMEM((2 * FFN_NCHUNK, (FFN_CONV - 1) * B, FFN_CHUNK), F32),
                        pltpu.VMEM((rows, D_FF), BF16)],
        compiler_params=_params("arbitrary"),
        name="conv_ffn",
    )(x, swap, g, _wop(wup), cw, cb, _wop(wdn), gf)


def _row(v):
    return v.reshape(1, -1).astype(F32)


def _block_diag(blocks):
    n, r, c = blocks.shape
    eye = jnp.eye(n, dtype=blocks.dtype)
    return (blocks[:, :, None, :] * eye[:, None, :, None]).reshape(n * r, n * c)


def _s5_b_weights(bbr, bbi):
    per = SSM_GROUPS // SSM_LANE_BLOCKS
    out = []
    for j in range(SSM_LANE_BLOCKS):
        r = bbr[j * 128:(j + 1) * 128].reshape(per, SSM_GROUP, SSM_STATE)
        i = bbi[j * 128:(j + 1) * 128].reshape(per, SSM_GROUP, SSM_STATE)
        out.append(jnp.concatenate([_block_diag(r), _block_diag(i)], axis=1))
    return jnp.stack(out).astype(BF16)


def _s5_c_weights(c):
    per = SSM_GROUPS // SSM_LANE_BLOCKS
    ct = jnp.swapaxes(c, 1, 2)
    return jnp.stack([_block_diag(ct[j * per:(j + 1) * per]) for j in range(SSM_LANE_BLOCKS)]).astype(BF16)


def kernel(x, mem, positions, norm_mix_g, w_in, ssm_lambda_re, ssm_lambda_im, ssm_log_step, ssm_b_re, ssm_b_im, ssm_c_re, ssm_c_im, ssm_d, ssm_w_glu, ssm_b_glu, diff_lq1, diff_lk1, diff_lq2, diff_lk2, diff_subln_g, lru_conv_w, lru_conv_b, lru_wa, lru_ba, lru_wx, lru_bx, lru_lambda, w_br_ssm, w_br_attn, w_br_lru, w_out, norm_xattn_g, norm_mem_g, xattn_wq, xattn_wkv, xattn_wo, norm_ffn_g, ffn_w_up, ffn_conv_w, ffn_conv_b, ffn_w_down, final_norm_g):
    B, S, _ = x.shape
    depth = norm_mix_g.shape[0]
    mem_len = mem.shape[1]
    assert S % ROW_TILE == 0 and ROW_TILE % B == 0 and B % 8 == 0 and (B * mem_len) % ROW_TILE == 0
    assert S % (2 * ATTN_TK) == 0 and FFN_ROW_TILE % (B * B) == 0 and (S * B) % FFN_ROW_TILE == 0
    assert REC_ROW_TILE % (B * B) == 0 and (S * B) % REC_ROW_TILE == 0 and S % PROJ_ROW_TILE == 0

    cos, sin = _rope_tables(positions.astype(F32)[..., None], B, S)

    rep = lambda a: jnp.repeat(a, SSM_GROUP, axis=1)
    b_t = lambda a: jnp.swapaxes(a, 2, 3).reshape(depth, SSM_WIDTH, SSM_STATE)
    abr, abi, bbr, bbi = _s5_prep(rep(ssm_lambda_re), rep(ssm_lambda_im), rep(ssm_log_step[..., None]),
                                  b_t(ssm_b_re), b_t(ssm_b_im))
    mem_rows = mem.reshape(B * mem_len, D_MODEL)
    swap = _swap_matrix(B)
    (w_in, ssm_w_glu, w_br_ssm, w_br_attn, w_br_lru, w_out, xattn_wq, xattn_wkv, xattn_wo, ffn_w_up,
     ffn_w_down) = (a.astype(BF16) for a in (w_in, ssm_w_glu, w_br_ssm, w_br_attn, w_br_lru, w_out, xattn_wq,
                                             xattn_wkv, xattn_wo, ffn_w_up, ffn_w_down))

    xs = x
    batch_major = True
    for l in range(depth):
        lambda_init = 0.8 - 0.6 * math.exp(-0.3 * l)
        u, q, k, v, xr, gr, gt = _inproj(xs, batch_major, _row(norm_mix_g[l]), _Layer(w_in, l), cos, sin, B, S)

        y_ssm = _s5(u, swap, _s5_b_weights(bbr[l], bbi[l]),
                    abr[l, ::SSM_GROUP].reshape(1, SSM_NSTATE), abi[l, ::SSM_GROUP].reshape(1, SSM_NSTATE),
                    _s5_c_weights(ssm_c_re[l]), _s5_c_weights(ssm_c_im[l]), _row(ssm_d[l]),
                    _Layer(ssm_w_glu, l), _row(ssm_b_glu[l]), B)

        wax = jnp.concatenate([_block_diag(lru_wa[l]), _block_diag(lru_wx[l])], axis=1).astype(BF16)
        y_lru = _lru(xr, gr, swap, lru_conv_w[l].astype(F32), _row(lru_conv_b[l]), wax, _row(lru_ba[l]),
                     _row(lru_bx[l]), _row(lru_lambda[l]), B)

        y_att = _diff_attn(q, k, v, _row(diff_lq1[l]), _row(diff_lk1[l]), _row(diff_lq2[l]), _row(diff_lk2[l]),
                           _row(diff_subln_g[l]), lambda_init, B, S)

        kv = _memkv(mem_rows, _row(norm_mem_g[l]), _Layer(xattn_wkv, l))
        xs = _merge_xattn(xs, batch_major, y_ssm, y_att, y_lru, gt, _Layer(w_br_ssm, l), _Layer(w_br_attn, l),
                          _Layer(w_br_lru, l), _Layer(w_out, l), _row(norm_xattn_g[l]), _Layer(xattn_wq, l), kv,
                          _Layer(xattn_wo, l), mem_len, B, S)
        batch_major = False

        xs = _ffn(xs, swap, _row(norm_ffn_g[l]), _Layer(ffn_w_up, l), ffn_conv_w[l].astype(F32),
                  _row(ffn_conv_b[l]), _Layer(ffn_w_down, l), _row(final_norm_g), B, S, final=l == depth - 1)

    return xs
```

```python
import collections
import functools
import math

import jax
import jax.numpy as jnp
from jax import lax
from jax.experimental import pallas as pl
from jax.experimental.pallas import tpu as pltpu

F32 = jnp.float32
BF16 = jnp.bfloat16

EPS = 1e-6
D_MODEL = 1024
SSM_WIDTH = 384
SSM_GROUP = 16
SSM_GROUPS = 24
SSM_STATE = 64
SSM_NSTATE = SSM_GROUPS * SSM_STATE
SSM_LANE_BLOCKS = 3
DIFF_HEADS = 4
DIFF_HD = 64
DIFF_W = 512
ROPE_THETA = 10000.0
LRU_WIDTH = 512
LRU_CONV = 4
LRU_C = 8.0
XATTN_HEADS = 4
XATTN_HD = 256
D_FF = 2816
FFN_CONV = 3
FFN_CHUNK = 256
FFN_NCHUNK = D_FF // FFN_CHUNK
OFF_U, OFF_Q, OFF_K, OFF_V, OFF_XR, OFF_GR, OFF_G = 0, 384, 896, 1408, 1920, 2432, 2944
D_IN = 6016
GATE_CHUNK = 512

LANES = 128
SSM_BLOCK_STATES = SSM_NSTATE // SSM_LANE_BLOCKS

ROW_TILE = 512
PROJ_ROW_TILE = 1024
FFN_ROW_TILE = 1024
REC_ROW_TILE = 1024
ATTN_TK = 512
ATTN_NORM_ROWS = 128
ATTN_MIN_ROW_SUM = 1e-25
VMEM_LIMIT = 56 * 1024 * 1024


def _dot(a, b):
    return jnp.dot(a, b, preferred_element_type=F32)


def _dot_nt(a, b):
    return lax.dot_general(a, b, (((1,), (1,)), ((), ())), preferred_element_type=F32)


def _rms(x, g):
    ms = jnp.mean(x * x, axis=-1, keepdims=True)
    return x * lax.rsqrt(ms + EPS) * g


def _sigmoid(x):
    return 1.0 / (1.0 + jnp.exp(-x))


def _gelu(x):
    return 0.5 * x * (1.0 + jnp.tanh(0.7978845608028654 * (x + 0.044715 * (x * x * x))))


def _params(*sem):
    return pltpu.CompilerParams(dimension_semantics=sem, vmem_limit_bytes=VMEM_LIMIT)


def _whole(shape):
    zeros = (0,) * len(shape)
    return pl.BlockSpec(shape, lambda *_: zeros, pipeline_mode=pl.Buffered(1))


_Layer = collections.namedtuple("_Layer", "stack index")


def _wspec(w):
    if isinstance(w, _Layer):
        shape = w.stack.shape[1:]
        idx = (w.index,) + (0,) * len(shape)
        return pl.BlockSpec((None,) + shape, lambda *_: idx, pipeline_mode=pl.Buffered(1))
    return _whole(w.shape)


def _wop(w):
    return w.stack if isinstance(w, _Layer) else w


def _x_spec(batch_major, rows, width):
    if batch_major:
        return pl.BlockSpec((None, rows, width), lambda b, i: (b, i, 0))
    return pl.BlockSpec((rows, width), lambda b, i: (i, b))


def _rope_kernel(pos_ref, cos_ref, sin_ref):
    half = DIFF_HD // 2
    lane = lax.broadcasted_iota(jnp.int32, (1, LANES), 1)
    j = (lane & (half - 1)).astype(F32)
    inv = jnp.exp((-math.log(ROPE_THETA) * (2.0 * j)) / DIFF_HD)
    ang = pos_ref[...] * inv
    first_half = (lane & (DIFF_HD - 1)) < half
    cos_ref[...] = jnp.cos(ang)
    s = jnp.sin(ang)
    sin_ref[...] = jnp.where(first_half, -s, s)


def _rope_tables(pos_f, B, S):
    rows = ROW_TILE
    spec = pl.BlockSpec((None, rows, LANES), lambda b, i: (b, i, 0))
    return pl.pallas_call(
        _rope_kernel,
        grid=(B, S // rows),
        in_specs=[pl.BlockSpec((None, rows, 1), lambda b, i: (b, i, 0))],
        out_specs=[spec, spec],
        out_shape=[jax.ShapeDtypeStruct((B, S, LANES), F32)] * 2,
        compiler_params=_params("parallel", "parallel"),
        name="rope_tables",
    )(pos_f)


def _s5_prep_kernel(lr_ref, li_ref, ls_ref, br_ref, bi_ref, abr_ref, abi_ref, bbr_ref, bbi_ref):
    lr = lr_ref[...]
    li = li_ref[...]
    dt = jnp.exp(ls_ref[...])
    mag = jnp.exp(lr * dt)
    ab_r = mag * jnp.cos(li * dt)
    ab_i = mag * jnp.sin(li * dt)
    den = lr * lr + li * li
    nr = ab_r - 1.0
    f_r = (nr * lr + ab_i * li) / den
    f_i = (ab_i * lr - nr * li) / den
    br = br_ref[...]
    bi = bi_ref[...]
    abr_ref[...] = ab_r
    abi_ref[...] = ab_i
    bbr_ref[...] = f_r * br - f_i * bi
    bbi_ref[...] = f_r * bi + f_i * br


def _s5_prep(lr_rep, li_rep, ls_rep, br_t, bi_t):
    depth = lr_rep.shape[0]
    spec = pl.BlockSpec((None, SSM_WIDTH, SSM_STATE), lambda l: (l, 0, 0))
    return pl.pallas_call(
        _s5_prep_kernel,
        grid=(depth,),
        in_specs=[spec, spec, pl.BlockSpec((None, SSM_WIDTH, 1), lambda l: (l, 0, 0)), spec, spec],
        out_specs=[spec] * 4,
        out_shape=[jax.ShapeDtypeStruct((depth, SSM_WIDTH, SSM_STATE), F32)] * 4,
        compiler_params=_params("parallel"),
        name="s5_prep",
    )(lr_rep, li_rep, ls_rep, br_t, bi_t)


def _rope(x, cos, sin_signed, first_half):
    half = DIFF_HD // 2
    back = pltpu.roll(x, half, 1)
    fwd = pltpu.roll(x, DIFF_W - half, 1)
    return x * cos + jnp.where(first_half, fwd, back) * sin_signed


def _inproj_kernel(x_ref, g_ref, w_ref, cos_ref, sin_ref,
                   u_ref, q_ref, k_ref, v_ref, xr_ref, gr_ref, gt_ref):
    hn = _rms(x_ref[...], g_ref[...]).astype(BF16)

    def seg(off, width):
        return _dot(hn, w_ref[:, off:off + width])

    cos = jnp.concatenate([cos_ref[...]] * (DIFF_W // LANES), axis=1)
    sin = jnp.concatenate([sin_ref[...]] * (DIFF_W // LANES), axis=1)
    lane = lax.broadcasted_iota(jnp.int32, (1, DIFF_W), 1)
    first_half = (lane & (DIFF_HD - 1)) < DIFF_HD // 2

    u_ref[...] = seg(OFF_U, SSM_WIDTH).astype(BF16)
    q_ref[...] = (_rope(seg(OFF_Q, DIFF_W), cos, sin, first_half) * (DIFF_HD ** -0.5)).astype(BF16)
    k_ref[...] = _rope(seg(OFF_K, DIFF_W), cos, sin, first_half).astype(BF16)
    v_ref[...] = seg(OFF_V, DIFF_W).astype(BF16)
    xr_ref[...] = seg(OFF_XR, LRU_WIDTH).astype(BF16)
    gr_ref[...] = seg(OFF_GR, LRU_WIDTH).astype(BF16)
    gw = GATE_CHUNK
    for c in range(3 * D_MODEL // gw):
        gt_ref[:, c * gw:(c + 1) * gw] = _sigmoid(seg(OFF_G + c * gw, gw)).astype(BF16)


def _inproj(x, batch_major, g, w_in, cos, sin, B, S):
    rows = PROJ_ROW_TILE

    def out(width):
        return pl.BlockSpec((rows, width), lambda b, i: (i, b))

    widths = (SSM_WIDTH, DIFF_W, DIFF_W, DIFF_W, LRU_WIDTH, LRU_WIDTH, 3 * D_MODEL)
    tab = pl.BlockSpec((None, rows, LANES), lambda b, i: (b, i, 0))
    return pl.pallas_call(
        _inproj_kernel,
        grid=(B, S // rows),
        in_specs=[_x_spec(batch_major, rows, D_MODEL), _whole((1, D_MODEL)), _wspec(w_in), tab, tab],
        out_specs=[out(w) for w in widths],
        out_shape=[jax.ShapeDtypeStruct((S, B * w), BF16) for w in widths],
        compiler_params=_params("parallel", "parallel"),
        name="inproj",
    )(x, g, _wop(w_in), cos, sin)


def _swap_matrix(batch):
    idx = jnp.arange(batch * batch)
    return (idx[:, None] == (idx[None, :] % batch) * batch + idx[None, :] // batch).astype(BF16)


def _load_time_major(ref, width, batch, steps, swap):
    subs = []
    for k in range(steps // batch):
        rows = slice(k * batch, (k + 1) * batch)
        xb = jnp.concatenate([ref[rows, b * width:(b + 1) * width] for b in range(batch)], axis=0)
        subs.append(_dot(swap, xb).astype(BF16))
    return jnp.concatenate(subs, axis=0)


def _store_batch_major(ref, y, width, batch, steps, swap):
    n = batch * batch
    for k in range(steps // batch):
        yb = _dot(swap, y[k * n:(k + 1) * n]).astype(BF16)
        for b in range(batch):
            ref[k * batch:(k + 1) * batch, b * width:(b + 1) * width] = yb[b * batch:(b + 1) * batch]


def _s5_body(u_ref, swap, wb_ref, abr_ref, abi_ref, cre_ref, cim_ref, d_ref, wglu_ref, bglu_ref,
             o_ref, sre, sim, st_re, st_im, batch, steps):
    u = _load_time_major(u_ref, SSM_WIDTH, batch, steps, swap)
    nb = SSM_BLOCK_STATES
    for j in range(SSM_LANE_BLOCKS):
        bu = _dot(u[:, j * LANES:(j + 1) * LANES], wb_ref[j])
        sre[:, j * nb:(j + 1) * nb] = bu[:, :nb]
        sim[:, j * nb:(j + 1) * nb] = bu[:, nb:]

    for j in range(SSM_LANE_BLOCKS):
        cols = slice(j * nb, (j + 1) * nb)
        ar = jnp.broadcast_to(abr_ref[:, cols], (batch, nb))
        ai = jnp.broadcast_to(abi_ref[:, cols], (batch, nb))
        sr = st_re[:, cols]
        si = st_im[:, cols]
        for t in range(steps):
            rows = slice(t * batch, (t + 1) * batch)
            nr = ar * sr - ai * si + sre[rows, cols]
            ni = ar * si + ai * sr + sim[rows, cols]
            sre[rows, cols] = nr
            sim[rows, cols] = ni
            sr, si = nr, ni
        st_re[:, cols] = sr
        st_im[:, cols] = si

    ys = []
    for j in range(SSM_LANE_BLOCKS):
        cols = slice(j * nb, (j + 1) * nb)
        ys.append(_dot(sre[:, cols].astype(BF16), cre_ref[j]) - _dot(sim[:, cols].astype(BF16), cim_ref[j]))
    y = jnp.concatenate(ys, axis=1) + d_ref[...] * u.astype(F32)
    y = _gelu(y)
    z = _dot(y.astype(BF16), wglu_ref[...]) + bglu_ref[...]
    out = (z[:, :SSM_WIDTH] * _sigmoid(z[:, SSM_WIDTH:])).astype(BF16)
    _store_batch_major(o_ref, out, SSM_WIDTH, batch, steps, swap)


def _lru_body(xr_ref, gr_ref, swap, cw_ref, cb_ref, wax_ref, ba_ref, bx_ref, lam_ref,
              o_ref, ext, a_s, b_s, h_st, batch, steps):
    rows = batch * steps
    halo = (LRU_CONV - 1) * batch
    ext[halo:halo + rows, :] = _load_time_major(xr_ref, LRU_WIDTH, batch, steps, swap).astype(F32)
    xc = cb_ref[...] + cw_ref[LRU_CONV - 1:LRU_CONV, :] * ext[halo:halo + rows, :]
    for j in range(LRU_CONV - 1):
        xc = xc + cw_ref[j:j + 1, :] * ext[j * batch:j * batch + rows, :]
    ext[0:halo, :] = ext[rows:rows + halo, :]

    z = _dot(xc.astype(BF16), wax_ref[...])
    r = _sigmoid(z[:, :LRU_WIDTH] + ba_ref[...])
    ig = _sigmoid(z[:, LRU_WIDTH:] + bx_ref[...])
    softplus_neg_lam = jnp.log1p(jnp.exp(-lam_ref[...]))
    a = jnp.exp((-LRU_C) * r * softplus_neg_lam)
    a_s[...] = a
    y = 1.0 - a * a
    b_s[...] = jnp.where(y > 0.0, y * lax.rsqrt(y), 0.0) * (ig * xc)

    h = h_st[...]
    for t in range(steps):
        rs = slice(t * batch, (t + 1) * batch)
        h = a_s[rs, :] * h + b_s[rs, :]
        b_s[rs, :] = h
    h_st[...] = h
    gr = _load_time_major(gr_ref, LRU_WIDTH, batch, steps, swap).astype(F32)
    _store_batch_major(o_ref, (b_s[...] * _gelu(gr)).astype(BF16), LRU_WIDTH, batch, steps, swap)


def _recurrent_kernel(u_ref, xr_ref, gr_ref, swap_ref,
                      wb_ref, abr_ref, abi_ref, cre_ref, cim_ref, d_ref, wglu_ref, bglu_ref,
                      cw_ref, cb_ref, wax_ref, ba_ref, bx_ref, lam_ref,
                      ys_ref, yl_ref, sre, sim, st_re, st_im, ext, a_s, b_s, h_st, *, batch, steps):
    @pl.when(pl.program_id(0) == 0)
    def _():
        st_re[...] = jnp.zeros_like(st_re)
        st_im[...] = jnp.zeros_like(st_im)
        ext[0:(LRU_CONV - 1) * batch, :] = jnp.zeros(((LRU_CONV - 1) * batch, LRU_WIDTH), F32)
        h_st[...] = jnp.zeros_like(h_st)

    swap = swap_ref[...]
    _s5_body(u_ref, swap, wb_ref, abr_ref, abi_ref, cre_ref, cim_ref, d_ref, wglu_ref, bglu_ref,
             ys_ref, sre, sim, st_re, st_im, batch, steps)
    _lru_body(xr_ref, gr_ref, swap, cw_ref, cb_ref, wax_ref, ba_ref, bx_ref, lam_ref,
              yl_ref, ext, a_s, b_s, h_st, batch, steps)


def _recurrent(u, xr, gr, swap, s5_params, lru_params, B):
    S = u.shape[0]
    rows = REC_ROW_TILE
    steps = rows // B
    halo = (LRU_CONV - 1) * B

    def blk(width):
        return pl.BlockSpec((steps, B * width), lambda i: (i, 0))

    params = tuple(s5_params) + tuple(lru_params)
    return pl.pallas_call(
        functools.partial(_recurrent_kernel, batch=B, steps=steps),
        grid=(S // steps,),
        in_specs=[blk(SSM_WIDTH), blk(LRU_WIDTH), blk(LRU_WIDTH), _whole(swap.shape)] + [_wspec(p) for p in params],
        out_specs=[blk(SSM_WIDTH), blk(LRU_WIDTH)],
        out_shape=[jax.ShapeDtypeStruct((S, B * SSM_WIDTH), BF16), jax.ShapeDtypeStruct((S, B * LRU_WIDTH), BF16)],
        scratch_shapes=[pltpu.VMEM((rows, SSM_NSTATE), F32), pltpu.VMEM((rows, SSM_NSTATE), F32),
                        pltpu.VMEM((B, SSM_NSTATE), F32), pltpu.VMEM((B, SSM_NSTATE), F32),
                        pltpu.VMEM((rows + halo, LRU_WIDTH), F32), pltpu.VMEM((rows, LRU_WIDTH), F32),
                        pltpu.VMEM((rows, LRU_WIDTH), F32), pltpu.VMEM((B, LRU_WIDTH), F32)],
        compiler_params=_params("arbitrary"),
        name="recurrent_branches",
    )(u, xr, gr, swap, *[_wop(p) for p in params])


def _diff_attn_kernel(q_ref, k_ref, v_ref, lq1_ref, lk1_ref, lq2_ref, lk2_ref, g_ref, o_ref,
                      qq, m_s, acc_s, kn_s, *, lambda_init, tk):
    qi = pl.program_id(1)
    hw = 2 * DIFF_HD
    tq = 2 * tk
    every = slice(0, 2 * tq)
    upper = slice(tq, 2 * tq)
    lane = lax.broadcasted_iota(jnp.int32, (1, hw), 1)

    @pl.when((pl.program_id(0) == 0) & (qi == 0))
    def _():
        acc_s[...] = jnp.zeros_like(acc_s)

    for h in range(DIFF_HEADS):
        for half in range(2):
            q = q_ref[half * tk:(half + 1) * tk, h * hw:(h + 1) * hw]
            zero = jnp.zeros_like(q)
            qq[h, half * tq:half * tq + tk, :] = jnp.where(lane < DIFF_HD, q, zero)
            qq[h, half * tq + tk:(half + 1) * tq, :] = jnp.where(lane >= DIFF_HD, q, zero)

    @pl.when(qi == 0)
    def _():
        kn_s[...] = jnp.zeros_like(kn_s)

    new_keys = pl.multiple_of(qi * tq, tq)
    d_row = lax.broadcasted_iota(jnp.int32, (hw, 2 * hw), 0)
    d_col = lax.broadcasted_iota(jnp.int32, (hw, 2 * hw), 1)
    comp_sum = ((d_row < DIFF_HD) == (d_col < hw)).astype(BF16)

    def max_sq_norm(x):
        sq = x * x
        fold = sq[0:ATTN_NORM_ROWS]
        for r in range(ATTN_NORM_ROWS, tq, ATTN_NORM_ROWS):
            fold = jnp.maximum(fold, sq[r:r + ATTN_NORM_ROWS])
        return jnp.max(_dot(fold, comp_sum), axis=0, keepdims=True)

    bound = []
    for h in range(DIFF_HEADS):
        kn = jnp.maximum(kn_s[h:h + 1, :], max_sq_norm(k_ref[pl.ds(new_keys, tq), h * hw:(h + 1) * hw]))
        kn_s[h:h + 1, :] = kn
        qn = max_sq_norm(q_ref[:, h * hw:(h + 1) * hw])
        n2 = qn * kn
        b = n2 * lax.rsqrt(jnp.maximum(n2, 1e-30))
        bound.append((b[:, 0:1], b[:, hw:hw + 1]))

    def scores(j, h, rows, diagonal):
        start = pl.multiple_of(j * tk, tk)
        s = _dot_nt(qq[h, rows, :], k_ref[pl.ds(start, tk), h * hw:(h + 1) * hw])
        if diagonal:
            row = lax.broadcasted_iota(jnp.int32, s.shape, 0)
            col = lax.broadcasted_iota(jnp.int32, s.shape, 1)
            seen = col <= (row & (tk - 1))
            if s.shape[0] == 2 * tq:
                seen = seen | (row >= tq)
            s = jnp.where(seen, s, -1e30)
        return s

    chunks = [slice(c * hw, (c + 1) * hw) for c in range(tk // hw)]

    def max_step(j, rows, diagonal):
        for h in range(DIFF_HEADS):
            s = scores(j, h, rows, diagonal)
            m = m_s[h, rows, :]
            for c in chunks:
                m = jnp.maximum(m, s[:, c])
            m_s[h, rows, :] = m

    def shift_bound(h, rows, s):
        return jnp.concatenate([s[i * tk:(i + 1) * tk] - bound[h][i % 2] for i in range(s.shape[0] // tk)], axis=0)

    def shift_row_max(h, rows, s):
        m = m_s[h, rows, :]
        return jnp.concatenate([s[:, c] - m for c in chunks], axis=1)

    def acc_step(shift, j, rows, diagonal):
        start = pl.multiple_of(j * tk, tk)
        ones = jnp.ones((tk, hw), BF16)
        for h in range(DIFF_HEADS):
            p = jnp.exp(shift(h, rows, scores(j, h, rows, diagonal))).astype(BF16)
            v_ext = jnp.concatenate([v_ref[pl.ds(start, tk), h * hw:(h + 1) * hw], ones], axis=1)
            acc_s[h, rows, :] = acc_s[h, rows, :] + _dot(p, v_ext)

    def loop(step):
        def body(j, c):
            step(j, every, False)
            return c
        lax.fori_loop(0, 2 * qi, body, 0)
        step(2 * qi, every, True)
        step(2 * qi + 1, upper, True)

    loop(functools.partial(acc_step, shift_bound))
    row_sum_min = jnp.min(acc_s[0][:, hw:])
    for h in range(1, DIFF_HEADS):
        row_sum_min = jnp.minimum(row_sum_min, jnp.min(acc_s[h][:, hw:]))

    @pl.when(jnp.logical_not(row_sum_min >= ATTN_MIN_ROW_SUM))
    def _():
        m_s[...] = jnp.full(m_s.shape, -1e30, F32)
        acc_s[...] = jnp.zeros_like(acc_s)
        loop(max_step)
        for h in range(DIFF_HEADS):
            m_s[h] = jnp.broadcast_to(jnp.max(m_s[h], axis=-1, keepdims=True), (2 * tq, hw))
        loop(functools.partial(acc_step, shift_row_max))

    lam = (jnp.exp(jnp.sum(lq1_ref[...] * lk1_ref[...], axis=-1, keepdims=True))
           - jnp.exp(jnp.sum(lq2_ref[...] * lk2_ref[...], axis=-1, keepdims=True)) + lambda_init)
    for h in range(DIFF_HEADS):
        for half in range(2):
            acc = acc_s[h, half * tq:(half + 1) * tq, :]
            acc_s[h, half * tq:(half + 1) * tq, :] = jnp.zeros_like(acc)
            o = acc[:, :hw] / acc[:, hw:]
            o = o[:tk] - lam * o[tk:]
            o_ref[half * tk:(half + 1) * tk, h * hw:(h + 1) * hw] = (
                _rms(o, g_ref[...]) * (1.0 - lambda_init)).astype(BF16)


def _diff_attn(q, k, v, lq1, lk1, lq2, lk2, g, lambda_init, B, S):
    tk = ATTN_TK
    tq = 2 * tk
    hw = 2 * DIFF_HD
    qspec = pl.BlockSpec((tq, DIFF_W), lambda b, i: (i, b))
    kspec = pl.BlockSpec((S, DIFF_W), lambda b, i: (0, b))
    small = pl.BlockSpec((1, DIFF_HD), lambda b, i: (0, 0))
    return pl.pallas_call(
        functools.partial(_diff_attn_kernel, lambda_init=lambda_init, tk=tk),
        grid=(B, S // tq),
        in_specs=[qspec, kspec, kspec, small, small, small, small,
                  pl.BlockSpec((1, hw), lambda b, i: (0, 0))],
        out_specs=qspec,
        out_shape=jax.ShapeDtypeStruct((S, B * DIFF_W), BF16),
        scratch_shapes=[pltpu.VMEM((DIFF_HEADS, 2 * tq, hw), BF16), pltpu.VMEM((DIFF_HEADS, 2 * tq, hw), F32),
                        pltpu.VMEM((DIFF_HEADS, 2 * tq, 2 * hw), F32), pltpu.VMEM((DIFF_HEADS, 2 * hw), F32)],
        compiler_params=_params("arbitrary", "arbitrary"),
        name="diff_attn",
    )(q, k, v, lq1, lk1, lq2, lk2, g)


def _merge_xattn_kernel(x_ref, ys_ref, ya_ref, yl_ref, gt_ref, wbs_ref, wba_ref, wbl_ref, wout_ref,
                        g_ref, wq_ref, k_ref, v_ref, wo_ref, o_ref):
    d = D_MODEL
    m = gt_ref[:, 0:d].astype(F32) * _dot(ys_ref[...], wbs_ref[...])
    m = m + gt_ref[:, d:2 * d].astype(F32) * _dot(ya_ref[...], wba_ref[...])
    m = m + gt_ref[:, 2 * d:3 * d].astype(F32) * _dot(yl_ref[...], wbl_ref[...])
    x = x_ref[...] + _dot(m.astype(BF16), wout_ref[...])

    hn = _rms(x, g_ref[...]).astype(BF16)
    q = (_dot(hn, wq_ref[...]) * (XATTN_HD ** -0.5)).astype(BF16)
    outs = []
    for h in range(XATTN_HEADS):
        cols = slice(h * XATTN_HD, (h + 1) * XATTN_HD)
        s = _dot_nt(q[:, cols], k_ref[:, cols])
        p = jnp.exp(s - jnp.max(s, axis=-1, keepdims=True))
        p = p / jnp.sum(p, axis=-1, keepdims=True)
        outs.append(_dot(p.astype(BF16), v_ref[:, cols]).astype(BF16))
    o_ref[...] = x + _dot(jnp.concatenate(outs, axis=1), wo_ref[...])


def _merge_xattn(x, batch_major, ys, ya, yl, gt, wbs, wba, wbl, wout, g, wq, kv, wo, mem_len, B, S):
    rows = PROJ_ROW_TILE

    def blk(width):
        return pl.BlockSpec((rows, width), lambda b, i: (i, b))

    return pl.pallas_call(
        _merge_xattn_kernel,
        grid=(B, S // rows),
        in_specs=[_x_spec(batch_major, rows, D_MODEL), blk(SSM_WIDTH), blk(DIFF_W), blk(LRU_WIDTH),
                  blk(3 * D_MODEL), _wspec(wbs), _wspec(wba), _wspec(wbl), _wspec(wout),
                  _whole(g.shape), _wspec(wq),
                  pl.BlockSpec((mem_len, D_MODEL), lambda b, i: (b, 0)),
                  pl.BlockSpec((mem_len, D_MODEL), lambda b, i: (b, 1)),
                  _wspec(wo)],
        out_specs=blk(D_MODEL),
        out_shape=jax.ShapeDtypeStruct((S, B * D_MODEL), F32),
        compiler_params=_params("parallel", "parallel"),
        name="merge_xattn",
    )(x, ys, ya, yl, gt, _wop(wbs), _wop(wba), _wop(wbl), _wop(wout), g, _wop(wq), kv, kv, _wop(wo))


def _memkv_kernel(m_ref, g_ref, w_ref, o_ref):
    hn = _rms(m_ref[...], g_ref[...]).astype(BF16)
    o_ref[...] = _dot(hn, w_ref[...]).astype(BF16)


def _memkv(mem_rows, g, wkv):
    n = mem_rows.shape[0]
    rows = ROW_TILE
    return pl.pallas_call(
        _memkv_kernel,
        grid=(n // rows,),
        in_specs=[pl.BlockSpec((rows, D_MODEL), lambda i: (i, 0)), _whole(g.shape), _wspec(wkv)],
        out_specs=pl.BlockSpec((rows, 2 * D_MODEL), lambda i: (i, 0)),
        out_shape=jax.ShapeDtypeStruct((n, 2 * D_MODEL), BF16),
        compiler_params=_params("parallel"),
        name="mem_kv",
    )(mem_rows, g, _wop(wkv))


def _ffn_kernel(x_ref, swap_ref, g_ref, wup_ref, cw_ref, cb_ref, wdn_ref, gf_ref, o_ref, prev, act_s,
                *, batch, steps, final):
    rows = batch * steps
    d = D_MODEL
    halo = (FFN_CONV - 1) * batch
    n = batch * batch

    @pl.when(pl.program_id(0) == 0)
    def _():
        prev[...] = jnp.zeros_like(prev)

    swap = swap_ref[...]
    g = g_ref[...]
    hn_b = [_rms(x_ref[:, b * d:(b + 1) * d], g).astype(BF16) for b in range(batch)]
    subs = []
    for k in range(steps // batch):
        hb = jnp.concatenate([h[k * batch:(k + 1) * batch] for h in hn_b], axis=0)
        subs.append(_dot(swap, hb).astype(BF16))
    hn = jnp.concatenate(subs, axis=0)

    def conv(up, part, cols):
        hist = prev[part]
        y = cb_ref[:, cols] + cw_ref[FFN_CONV - 1:FFN_CONV, cols] * up
        for t in range(FFN_CONV - 1):
            back = (FFN_CONV - 1 - t) * batch
            shifted = jnp.concatenate([hist[halo - back:], up[:rows - back]], axis=0)
            y = y + cw_ref[t:t + 1, cols] * shifted
        prev[part] = up[rows - halo:]
        return y

    for j in range(FFN_NCHUNK):
        vc = slice(j * FFN_CHUNK, (j + 1) * FFN_CHUNK)
        gc = slice(D_FF + j * FFN_CHUNK, D_FF + (j + 1) * FFN_CHUNK)
        val = conv(_dot(hn, wup_ref[:, vc]), 2 * j, vc)
        gate = conv(_dot(hn, wup_ref[:, gc]), 2 * j + 1, gc)
        act_s[:, vc] = (gate * _sigmoid(gate) * val).astype(BF16)
    acc = _dot(act_s[...], wdn_ref[...])

    hi = acc.astype(BF16)
    lo = (acc - hi.astype(F32)).astype(BF16)
    for k in range(steps // batch):
        yb = _dot(swap, hi[k * n:(k + 1) * n]) + _dot(swap, lo[k * n:(k + 1) * n])
        ts = slice(k * batch, (k + 1) * batch)
        for b in range(batch):
            out = x_ref[ts, b * d:(b + 1) * d] + yb[b * batch:(b + 1) * batch]
            if final:
                o_ref[b, ts, :] = _rms(out, gf_ref[...])
            else:
                o_ref[ts, b * d:(b + 1) * d] = out


def _ffn(x, swap, g, wup, cw, cb, wdn, gf, B, S, final):
    rows = FFN_ROW_TILE
    steps = rows // B
    blk = pl.BlockSpec((steps, B * D_MODEL), lambda i: (i, 0))
    if final:
        out_spec = pl.BlockSpec((B, steps, D_MODEL), lambda i: (0, i, 0))
        out_shape = jax.ShapeDtypeStruct((B, S, D_MODEL), F32)
    else:
        out_spec, out_shape = blk, jax.ShapeDtypeStruct((S, B * D_MODEL), F32)
    return pl.pallas_call(
        functools.partial(_ffn_kernel, batch=B, steps=steps, final=final),
        grid=(S // steps,),
        in_specs=[blk, _whole(swap.shape), _whole(g.shape), _wspec(wup), _whole(cw.shape), _whole(cb.shape),
                  _wspec(wdn), _whole(gf.shape)],
        out_specs=out_spec,
        out_shape=out_shape,
        scratch_shapes=[pltpu.VMEM((2 * FFN_NCHUNK, (FFN_CONV - 1) * B, FFN_CHUNK), F32),
                        pltpu.VMEM((rows, D_FF), BF16)],
        compiler_params=_params("arbitrary"),
        name="conv_ffn",
    )(x, swap, g, _wop(wup), cw, cb, _wop(wdn), gf)


def _row(v):
    return v.reshape(1, -1).astype(F32)


def _block_diag(blocks):
    n, r, c = blocks.shape
    eye = jnp.eye(n, dtype=blocks.dtype)
    return (blocks[:, :, None, :] * eye[:, None, :, None]).reshape(n * r, n * c)


def _s5_b_weights(bbr, bbi):
    per = SSM_GROUPS // SSM_LANE_BLOCKS
    out = []
    for j in range(SSM_LANE_BLOCKS):
        r = bbr[j * LANES:(j + 1) * LANES].reshape(per, SSM_GROUP, SSM_STATE)
        i = bbi[j * LANES:(j + 1) * LANES].reshape(per, SSM_GROUP, SSM_STATE)
        out.append(jnp.concatenate([_block_diag(r), _block_diag(i)], axis=1))
    return jnp.stack(out).astype(BF16)


def _s5_c_weights(c):
    per = SSM_GROUPS // SSM_LANE_BLOCKS
    ct = jnp.swapaxes(c, 1, 2)
    return jnp.stack([_block_diag(ct[j * per:(j + 1) * per]) for j in range(SSM_LANE_BLOCKS)]).astype(BF16)


def kernel(x, mem, positions, norm_mix_g, w_in, ssm_lambda_re, ssm_lambda_im, ssm_log_step, ssm_b_re, ssm_b_im, ssm_c_re, ssm_c_im, ssm_d, ssm_w_glu, ssm_b_glu, diff_lq1, diff_lk1, diff_lq2, diff_lk2, diff_subln_g, lru_conv_w, lru_conv_b, lru_wa, lru_ba, lru_wx, lru_bx, lru_lambda, w_br_ssm, w_br_attn, w_br_lru, w_out, norm_xattn_g, norm_mem_g, xattn_wq, xattn_wkv, xattn_wo, norm_ffn_g, ffn_w_up, ffn_conv_w, ffn_conv_b, ffn_w_down, final_norm_g):
    B, S, _ = x.shape
    depth = norm_mix_g.shape[0]
    mem_len = mem.shape[1]
    assert S % ROW_TILE == 0 and ROW_TILE % B == 0 and B % 8 == 0 and (B * mem_len) % ROW_TILE == 0
    assert S % (2 * ATTN_TK) == 0 and FFN_ROW_TILE % (B * B) == 0 and (S * B) % FFN_ROW_TILE == 0
    assert REC_ROW_TILE % (B * B) == 0 and (S * B) % REC_ROW_TILE == 0 and S % PROJ_ROW_TILE == 0

    cos, sin = _rope_tables(positions.astype(F32)[..., None], B, S)

    rep = lambda a: jnp.repeat(a, SSM_GROUP, axis=1)
    b_t = lambda a: jnp.swapaxes(a, 2, 3).reshape(depth, SSM_WIDTH, SSM_STATE)
    abr, abi, bbr, bbi = _s5_prep(rep(ssm_lambda_re), rep(ssm_lambda_im), rep(ssm_log_step[..., None]),
                                  b_t(ssm_b_re), b_t(ssm_b_im))
    mem_rows = mem.reshape(B * mem_len, D_MODEL)
    swap = _swap_matrix(B)
    (w_in, ssm_w_glu, w_br_ssm, w_br_attn, w_br_lru, w_out, xattn_wq, xattn_wkv, xattn_wo, ffn_w_up,
     ffn_w_down) = (a.astype(BF16) for a in (w_in, ssm_w_glu, w_br_ssm, w_br_attn, w_br_lru, w_out, xattn_wq,
                                             xattn_wkv, xattn_wo, ffn_w_up, ffn_w_down))

    xs = x
    batch_major = True
    for l in range(depth):
        lambda_init = 0.8 - 0.6 * math.exp(-0.3 * l)
        u, q, k, v, xr, gr, gt = _inproj(xs, batch_major, _row(norm_mix_g[l]), _Layer(w_in, l), cos, sin, B, S)

        s5_params = (_s5_b_weights(bbr[l], bbi[l]),
                     abr[l, ::SSM_GROUP].reshape(1, SSM_NSTATE), abi[l, ::SSM_GROUP].reshape(1, SSM_NSTATE),
                     _s5_c_weights(ssm_c_re[l]), _s5_c_weights(ssm_c_im[l]), _row(ssm_d[l]),
                     _Layer(ssm_w_glu, l), _row(ssm_b_glu[l]))
        wax = jnp.concatenate([_block_diag(lru_wa[l]), _block_diag(lru_wx[l])], axis=1).astype(BF16)
        lru_params = (lru_conv_w[l].astype(F32), _row(lru_conv_b[l]), wax, _row(lru_ba[l]), _row(lru_bx[l]),
                      _row(lru_lambda[l]))
        y_ssm, y_lru = _recurrent(u, xr, gr, swap, s5_params, lru_params, B)

        y_att = _diff_attn(q, k, v, _row(diff_lq1[l]), _row(diff_lk1[l]), _row(diff_lq2[l]), _row(diff_lk2[l]),
                           _row(diff_subln_g[l]), lambda_init, B, S)

        kv = _memkv(mem_rows, _row(norm_mem_g[l]), _Layer(xattn_wkv, l))
        xs = _merge_xattn(xs, batch_major, y_ssm, y_att, y_lru, gt, _Layer(w_br_ssm, l), _Layer(w_br_attn, l),
                          _Layer(w_br_lru, l), _Layer(w_out, l), _row(norm_xattn_g[l]), _Layer(xattn_wq, l), kv,
                          _Layer(xattn_wo, l), mem_len, B, S)
        batch_major = False

        xs = _ffn(xs, swap, _row(norm_ffn_g[l]), _Layer(ffn_w_up, l), ffn_conv_w[l].astype(F32),
                  _row(ffn_conv_b[l]), _Layer(ffn_w_down, l), _row(final_norm_g), B, S, final=l == depth - 1)

    return xs
```

```python
import collections
import functools
import math

import jax
import jax.numpy as jnp
from jax import lax
from jax.experimental import pallas as pl
from jax.experimental.pallas import tpu as pltpu

F32 = jnp.float32
BF16 = jnp.bfloat16

EPS = 1e-6
D_MODEL = 1024
SSM_WIDTH = 384
SSM_GROUP = 16
SSM_GROUPS = 24
SSM_STATE = 64
SSM_NSTATE = SSM_GROUPS * SSM_STATE
SSM_LANE_BLOCKS = 3
DIFF_HEADS = 4
DIFF_HD = 64
DIFF_W = 512
ROPE_THETA = 10000.0
LRU_WIDTH = 512
LRU_CONV = 4
LRU_C = 8.0
XATTN_HEADS = 4
XATTN_HD = 256
D_FF = 2816
FFN_CONV = 3
FFN_CHUNK = 256
FFN_NCHUNK = D_FF // FFN_CHUNK
OFF_U, OFF_Q, OFF_K, OFF_V, OFF_XR, OFF_GR, OFF_G = 0, 384, 896, 1408, 1920, 2432, 2944
D_IN = 6016
GATE_CHUNK = 512

LANES = 128
SSM_BLOCK_STATES = SSM_NSTATE // SSM_LANE_BLOCKS

ROW_TILE = 512
PROJ_ROW_TILE = 1024
FFN_ROW_TILE = 1024
REC_ROW_TILE = 1024
ATTN_TK = 512
ATTN_NORM_ROWS = 128
ATTN_MIN_ROW_SUM = 1e-25
VMEM_LIMIT = 56 * 1024 * 1024


def _dot(a, b):
    return jnp.dot(a, b, preferred_element_type=F32)


def _dot_nt(a, b):
    return lax.dot_general(a, b, (((1,), (1,)), ((), ())), preferred_element_type=F32)


def _rms(x, g):
    ms = jnp.mean(x * x, axis=-1, keepdims=True)
    return x * lax.rsqrt(ms + EPS) * g


def _sigmoid(x):
    return 1.0 / (1.0 + jnp.exp(-x))


def _gelu(x):
    return 0.5 * x * (1.0 + jnp.tanh(0.7978845608028654 * (x + 0.044715 * (x * x * x))))


def _params(*sem):
    return pltpu.CompilerParams(dimension_semantics=sem, vmem_limit_bytes=VMEM_LIMIT)


def _whole(shape):
    zeros = (0,) * len(shape)
    return pl.BlockSpec(shape, lambda *_: zeros, pipeline_mode=pl.Buffered(1))


_Layer = collections.namedtuple("_Layer", "stack index")


def _wspec(w):
    if isinstance(w, _Layer):
        shape = w.stack.shape[1:]
        idx = (w.index,) + (0,) * len(shape)
        return pl.BlockSpec((None,) + shape, lambda *_: idx, pipeline_mode=pl.Buffered(1))
    return _whole(w.shape)


def _wop(w):
    return w.stack if isinstance(w, _Layer) else w


def _x_spec(batch_major, rows, width):
    if batch_major:
        return pl.BlockSpec((None, rows, width), lambda b, i: (b, i, 0))
    return pl.BlockSpec((rows, width), lambda b, i: (i, b))


def _rope_kernel(pos_ref, cos_ref, sin_ref):
    half = DIFF_HD // 2
    lane = lax.broadcasted_iota(jnp.int32, (1, LANES), 1)
    j = (lane & (half - 1)).astype(F32)
    inv = jnp.exp((-math.log(ROPE_THETA) * (2.0 * j)) / DIFF_HD)
    ang = pos_ref[...] * inv
    first_half = (lane & (DIFF_HD - 1)) < half
    cos_ref[...] = jnp.cos(ang)
    s = jnp.sin(ang)
    sin_ref[...] = jnp.where(first_half, -s, s)


def _rope_tables(pos_f, B, S):
    rows = ROW_TILE
    spec = pl.BlockSpec((None, rows, LANES), lambda b, i: (b, i, 0))
    return pl.pallas_call(
        _rope_kernel,
        grid=(B, S // rows),
        in_specs=[pl.BlockSpec((None, rows, 1), lambda b, i: (b, i, 0))],
        out_specs=[spec, spec],
        out_shape=[jax.ShapeDtypeStruct((B, S, LANES), F32)] * 2,
        compiler_params=_params("parallel", "parallel"),
        name="rope_tables",
    )(pos_f)


def _s5_prep_kernel(lr_ref, li_ref, ls_ref, br_ref, bi_ref, abr_ref, abi_ref, bbr_ref, bbi_ref):
    lr = lr_ref[...]
    li = li_ref[...]
    dt = jnp.exp(ls_ref[...])
    mag = jnp.exp(lr * dt)
    ab_r = mag * jnp.cos(li * dt)
    ab_i = mag * jnp.sin(li * dt)
    den = lr * lr + li * li
    nr = ab_r - 1.0
    f_r = (nr * lr + ab_i * li) / den
    f_i = (ab_i * lr - nr * li) / den
    br = br_ref[...]
    bi = bi_ref[...]
    abr_ref[...] = ab_r
    abi_ref[...] = ab_i
    bbr_ref[...] = f_r * br - f_i * bi
    bbi_ref[...] = f_r * bi + f_i * br


def _s5_prep(lr_rep, li_rep, ls_rep, br_t, bi_t):
    depth = lr_rep.shape[0]
    spec = pl.BlockSpec((None, SSM_WIDTH, SSM_STATE), lambda l: (l, 0, 0))
    return pl.pallas_call(
        _s5_prep_kernel,
        grid=(depth,),
        in_specs=[spec, spec, pl.BlockSpec((None, SSM_WIDTH, 1), lambda l: (l, 0, 0)), spec, spec],
        out_specs=[spec] * 4,
        out_shape=[jax.ShapeDtypeStruct((depth, SSM_WIDTH, SSM_STATE), F32)] * 4,
        compiler_params=_params("parallel"),
        name="s5_prep",
    )(lr_rep, li_rep, ls_rep, br_t, bi_t)


def _rope(x, cos, sin_signed, first_half):
    half = DIFF_HD // 2
    back = pltpu.roll(x, half, 1)
    fwd = pltpu.roll(x, DIFF_W - half, 1)
    return x * cos + jnp.where(first_half, fwd, back) * sin_signed


def _inproj_kernel(x_ref, g_ref, w_ref, cos_ref, sin_ref,
                   u_ref, q_ref, k_ref, v_ref, xr_ref, gr_ref, gt_ref):
    hn = _rms(x_ref[...], g_ref[...]).astype(BF16)

    def seg(off, width):
        return _dot(hn, w_ref[:, off:off + width])

    cos = jnp.concatenate([cos_ref[...]] * (DIFF_W // LANES), axis=1)
    sin = jnp.concatenate([sin_ref[...]] * (DIFF_W // LANES), axis=1)
    lane = lax.broadcasted_iota(jnp.int32, (1, DIFF_W), 1)
    first_half = (lane & (DIFF_HD - 1)) < DIFF_HD // 2

    u_ref[...] = seg(OFF_U, SSM_WIDTH).astype(BF16)
    q_ref[...] = (_rope(seg(OFF_Q, DIFF_W), cos, sin, first_half) * (DIFF_HD ** -0.5)).astype(BF16)
    k_ref[...] = _rope(seg(OFF_K, DIFF_W), cos, sin, first_half).astype(BF16)
    v_ref[...] = seg(OFF_V, DIFF_W).astype(BF16)
    xr_ref[...] = seg(OFF_XR, LRU_WIDTH).astype(BF16)
    gr_ref[...] = seg(OFF_GR, LRU_WIDTH).astype(BF16)
    gw = GATE_CHUNK
    for c in range(3 * D_MODEL // gw):
        gt_ref[:, c * gw:(c + 1) * gw] = _sigmoid(seg(OFF_G + c * gw, gw)).astype(BF16)


def _inproj(x, batch_major, g, w_in, cos, sin, B, S):
    rows = PROJ_ROW_TILE

    def out(width):
        return pl.BlockSpec((rows, width), lambda b, i: (i, b))

    widths = (SSM_WIDTH, DIFF_W, DIFF_W, DIFF_W, LRU_WIDTH, LRU_WIDTH, 3 * D_MODEL)
    tab = pl.BlockSpec((None, rows, LANES), lambda b, i: (b, i, 0))
    return pl.pallas_call(
        _inproj_kernel,
        grid=(B, S // rows),
        in_specs=[_x_spec(batch_major, rows, D_MODEL), _whole((1, D_MODEL)), _wspec(w_in), tab, tab],
        out_specs=[out(w) for w in widths],
        out_shape=[jax.ShapeDtypeStruct((S, B * w), BF16) for w in widths],
        compiler_params=_params("parallel", "parallel"),
        name="inproj",
    )(x, g, _wop(w_in), cos, sin)


def _swap_matrix(batch):
    idx = jnp.arange(batch * batch)
    return (idx[:, None] == (idx[None, :] % batch) * batch + idx[None, :] // batch).astype(BF16)


def _load_time_major(ref, width, batch, steps, swap):
    subs = []
    for k in range(steps // batch):
        rows = slice(k * batch, (k + 1) * batch)
        xb = jnp.concatenate([ref[rows, b * width:(b + 1) * width] for b in range(batch)], axis=0)
        subs.append(_dot(swap, xb).astype(BF16))
    return jnp.concatenate(subs, axis=0)


def _store_batch_major(ref, y, width, batch, steps, swap):
    n = batch * batch
    for k in range(steps // batch):
        yb = _dot(swap, y[k * n:(k + 1) * n]).astype(BF16)
        for b in range(batch):
            ref[k * batch:(k + 1) * batch, b * width:(b + 1) * width] = yb[b * batch:(b + 1) * batch]


def _s5_body(u_ref, swap, wb_ref, abr_ref, abi_ref, cre_ref, cim_ref, d_ref, wglu_ref, bglu_ref,
             o_ref, sre, sim, st_re, st_im, batch, steps):
    u = _load_time_major(u_ref, SSM_WIDTH, batch, steps, swap)
    nb = SSM_BLOCK_STATES
    for j in range(SSM_LANE_BLOCKS):
        bu = _dot(u[:, j * LANES:(j + 1) * LANES], wb_ref[j])
        sre[:, j * nb:(j + 1) * nb] = bu[:, :nb]
        sim[:, j * nb:(j + 1) * nb] = bu[:, nb:]

    for j in range(SSM_LANE_BLOCKS):
        cols = slice(j * nb, (j + 1) * nb)
        ar = jnp.broadcast_to(abr_ref[:, cols], (batch, nb))
        ai = jnp.broadcast_to(abi_ref[:, cols], (batch, nb))
        sr = st_re[:, cols]
        si = st_im[:, cols]
        for t in range(steps):
            rows = slice(t * batch, (t + 1) * batch)
            nr = ar * sr - ai * si + sre[rows, cols]
            ni = ar * si + ai * sr + sim[rows, cols]
            sre[rows, cols] = nr
            sim[rows, cols] = ni
            sr, si = nr, ni
        st_re[:, cols] = sr
        st_im[:, cols] = si

    ys = []
    for j in range(SSM_LANE_BLOCKS):
        cols = slice(j * nb, (j + 1) * nb)
        ys.append(_dot(sre[:, cols].astype(BF16), cre_ref[j]) - _dot(sim[:, cols].astype(BF16), cim_ref[j]))
    y = jnp.concatenate(ys, axis=1) + d_ref[...] * u.astype(F32)
    y = _gelu(y)
    z = _dot(y.astype(BF16), wglu_ref[...]) + bglu_ref[...]
    out = (z[:, :SSM_WIDTH] * _sigmoid(z[:, SSM_WIDTH:])).astype(BF16)
    _store_batch_major(o_ref, out, SSM_WIDTH, batch, steps, swap)


def _lru_body(xr_ref, gr_ref, swap, cw_ref, cb_ref, wax_ref, ba_ref, bx_ref, lam_ref,
              o_ref, ext, a_s, b_s, h_st, batch, steps):
    rows = batch * steps
    halo = (LRU_CONV - 1) * batch
    ext[halo:halo + rows, :] = _load_time_major(xr_ref, LRU_WIDTH, batch, steps, swap).astype(F32)
    xc = cb_ref[...] + cw_ref[LRU_CONV - 1:LRU_CONV, :] * ext[halo:halo + rows, :]
    for j in range(LRU_CONV - 1):
        xc = xc + cw_ref[j:j + 1, :] * ext[j * batch:j * batch + rows, :]
    ext[0:halo, :] = ext[rows:rows + halo, :]

    z = _dot(xc.astype(BF16), wax_ref[...])
    r = _sigmoid(z[:, :LRU_WIDTH] + ba_ref[...])
    ig = _sigmoid(z[:, LRU_WIDTH:] + bx_ref[...])
    softplus_neg_lam = jnp.log1p(jnp.exp(-lam_ref[...]))
    a = jnp.exp((-LRU_C) * r * softplus_neg_lam)
    a_s[...] = a
    y = 1.0 - a * a
    b_s[...] = jnp.where(y > 0.0, y * lax.rsqrt(y), 0.0) * (ig * xc)

    h = h_st[...]
    for t in range(steps):
        rs = slice(t * batch, (t + 1) * batch)
        h = a_s[rs, :] * h + b_s[rs, :]
        b_s[rs, :] = h
    h_st[...] = h
    gr = _load_time_major(gr_ref, LRU_WIDTH, batch, steps, swap).astype(F32)
    _store_batch_major(o_ref, (b_s[...] * _gelu(gr)).astype(BF16), LRU_WIDTH, batch, steps, swap)


def _recurrent_kernel(u_ref, xr_ref, gr_ref, swap_ref,
                      wb_ref, abr_ref, abi_ref, cre_ref, cim_ref, d_ref, wglu_ref, bglu_ref,
                      cw_ref, cb_ref, wax_ref, ba_ref, bx_ref, lam_ref,
                      ys_ref, yl_ref, sre, sim, st_re, st_im, ext, a_s, b_s, h_st, *, batch, steps):
    @pl.when(pl.program_id(0) == 0)
    def _():
        st_re[...] = jnp.zeros_like(st_re)
        st_im[...] = jnp.zeros_like(st_im)
        ext[0:(LRU_CONV - 1) * batch, :] = jnp.zeros(((LRU_CONV - 1) * batch, LRU_WIDTH), F32)
        h_st[...] = jnp.zeros_like(h_st)

    swap = swap_ref[...]
    _s5_body(u_ref, swap, wb_ref, abr_ref, abi_ref, cre_ref, cim_ref, d_ref, wglu_ref, bglu_ref,
             ys_ref, sre, sim, st_re, st_im, batch, steps)
    _lru_body(xr_ref, gr_ref, swap, cw_ref, cb_ref, wax_ref, ba_ref, bx_ref, lam_ref,
              yl_ref, ext, a_s, b_s, h_st, batch, steps)


def _recurrent(u, xr, gr, swap, s5_params, lru_params, B):
    S = u.shape[0]
    rows = REC_ROW_TILE
    steps = rows // B
    halo = (LRU_CONV - 1) * B

    def blk(width):
        return pl.BlockSpec((steps, B * width), lambda i: (i, 0))

    params = tuple(s5_params) + tuple(lru_params)
    return pl.pallas_call(
        functools.partial(_recurrent_kernel, batch=B, steps=steps),
        grid=(S // steps,),
        in_specs=[blk(SSM_WIDTH), blk(LRU_WIDTH), blk(LRU_WIDTH), _whole(swap.shape)] + [_wspec(p) for p in params],
        out_specs=[blk(SSM_WIDTH), blk(LRU_WIDTH)],
        out_shape=[jax.ShapeDtypeStruct((S, B * SSM_WIDTH), BF16), jax.ShapeDtypeStruct((S, B * LRU_WIDTH), BF16)],
        scratch_shapes=[pltpu.VMEM((rows, SSM_NSTATE), F32), pltpu.VMEM((rows, SSM_NSTATE), F32),
                        pltpu.VMEM((B, SSM_NSTATE), F32), pltpu.VMEM((B, SSM_NSTATE), F32),
                        pltpu.VMEM((rows + halo, LRU_WIDTH), F32), pltpu.VMEM((rows, LRU_WIDTH), F32),
                        pltpu.VMEM((rows, LRU_WIDTH), F32), pltpu.VMEM((B, LRU_WIDTH), F32)],
        compiler_params=_params("arbitrary"),
        name="recurrent_branches",
    )(u, xr, gr, swap, *[_wop(p) for p in params])


def _diff_attn_kernel(q_ref, k_ref, v_ref, lq1_ref, lk1_ref, lq2_ref, lk2_ref, g_ref, o_ref,
                      qq, m_s, acc_s, kn_s, *, lambda_init, tk):
    qi = pl.program_id(1)
    hw = 2 * DIFF_HD
    tq = 2 * tk
    every = slice(0, 2 * tq)
    upper = slice(tq, 2 * tq)
    lane = lax.broadcasted_iota(jnp.int32, (1, hw), 1)

    @pl.when((pl.program_id(0) == 0) & (qi == 0))
    def _():
        acc_s[...] = jnp.zeros_like(acc_s)

    for h in range(DIFF_HEADS):
        for half in range(2):
            q = q_ref[half * tk:(half + 1) * tk, h * hw:(h + 1) * hw]
            zero = jnp.zeros_like(q)
            qq[h, half * tq:half * tq + tk, :] = jnp.where(lane < DIFF_HD, q, zero)
            qq[h, half * tq + tk:(half + 1) * tq, :] = jnp.where(lane >= DIFF_HD, q, zero)

    @pl.when(qi == 0)
    def _():
        kn_s[...] = jnp.zeros_like(kn_s)

    new_keys = pl.multiple_of(qi * tq, tq)
    d_row = lax.broadcasted_iota(jnp.int32, (hw, 2 * hw), 0)
    d_col = lax.broadcasted_iota(jnp.int32, (hw, 2 * hw), 1)
    comp_sum = ((d_row < DIFF_HD) == (d_col < hw)).astype(BF16)

    def max_sq_norm(x):
        sq = x * x
        fold = sq[0:ATTN_NORM_ROWS]
        for r in range(ATTN_NORM_ROWS, tq, ATTN_NORM_ROWS):
            fold = jnp.maximum(fold, sq[r:r + ATTN_NORM_ROWS])
        return jnp.max(_dot(fold, comp_sum), axis=0, keepdims=True)

    bound = []
    for h in range(DIFF_HEADS):
        kn = jnp.maximum(kn_s[h:h + 1, :], max_sq_norm(k_ref[pl.ds(new_keys, tq), h * hw:(h + 1) * hw]))
        kn_s[h:h + 1, :] = kn
        qn = max_sq_norm(q_ref[:, h * hw:(h + 1) * hw])
        n2 = qn * kn
        b = n2 * lax.rsqrt(jnp.maximum(n2, 1e-30))
        bound.append((b[:, 0:1], b[:, hw:hw + 1]))

    def scores(j, h, rows, diagonal):
        start = pl.multiple_of(j * tk, tk)
        s = _dot_nt(qq[h, rows, :], k_ref[pl.ds(start, tk), h * hw:(h + 1) * hw])
        if diagonal:
            row = lax.broadcasted_iota(jnp.int32, s.shape, 0)
            col = lax.broadcasted_iota(jnp.int32, s.shape, 1)
            seen = col <= (row & (tk - 1))
            if s.shape[0] == 2 * tq:
                seen = seen | (row >= tq)
            s = jnp.where(seen, s, -1e30)
        return s

    chunks = [slice(c * hw, (c + 1) * hw) for c in range(tk // hw)]

    def max_step(j, rows, diagonal):
        for h in range(DIFF_HEADS):
            s = scores(j, h, rows, diagonal)
            m = m_s[h, rows, :]
            for c in chunks:
                m = jnp.maximum(m, s[:, c])
            m_s[h, rows, :] = m

    def shift_bound(h, rows, s):
        return jnp.concatenate([s[i * tk:(i + 1) * tk] - bound[h][i % 2] for i in range(s.shape[0] // tk)], axis=0)

    def shift_row_max(h, rows, s):
        m = m_s[h, rows, :]
        return jnp.concatenate([s[:, c] - m for c in chunks], axis=1)

    def acc_step(shift, j, rows, diagonal):
        start = pl.multiple_of(j * tk, tk)
        ones = jnp.ones((tk, hw), BF16)
        for h in range(DIFF_HEADS):
            p = jnp.exp(shift(h, rows, scores(j, h, rows, diagonal))).astype(BF16)
            v_ext = jnp.concatenate([v_ref[pl.ds(start, tk), h * hw:(h + 1) * hw], ones], axis=1)
            acc_s[h, rows, :] = acc_s[h, rows, :] + _dot(p, v_ext)

    def loop(step):
        def body(j, c):
            step(j, every, False)
            return c
        lax.fori_loop(0, 2 * qi, body, 0)
        step(2 * qi, every, True)
        step(2 * qi + 1, upper, True)

    loop(functools.partial(acc_step, shift_bound))
    row_sum_min = jnp.min(acc_s[0][:, hw:])
    for h in range(1, DIFF_HEADS):
        row_sum_min = jnp.minimum(row_sum_min, jnp.min(acc_s[h][:, hw:]))

    @pl.when(jnp.logical_not(row_sum_min >= ATTN_MIN_ROW_SUM))
    def _():
        m_s[...] = jnp.full(m_s.shape, -1e30, F32)
        acc_s[...] = jnp.zeros_like(acc_s)
        loop(max_step)
        for h in range(DIFF_HEADS):
            m_s[h] = jnp.broadcast_to(jnp.max(m_s[h], axis=-1, keepdims=True), (2 * tq, hw))
        loop(functools.partial(acc_step, shift_row_max))

    lam = (jnp.exp(jnp.sum(lq1_ref[...] * lk1_ref[...], axis=-1, keepdims=True))
           - jnp.exp(jnp.sum(lq2_ref[...] * lk2_ref[...], axis=-1, keepdims=True)) + lambda_init)
    for h in range(DIFF_HEADS):
        for half in range(2):
            acc = acc_s[h, half * tq:(half + 1) * tq, :]
            acc_s[h, half * tq:(half + 1) * tq, :] = jnp.zeros_like(acc)
            o = acc[:, :hw] / acc[:, hw:]
            o = o[:tk] - lam * o[tk:]
            o_ref[half * tk:(half + 1) * tk, h * hw:(h + 1) * hw] = (
                _rms(o, g_ref[...]) * (1.0 - lambda_init)).astype(BF16)


def _diff_attn(q, k, v, lq1, lk1, lq2, lk2, g, lambda_init, B, S):
    tk = ATTN_TK
    tq = 2 * tk
    hw = 2 * DIFF_HD
    qspec = pl.BlockSpec((tq, DIFF_W), lambda b, i: (i, b))
    kspec = pl.BlockSpec((S, DIFF_W), lambda b, i: (0, b))
    small = pl.BlockSpec((1, DIFF_HD), lambda b, i: (0, 0))
    return pl.pallas_call(
        functools.partial(_diff_attn_kernel, lambda_init=lambda_init, tk=tk),
        grid=(B, S // tq),
        in_specs=[qspec, kspec, kspec, small, small, small, small,
                  pl.BlockSpec((1, hw), lambda b, i: (0, 0))],
        out_specs=qspec,
        out_shape=jax.ShapeDtypeStruct((S, B * DIFF_W), BF16),
        scratch_shapes=[pltpu.VMEM((DIFF_HEADS, 2 * tq, hw), BF16), pltpu.VMEM((DIFF_HEADS, 2 * tq, hw), F32),
                        pltpu.VMEM((DIFF_HEADS, 2 * tq, 2 * hw), F32), pltpu.VMEM((DIFF_HEADS, 2 * hw), F32)],
        compiler_params=_params("arbitrary", "arbitrary"),
        name="diff_attn",
    )(q, k, v, lq1, lk1, lq2, lk2, g)


def _merge_xattn_kernel(x_ref, ys_ref, ya_ref, yl_ref, gt_ref, wbs_ref, wba_ref, wbl_ref, wout_ref,
                        g_ref, wq_ref, k_ref, v_ref, wo_ref, o_ref):
    d = D_MODEL
    m = gt_ref[:, 0:d].astype(F32) * _dot(ys_ref[...], wbs_ref[...])
    m = m + gt_ref[:, d:2 * d].astype(F32) * _dot(ya_ref[...], wba_ref[...])
    m = m + gt_ref[:, 2 * d:3 * d].astype(F32) * _dot(yl_ref[...], wbl_ref[...])
    x = x_ref[...] + _dot(m.astype(BF16), wout_ref[...])

    hn = _rms(x, g_ref[...]).astype(BF16)
    q = (_dot(hn, wq_ref[...]) * (XATTN_HD ** -0.5)).astype(BF16)
    outs = []
    for h in range(XATTN_HEADS):
        cols = slice(h * XATTN_HD, (h + 1) * XATTN_HD)
        s = _dot_nt(q[:, cols], k_ref[:, cols])
        p = jnp.exp(s - jnp.max(s, axis=-1, keepdims=True))
        p = p / jnp.sum(p, axis=-1, keepdims=True)
        outs.append(_dot(p.astype(BF16), v_ref[:, cols]).astype(BF16))
    o_ref[...] = x + _dot(jnp.concatenate(outs, axis=1), wo_ref[...])


def _merge_xattn(x, batch_major, ys, ya, yl, gt, wbs, wba, wbl, wout, g, wq, kv, wo, mem_len, B, S):
    rows = PROJ_ROW_TILE

    def blk(width):
        return pl.BlockSpec((rows, width), lambda b, i: (i, b))

    return pl.pallas_call(
        _merge_xattn_kernel,
        grid=(B, S // rows),
        in_specs=[_x_spec(batch_major, rows, D_MODEL), blk(SSM_WIDTH), blk(DIFF_W), blk(LRU_WIDTH),
                  blk(3 * D_MODEL), _wspec(wbs), _wspec(wba), _wspec(wbl), _wspec(wout),
                  _whole(g.shape), _wspec(wq),
                  pl.BlockSpec((mem_len, D_MODEL), lambda b, i: (b, 0)),
                  pl.BlockSpec((mem_len, D_MODEL), lambda b, i: (b, 1)),
                  _wspec(wo)],
        out_specs=blk(D_MODEL),
        out_shape=jax.ShapeDtypeStruct((S, B * D_MODEL), F32),
        compiler_params=_params("parallel", "parallel"),
        name="merge_xattn",
    )(x, ys, ya, yl, gt, _wop(wbs), _wop(wba), _wop(wbl), _wop(wout), g, _wop(wq), kv, kv, _wop(wo))


def _memkv_kernel(m_ref, g_ref, w_ref, o_ref):
    hn = _rms(m_ref[...], g_ref[...]).astype(BF16)
    o_ref[...] = _dot(hn, w_ref[...]).astype(BF16)


def _memkv(mem_rows, g, wkv):
    n = mem_rows.shape[0]
    rows = ROW_TILE
    return pl.pallas_call(
        _memkv_kernel,
        grid=(n // rows,),
        in_specs=[pl.BlockSpec((rows, D_MODEL), lambda i: (i, 0)), _whole(g.shape), _wspec(wkv)],
        out_specs=pl.BlockSpec((rows, 2 * D_MODEL), lambda i: (i, 0)),
        out_shape=jax.ShapeDtypeStruct((n, 2 * D_MODEL), BF16),
        compiler_params=_params("parallel"),
        name="mem_kv",
    )(mem_rows, g, _wop(wkv))


def _ffn_kernel(x_ref, swap_ref, g_ref, wup_ref, cw_ref, cb_ref, wdn_ref, gf_ref, o_ref, prev, act_s,
                *, batch, steps, final):
    rows = batch * steps
    d = D_MODEL
    halo = (FFN_CONV - 1) * batch
    n = batch * batch

    @pl.when(pl.program_id(0) == 0)
    def _():
        prev[...] = jnp.zeros_like(prev)

    swap = swap_ref[...]
    g = g_ref[...]
    hn_b = [_rms(x_ref[:, b * d:(b + 1) * d], g).astype(BF16) for b in range(batch)]
    subs = []
    for k in range(steps // batch):
        hb = jnp.concatenate([h[k * batch:(k + 1) * batch] for h in hn_b], axis=0)
        subs.append(_dot(swap, hb).astype(BF16))
    hn = jnp.concatenate(subs, axis=0)

    def conv(up, part, cols):
        hist = prev[part]
        y = cb_ref[:, cols] + cw_ref[FFN_CONV - 1:FFN_CONV, cols] * up
        for t in range(FFN_CONV - 1):
            back = (FFN_CONV - 1 - t) * batch
            shifted = jnp.concatenate([hist[halo - back:], up[:rows - back]], axis=0)
            y = y + cw_ref[t:t + 1, cols] * shifted
        prev[part] = up[rows - halo:]
        return y

    for j in range(FFN_NCHUNK):
        vc = slice(j * FFN_CHUNK, (j + 1) * FFN_CHUNK)
        gc = slice(D_FF + j * FFN_CHUNK, D_FF + (j + 1) * FFN_CHUNK)
        val = conv(_dot(hn, wup_ref[:, vc]), 2 * j, vc)
        gate = conv(_dot(hn, wup_ref[:, gc]), 2 * j + 1, gc)
        act_s[:, vc] = (gate * _sigmoid(gate) * val).astype(BF16)
    acc = _dot(act_s[...], wdn_ref[...])

    for k in range(steps // batch):
        yb = jnp.swapaxes(acc[k * n:(k + 1) * n].reshape(batch, batch, d), 0, 1).reshape(n, d)
        ts = slice(k * batch, (k + 1) * batch)
        for b in range(batch):
            out = x_ref[ts, b * d:(b + 1) * d] + yb[b * batch:(b + 1) * batch]
            if final:
                o_ref[b, ts, :] = _rms(out, gf_ref[...])
            else:
                o_ref[ts, b * d:(b + 1) * d] = out


def _ffn(x, swap, g, wup, cw, cb, wdn, gf, B, S, final):
    rows = FFN_ROW_TILE
    steps = rows // B
    blk = pl.BlockSpec((steps, B * D_MODEL), lambda i: (i, 0))
    if final:
        out_spec = pl.BlockSpec((B, steps, D_MODEL), lambda i: (0, i, 0))
        out_shape = jax.ShapeDtypeStruct((B, S, D_MODEL), F32)
    else:
        out_spec, out_shape = blk, jax.ShapeDtypeStruct((S, B * D_MODEL), F32)
    return pl.pallas_call(
        functools.partial(_ffn_kernel, batch=B, steps=steps, final=final),
        grid=(S // steps,),
        in_specs=[blk, _whole(swap.shape), _whole(g.shape), _wspec(wup), _whole(cw.shape), _whole(cb.shape),
                  _wspec(wdn), _whole(gf.shape)],
        out_specs=out_spec,
        out_shape=out_shape,
        scratch_shapes=[pltpu.VMEM((2 * FFN_NCHUNK, (FFN_CONV - 1) * B, FFN_CHUNK), F32),
                        pltpu.VMEM((rows, D_FF), BF16)],
        compiler_params=_params("arbitrary"),
        name="conv_ffn",
    )(x, swap, g, _wop(wup), cw, cb, _wop(wdn), gf)


def _row(v):
    return v.reshape(1, -1).astype(F32)


def _block_diag(blocks):
    n, r, c = blocks.shape
    eye = jnp.eye(n, dtype=blocks.dtype)
    return (blocks[:, :, None, :] * eye[:, None, :, None]).reshape(n * r, n * c)


def _s5_b_weights(bbr, bbi):
    per = SSM_GROUPS // SSM_LANE_BLOCKS
    out = []
    for j in range(SSM_LANE_BLOCKS):
        r = bbr[j * LANES:(j + 1) * LANES].reshape(per, SSM_GROUP, SSM_STATE)
        i = bbi[j * LANES:(j + 1) * LANES].reshape(per, SSM_GROUP, SSM_STATE)
        out.append(jnp.concatenate([_block_diag(r), _block_diag(i)], axis=1))
    return jnp.stack(out).astype(BF16)


def _s5_c_weights(c):
    per = SSM_GROUPS // SSM_LANE_BLOCKS
    ct = jnp.swapaxes(c, 1, 2)
    return jnp.stack([_block_diag(ct[j * per:(j + 1) * per]) for j in range(SSM_LANE_BLOCKS)]).astype(BF16)


def kernel(x, mem, positions, norm_mix_g, w_in, ssm_lambda_re, ssm_lambda_im, ssm_log_step, ssm_b_re, ssm_b_im, ssm_c_re, ssm_c_im, ssm_d, ssm_w_glu, ssm_b_glu, diff_lq1, diff_lk1, diff_lq2, diff_lk2, diff_subln_g, lru_conv_w, lru_conv_b, lru_wa, lru_ba, lru_wx, lru_bx, lru_lambda, w_br_ssm, w_br_attn, w_br_lru, w_out, norm_xattn_g, norm_mem_g, xattn_wq, xattn_wkv, xattn_wo, norm_ffn_g, ffn_w_up, ffn_conv_w, ffn_conv_b, ffn_w_down, final_norm_g):
    B, S, _ = x.shape
    depth = norm_mix_g.shape[0]
    mem_len = mem.shape[1]
    assert S % ROW_TILE == 0 and ROW_TILE % B == 0 and B % 8 == 0 and (B * mem_len) % ROW_TILE == 0
    assert S % (2 * ATTN_TK) == 0 and FFN_ROW_TILE % (B * B) == 0 and (S * B) % FFN_ROW_TILE == 0
    assert REC_ROW_TILE % (B * B) == 0 and (S * B) % REC_ROW_TILE == 0 and S % PROJ_ROW_TILE == 0

    cos, sin = _rope_tables(positions.astype(F32)[..., None], B, S)

    rep = lambda a: jnp.repeat(a, SSM_GROUP, axis=1)
    b_t = lambda a: jnp.swapaxes(a, 2, 3).reshape(depth, SSM_WIDTH, SSM_STATE)
    abr, abi, bbr, bbi = _s5_prep(rep(ssm_lambda_re), rep(ssm_lambda_im), rep(ssm_log_step[..., None]),
                                  b_t(ssm_b_re), b_t(ssm_b_im))
    mem_rows = mem.reshape(B * mem_len, D_MODEL)
    swap = _swap_matrix(B)
    (w_in, ssm_w_glu, w_br_ssm, w_br_attn, w_br_lru, w_out, xattn_wq, xattn_wkv, xattn_wo, ffn_w_up,
     ffn_w_down) = (a.astype(BF16) for a in (w_in, ssm_w_glu, w_br_ssm, w_br_attn, w_br_lru, w_out, xattn_wq,
                                             xattn_wkv, xattn_wo, ffn_w_up, ffn_w_down))

    xs = x
    batch_major = True
    for l in range(depth):
        lambda_init = 0.8 - 0.6 * math.exp(-0.3 * l)
        u, q, k, v, xr, gr, gt = _inproj(xs, batch_major, _row(norm_mix_g[l]), _Layer(w_in, l), cos, sin, B, S)

        s5_params = (_s5_b_weights(bbr[l], bbi[l]),
                     abr[l, ::SSM_GROUP].reshape(1, SSM_NSTATE), abi[l, ::SSM_GROUP].reshape(1, SSM_NSTATE),
                     _s5_c_weights(ssm_c_re[l]), _s5_c_weights(ssm_c_im[l]), _row(ssm_d[l]),
                     _Layer(ssm_w_glu, l), _row(ssm_b_glu[l]))
        wax = jnp.concatenate([_block_diag(lru_wa[l]), _block_diag(lru_wx[l])], axis=1).astype(BF16)
        lru_params = (lru_conv_w[l].astype(F32), _row(lru_conv_b[l]), wax, _row(lru_ba[l]), _row(lru_bx[l]),
                      _row(lru_lambda[l]))
        y_ssm, y_lru = _recurrent(u, xr, gr, swap, s5_params, lru_params, B)

        y_att = _diff_attn(q, k, v, _row(diff_lq1[l]), _row(diff_lk1[l]), _row(diff_lq2[l]), _row(diff_lk2[l]),
                           _row(diff_subln_g[l]), lambda_init, B, S)

        kv = _memkv(mem_rows, _row(norm_mem_g[l]), _Layer(xattn_wkv, l))
        xs = _merge_xattn(xs, batch_major, y_ssm, y_att, y_lru, gt, _Layer(w_br_ssm, l), _Layer(w_br_attn, l),
                          _Layer(w_br_lru, l), _Layer(w_out, l), _row(norm_xattn_g[l]), _Layer(xattn_wq, l), kv,
                          _Layer(xattn_wo, l), mem_len, B, S)
        batch_major = False

        xs = _ffn(xs, swap, _row(norm_ffn_g[l]), _Layer(ffn_w_up, l), ffn_conv_w[l].astype(F32),
                  _row(ffn_conv_b[l]), _Layer(ffn_w_down, l), _row(final_norm_g), B, S, final=l == depth - 1)

    return xs
```

```python
import collections
import functools
import math

import jax
import jax.numpy as jnp
from jax import lax
from jax.experimental import pallas as pl
from jax.experimental.pallas import tpu as pltpu

F32 = jnp.float32
BF16 = jnp.bfloat16

EPS = 1e-6
D_MODEL = 1024
SSM_WIDTH = 384
SSM_GROUP = 16
SSM_GROUPS = 24
SSM_STATE = 64
SSM_NSTATE = SSM_GROUPS * SSM_STATE
SSM_LANE_BLOCKS = 3
DIFF_HEADS = 4
DIFF_HD = 64
DIFF_W = 512
ROPE_THETA = 10000.0
LRU_WIDTH = 512
LRU_CONV = 4
LRU_C = 8.0
XATTN_HEADS = 4
XATTN_HD = 256
D_FF = 2816
FFN_CONV = 3
FFN_CHUNK = 256
FFN_NCHUNK = D_FF // FFN_CHUNK
OFF_U, OFF_Q, OFF_K, OFF_V, OFF_XR, OFF_GR, OFF_G = 0, 384, 896, 1408, 1920, 2432, 2944
D_IN = 6016
GATE_CHUNK = 512

LANES = 128
SSM_BLOCK_STATES = SSM_NSTATE // SSM_LANE_BLOCKS

ROW_TILE = 512
PROJ_ROW_TILE = 1024
FFN_ROW_TILE = 1024
REC_ROW_TILE = 1024
ATTN_TK = 512
ATTN_NORM_ROWS = 128
ATTN_MIN_ROW_SUM = 1e-25
VMEM_LIMIT = 56 * 1024 * 1024


def _dot(a, b):
    return jnp.dot(a, b, preferred_element_type=F32)


def _dot_nt(a, b):
    return lax.dot_general(a, b, (((1,), (1,)), ((), ())), preferred_element_type=F32)


def _rms(x, g):
    ms = jnp.mean(x * x, axis=-1, keepdims=True)
    return x * lax.rsqrt(ms + EPS) * g


def _sigmoid(x):
    return 1.0 / (1.0 + jnp.exp(-x))


def _gelu(x):
    return 0.5 * x * (1.0 + jnp.tanh(0.7978845608028654 * (x + 0.044715 * (x * x * x))))


def _params(*sem):
    return pltpu.CompilerParams(dimension_semantics=sem, vmem_limit_bytes=VMEM_LIMIT)


def _whole(shape):
    zeros = (0,) * len(shape)
    return pl.BlockSpec(shape, lambda *_: zeros, pipeline_mode=pl.Buffered(1))


_Layer = collections.namedtuple("_Layer", "stack index")


def _wspec(w):
    if isinstance(w, _Layer):
        shape = w.stack.shape[1:]
        idx = (w.index,) + (0,) * len(shape)
        return pl.BlockSpec((None,) + shape, lambda *_: idx, pipeline_mode=pl.Buffered(1))
    return _whole(w.shape)


def _wop(w):
    return w.stack if isinstance(w, _Layer) else w


def _x_spec(batch_major, rows, width):
    if batch_major:
        return pl.BlockSpec((None, rows, width), lambda b, i: (b, i, 0))
    return pl.BlockSpec((rows, width), lambda b, i: (i, b))


def _rope_kernel(pos_ref, cos_ref, sin_ref):
    half = DIFF_HD // 2
    lane = lax.broadcasted_iota(jnp.int32, (1, LANES), 1)
    j = (lane & (half - 1)).astype(F32)
    inv = jnp.exp((-math.log(ROPE_THETA) * (2.0 * j)) / DIFF_HD)
    ang = pos_ref[...] * inv
    first_half = (lane & (DIFF_HD - 1)) < half
    cos_ref[...] = jnp.cos(ang)
    s = jnp.sin(ang)
    sin_ref[...] = jnp.where(first_half, -s, s)


def _rope_tables(pos_f, B, S):
    rows = ROW_TILE
    spec = pl.BlockSpec((None, rows, LANES), lambda b, i: (b, i, 0))
    return pl.pallas_call(
        _rope_kernel,
        grid=(B, S // rows),
        in_specs=[pl.BlockSpec((None, rows, 1), lambda b, i: (b, i, 0))],
        out_specs=[spec, spec],
        out_shape=[jax.ShapeDtypeStruct((B, S, LANES), F32)] * 2,
        compiler_params=_params("parallel", "parallel"),
        name="rope_tables",
    )(pos_f)


def _s5_prep_kernel(lr_ref, li_ref, ls_ref, br_ref, bi_ref, abr_ref, abi_ref, bbr_ref, bbi_ref):
    lr = lr_ref[...]
    li = li_ref[...]
    dt = jnp.exp(ls_ref[...])
    mag = jnp.exp(lr * dt)
    ab_r = mag * jnp.cos(li * dt)
    ab_i = mag * jnp.sin(li * dt)
    den = lr * lr + li * li
    nr = ab_r - 1.0
    f_r = (nr * lr + ab_i * li) / den
    f_i = (ab_i * lr - nr * li) / den
    br = br_ref[...]
    bi = bi_ref[...]
    abr_ref[...] = ab_r
    abi_ref[...] = ab_i
    bbr_ref[...] = f_r * br - f_i * bi
    bbi_ref[...] = f_r * bi + f_i * br


def _s5_prep(lr_rep, li_rep, ls_rep, br_t, bi_t):
    depth = lr_rep.shape[0]
    spec = pl.BlockSpec((None, SSM_WIDTH, SSM_STATE), lambda l: (l, 0, 0))
    return pl.pallas_call(
        _s5_prep_kernel,
        grid=(depth,),
        in_specs=[spec, spec, pl.BlockSpec((None, SSM_WIDTH, 1), lambda l: (l, 0, 0)), spec, spec],
        out_specs=[spec] * 4,
        out_shape=[jax.ShapeDtypeStruct((depth, SSM_WIDTH, SSM_STATE), F32)] * 4,
        compiler_params=_params("parallel"),
        name="s5_prep",
    )(lr_rep, li_rep, ls_rep, br_t, bi_t)


def _rope(x, cos, sin_signed, first_half):
    half = DIFF_HD // 2
    back = pltpu.roll(x, half, 1)
    fwd = pltpu.roll(x, DIFF_W - half, 1)
    return x * cos + jnp.where(first_half, fwd, back) * sin_signed


def _inproj_kernel(x_ref, g_ref, w_ref, cos_ref, sin_ref,
                   u_ref, q_ref, k_ref, v_ref, xr_ref, gr_ref, gt_ref):
    hn = _rms(x_ref[...], g_ref[...]).astype(BF16)

    def seg(off, width):
        return _dot(hn, w_ref[:, off:off + width])

    cos = jnp.concatenate([cos_ref[...]] * (DIFF_W // LANES), axis=1)
    sin = jnp.concatenate([sin_ref[...]] * (DIFF_W // LANES), axis=1)
    lane = lax.broadcasted_iota(jnp.int32, (1, DIFF_W), 1)
    first_half = (lane & (DIFF_HD - 1)) < DIFF_HD // 2

    u_ref[...] = seg(OFF_U, SSM_WIDTH).astype(BF16)
    q_ref[...] = (_rope(seg(OFF_Q, DIFF_W), cos, sin, first_half) * (DIFF_HD ** -0.5)).astype(BF16)
    k_ref[...] = _rope(seg(OFF_K, DIFF_W), cos, sin, first_half).astype(BF16)
    v_ref[...] = seg(OFF_V, DIFF_W).astype(BF16)
    xr_ref[...] = seg(OFF_XR, LRU_WIDTH).astype(BF16)
    gr_ref[...] = seg(OFF_GR, LRU_WIDTH).astype(BF16)
    gw = GATE_CHUNK
    for c in range(3 * D_MODEL // gw):
        gt_ref[:, c * gw:(c + 1) * gw] = _sigmoid(seg(OFF_G + c * gw, gw)).astype(BF16)


def _inproj(x, batch_major, g, w_in, cos, sin, B, S):
    rows = PROJ_ROW_TILE

    def out(width):
        return pl.BlockSpec((rows, width), lambda b, i: (i, b))

    widths = (SSM_WIDTH, DIFF_W, DIFF_W, DIFF_W, LRU_WIDTH, LRU_WIDTH, 3 * D_MODEL)
    tab = pl.BlockSpec((None, rows, LANES), lambda b, i: (b, i, 0))
    return pl.pallas_call(
        _inproj_kernel,
        grid=(B, S // rows),
        in_specs=[_x_spec(batch_major, rows, D_MODEL), _whole((1, D_MODEL)), _wspec(w_in), tab, tab],
        out_specs=[out(w) for w in widths],
        out_shape=[jax.ShapeDtypeStruct((S, B * w), BF16) for w in widths],
        compiler_params=_params("parallel", "parallel"),
        name="inproj",
    )(x, g, _wop(w_in), cos, sin)


def _swap_matrix(batch):
    idx = jnp.arange(batch * batch)
    return (idx[:, None] == (idx[None, :] % batch) * batch + idx[None, :] // batch).astype(BF16)


def _load_time_major(ref, width, batch, steps, swap):
    subs = []
    for k in range(steps // batch):
        rows = slice(k * batch, (k + 1) * batch)
        xb = jnp.concatenate([ref[rows, b * width:(b + 1) * width] for b in range(batch)], axis=0)
        subs.append(_dot(swap, xb).astype(BF16))
    return jnp.concatenate(subs, axis=0)


def _store_batch_major(ref, y, width, batch, steps, swap):
    n = batch * batch
    for k in range(steps // batch):
        yb = _dot(swap, y[k * n:(k + 1) * n]).astype(BF16)
        for b in range(batch):
            ref[k * batch:(k + 1) * batch, b * width:(b + 1) * width] = yb[b * batch:(b + 1) * batch]


def _s5_body(u_ref, swap, wb_ref, abr_ref, abi_ref, cre_ref, cim_ref, d_ref, wglu_ref, bglu_ref,
             o_ref, sre, sim, st_re, st_im, batch, steps):
    u = _load_time_major(u_ref, SSM_WIDTH, batch, steps, swap)
    nb = SSM_BLOCK_STATES
    for j in range(SSM_LANE_BLOCKS):
        bu = _dot(u[:, j * LANES:(j + 1) * LANES], wb_ref[j])
        sre[:, j * nb:(j + 1) * nb] = bu[:, :nb]
        sim[:, j * nb:(j + 1) * nb] = bu[:, nb:]

    for j in range(SSM_LANE_BLOCKS):
        cols = slice(j * nb, (j + 1) * nb)
        ar = jnp.broadcast_to(abr_ref[:, cols], (batch, nb))
        ai = jnp.broadcast_to(abi_ref[:, cols], (batch, nb))
        sr = st_re[:, cols]
        si = st_im[:, cols]
        for t in range(steps):
            rows = slice(t * batch, (t + 1) * batch)
            nr = ar * sr - ai * si + sre[rows, cols]
            ni = ar * si + ai * sr + sim[rows, cols]
            sre[rows, cols] = nr
            sim[rows, cols] = ni
            sr, si = nr, ni
        st_re[:, cols] = sr
        st_im[:, cols] = si

    ys = []
    for j in range(SSM_LANE_BLOCKS):
        cols = slice(j * nb, (j + 1) * nb)
        ys.append(_dot(sre[:, cols].astype(BF16), cre_ref[j]) - _dot(sim[:, cols].astype(BF16), cim_ref[j]))
    y = jnp.concatenate(ys, axis=1) + d_ref[...] * u.astype(F32)
    y = _gelu(y)
    z = _dot(y.astype(BF16), wglu_ref[...]) + bglu_ref[...]
    out = (z[:, :SSM_WIDTH] * _sigmoid(z[:, SSM_WIDTH:])).astype(BF16)
    _store_batch_major(o_ref, out, SSM_WIDTH, batch, steps, swap)


def _lru_body(xr_ref, gr_ref, swap, cw_ref, cb_ref, wax_ref, ba_ref, bx_ref, lam_ref,
              o_ref, ext, a_s, b_s, h_st, batch, steps):
    rows = batch * steps
    halo = (LRU_CONV - 1) * batch
    ext[halo:halo + rows, :] = _load_time_major(xr_ref, LRU_WIDTH, batch, steps, swap).astype(F32)
    xc = cb_ref[...] + cw_ref[LRU_CONV - 1:LRU_CONV, :] * ext[halo:halo + rows, :]
    for j in range(LRU_CONV - 1):
        xc = xc + cw_ref[j:j + 1, :] * ext[j * batch:j * batch + rows, :]
    ext[0:halo, :] = ext[rows:rows + halo, :]

    z = _dot(xc.astype(BF16), wax_ref[...])
    r = _sigmoid(z[:, :LRU_WIDTH] + ba_ref[...])
    ig = _sigmoid(z[:, LRU_WIDTH:] + bx_ref[...])
    softplus_neg_lam = jnp.log1p(jnp.exp(-lam_ref[...]))
    a = jnp.exp((-LRU_C) * r * softplus_neg_lam)
    a_s[...] = a
    y = 1.0 - a * a
    b_s[...] = jnp.where(y > 0.0, y * lax.rsqrt(y), 0.0) * (ig * xc)

    h = h_st[...]
    for t in range(steps):
        rs = slice(t * batch, (t + 1) * batch)
        h = a_s[rs, :] * h + b_s[rs, :]
        b_s[rs, :] = h
    h_st[...] = h
    gr = _load_time_major(gr_ref, LRU_WIDTH, batch, steps, swap).astype(F32)
    _store_batch_major(o_ref, (b_s[...] * _gelu(gr)).astype(BF16), LRU_WIDTH, batch, steps, swap)


def _recurrent_kernel(u_ref, xr_ref, gr_ref, swap_ref,
                      wb_ref, abr_ref, abi_ref, cre_ref, cim_ref, d_ref, wglu_ref, bglu_ref,
                      cw_ref, cb_ref, wax_ref, ba_ref, bx_ref, lam_ref,
                      ys_ref, yl_ref, sre, sim, st_re, st_im, ext, a_s, b_s, h_st, *, batch, steps):
    @pl.when(pl.program_id(0) == 0)
    def _():
        st_re[...] = jnp.zeros_like(st_re)
        st_im[...] = jnp.zeros_like(st_im)
        ext[0:(LRU_CONV - 1) * batch, :] = jnp.zeros(((LRU_CONV - 1) * batch, LRU_WIDTH), F32)
        h_st[...] = jnp.zeros_like(h_st)

    swap = swap_ref[...]
    _s5_body(u_ref, swap, wb_ref, abr_ref, abi_ref, cre_ref, cim_ref, d_ref, wglu_ref, bglu_ref,
             ys_ref, sre, sim, st_re, st_im, batch, steps)
    _lru_body(xr_ref, gr_ref, swap, cw_ref, cb_ref, wax_ref, ba_ref, bx_ref, lam_ref,
              yl_ref, ext, a_s, b_s, h_st, batch, steps)


def _recurrent(u, xr, gr, swap, s5_params, lru_params, B):
    S = u.shape[0]
    rows = REC_ROW_TILE
    steps = rows // B
    halo = (LRU_CONV - 1) * B

    def blk(width):
        return pl.BlockSpec((steps, B * width), lambda i: (i, 0))

    params = tuple(s5_params) + tuple(lru_params)
    return pl.pallas_call(
        functools.partial(_recurrent_kernel, batch=B, steps=steps),
        grid=(S // steps,),
        in_specs=[blk(SSM_WIDTH), blk(LRU_WIDTH), blk(LRU_WIDTH), _whole(swap.shape)] + [_wspec(p) for p in params],
        out_specs=[blk(SSM_WIDTH), blk(LRU_WIDTH)],
        out_shape=[jax.ShapeDtypeStruct((S, B * SSM_WIDTH), BF16), jax.ShapeDtypeStruct((S, B * LRU_WIDTH), BF16)],
        scratch_shapes=[pltpu.VMEM((rows, SSM_NSTATE), F32), pltpu.VMEM((rows, SSM_NSTATE), F32),
                        pltpu.VMEM((B, SSM_NSTATE), F32), pltpu.VMEM((B, SSM_NSTATE), F32),
                        pltpu.VMEM((rows + halo, LRU_WIDTH), F32), pltpu.VMEM((rows, LRU_WIDTH), F32),
                        pltpu.VMEM((rows, LRU_WIDTH), F32), pltpu.VMEM((B, LRU_WIDTH), F32)],
        compiler_params=_params("arbitrary"),
        name="recurrent_branches",
    )(u, xr, gr, swap, *[_wop(p) for p in params])


def _diff_attn_kernel(q_ref, k_ref, v_ref, lq1_ref, lk1_ref, lq2_ref, lk2_ref, g_ref, o_ref,
                      qq, m_s, acc_s, kn_s, *, lambda_init, tk):
    qi = pl.program_id(1)
    hw = 2 * DIFF_HD
    tq = 2 * tk
    ts = tk // 2
    nstrip = tq // ts
    every = slice(0, 2 * tq)
    lane = lax.broadcasted_iota(jnp.int32, (1, hw), 1)

    @pl.when((pl.program_id(0) == 0) & (qi == 0))
    def _():
        acc_s[...] = jnp.zeros_like(acc_s)

    for h in range(DIFF_HEADS):
        for st in range(nstrip):
            q = q_ref[st * ts:(st + 1) * ts, h * hw:(h + 1) * hw]
            zero = jnp.zeros_like(q)
            qq[h, 2 * st * ts:(2 * st + 1) * ts, :] = jnp.where(lane < DIFF_HD, q, zero)
            qq[h, (2 * st + 1) * ts:(2 * st + 2) * ts, :] = jnp.where(lane >= DIFF_HD, q, zero)

    @pl.when(qi == 0)
    def _():
        kn_s[...] = jnp.zeros_like(kn_s)

    new_keys = pl.multiple_of(qi * tq, tq)
    d_row = lax.broadcasted_iota(jnp.int32, (hw, 2 * hw), 0)
    d_col = lax.broadcasted_iota(jnp.int32, (hw, 2 * hw), 1)
    comp_sum = ((d_row < DIFF_HD) == (d_col < hw)).astype(BF16)

    def max_sq_norm(x):
        sq = x * x
        fold = sq[0:ATTN_NORM_ROWS]
        for r in range(ATTN_NORM_ROWS, tq, ATTN_NORM_ROWS):
            fold = jnp.maximum(fold, sq[r:r + ATTN_NORM_ROWS])
        return jnp.max(_dot(fold, comp_sum), axis=0, keepdims=True)

    bound = []
    for h in range(DIFF_HEADS):
        kn = jnp.maximum(kn_s[h:h + 1, :], max_sq_norm(k_ref[pl.ds(new_keys, tq), h * hw:(h + 1) * hw]))
        kn_s[h:h + 1, :] = kn
        qn = max_sq_norm(q_ref[:, h * hw:(h + 1) * hw])
        n2 = qn * kn
        b = n2 * lax.rsqrt(jnp.maximum(n2, 1e-30))
        bound.append((b[:, 0:1], b[:, hw:hw + 1]))

    def scores(start, nkeys, h, rows, diagonal):
        s = _dot_nt(qq[h, rows, :], k_ref[pl.ds(start, nkeys), h * hw:(h + 1) * hw])
        if diagonal:
            row = lax.broadcasted_iota(jnp.int32, s.shape, 0)
            col = lax.broadcasted_iota(jnp.int32, s.shape, 1)
            s = jnp.where((col <= (row & (ts - 1))) | (row >= 2 * ts), s, -1e30)
        return s

    def chunks(nkeys):
        return [slice(c * hw, (c + 1) * hw) for c in range(nkeys // hw)]

    def max_step(start, nkeys, rows, diagonal):
        for h in range(DIFF_HEADS):
            s = scores(start, nkeys, h, rows, diagonal)
            m = m_s[h, rows, :]
            for c in chunks(nkeys):
                m = jnp.maximum(m, s[:, c])
            m_s[h, rows, :] = m

    def shift_bound(h, rows, s):
        return jnp.concatenate([s[i * ts:(i + 1) * ts] - bound[h][i % 2] for i in range(s.shape[0] // ts)], axis=0)

    def shift_row_max(h, rows, s):
        m = m_s[h, rows, :]
        return jnp.concatenate([s[:, c] - m for c in chunks(s.shape[1])], axis=1)

    def acc_step(shift, start, nkeys, rows, diagonal):
        ones = jnp.ones((nkeys, hw), BF16)
        for h in range(DIFF_HEADS):
            p = jnp.exp(shift(h, rows, scores(start, nkeys, h, rows, diagonal))).astype(BF16)
            v_ext = jnp.concatenate([v_ref[pl.ds(start, nkeys), h * hw:(h + 1) * hw], ones], axis=1)
            acc_s[h, rows, :] = acc_s[h, rows, :] + _dot(p, v_ext)

    def loop(step):
        def body(j, c):
            step(pl.multiple_of(j * tk, tk), tk, every, False)
            return c
        lax.fori_loop(0, 2 * qi, body, 0)
        for st in range(nstrip):
            step(pl.multiple_of(qi * tq + st * ts, ts), ts, slice(2 * st * ts, 2 * tq), True)

    loop(functools.partial(acc_step, shift_bound))
    row_sum_min = jnp.min(acc_s[0][:, hw:])
    for h in range(1, DIFF_HEADS):
        row_sum_min = jnp.minimum(row_sum_min, jnp.min(acc_s[h][:, hw:]))

    @pl.when(jnp.logical_not(row_sum_min >= ATTN_MIN_ROW_SUM))
    def _():
        m_s[...] = jnp.full(m_s.shape, -1e30, F32)
        acc_s[...] = jnp.zeros_like(acc_s)
        loop(max_step)
        for h in range(DIFF_HEADS):
            m_s[h] = jnp.broadcast_to(jnp.max(m_s[h], axis=-1, keepdims=True), (2 * tq, hw))
        loop(functools.partial(acc_step, shift_row_max))

    lam = (jnp.exp(jnp.sum(lq1_ref[...] * lk1_ref[...], axis=-1, keepdims=True))
           - jnp.exp(jnp.sum(lq2_ref[...] * lk2_ref[...], axis=-1, keepdims=True)) + lambda_init)
    for h in range(DIFF_HEADS):
        for st in range(nstrip):
            acc = acc_s[h, 2 * st * ts:(2 * st + 2) * ts, :]
            acc_s[h, 2 * st * ts:(2 * st + 2) * ts, :] = jnp.zeros_like(acc)
            o = acc[:, :hw] / acc[:, hw:]
            o = o[:ts] - lam * o[ts:]
            o_ref[st * ts:(st + 1) * ts, h * hw:(h + 1) * hw] = (
                _rms(o, g_ref[...]) * (1.0 - lambda_init)).astype(BF16)


def _diff_attn(q, k, v, lq1, lk1, lq2, lk2, g, lambda_init, B, S):
    tk = ATTN_TK
    tq = 2 * tk
    hw = 2 * DIFF_HD
    qspec = pl.BlockSpec((tq, DIFF_W), lambda b, i: (i, b))
    kspec = pl.BlockSpec((S, DIFF_W), lambda b, i: (0, b))
    small = pl.BlockSpec((1, DIFF_HD), lambda b, i: (0, 0))
    return pl.pallas_call(
        functools.partial(_diff_attn_kernel, lambda_init=lambda_init, tk=tk),
        grid=(B, S // tq),
        in_specs=[qspec, kspec, kspec, small, small, small, small,
                  pl.BlockSpec((1, hw), lambda b, i: (0, 0))],
        out_specs=qspec,
        out_shape=jax.ShapeDtypeStruct((S, B * DIFF_W), BF16),
        scratch_shapes=[pltpu.VMEM((DIFF_HEADS, 2 * tq, hw), BF16), pltpu.VMEM((DIFF_HEADS, 2 * tq, hw), F32),
                        pltpu.VMEM((DIFF_HEADS, 2 * tq, 2 * hw), F32), pltpu.VMEM((DIFF_HEADS, 2 * hw), F32)],
        compiler_params=_params("arbitrary", "arbitrary"),
        name="diff_attn",
    )(q, k, v, lq1, lk1, lq2, lk2, g)


def _merge_xattn_kernel(x_ref, ys_ref, ya_ref, yl_ref, gt_ref, wbs_ref, wba_ref, wbl_ref, wout_ref,
                        g_ref, wq_ref, k_ref, v_ref, wo_ref, o_ref):
    d = D_MODEL
    m = gt_ref[:, 0:d].astype(F32) * _dot(ys_ref[...], wbs_ref[...])
    m = m + gt_ref[:, d:2 * d].astype(F32) * _dot(ya_ref[...], wba_ref[...])
    m = m + gt_ref[:, 2 * d:3 * d].astype(F32) * _dot(yl_ref[...], wbl_ref[...])
    x = x_ref[...] + _dot(m.astype(BF16), wout_ref[...])

    hn = _rms(x, g_ref[...]).astype(BF16)
    q = (_dot(hn, wq_ref[...]) * (XATTN_HD ** -0.5)).astype(BF16)
    outs = []
    for h in range(XATTN_HEADS):
        cols = slice(h * XATTN_HD, (h + 1) * XATTN_HD)
        s = _dot_nt(q[:, cols], k_ref[:, cols])
        p = jnp.exp(s - jnp.max(s, axis=-1, keepdims=True))
        p = p / jnp.sum(p, axis=-1, keepdims=True)
        outs.append(_dot(p.astype(BF16), v_ref[:, cols]).astype(BF16))
    o_ref[...] = x + _dot(jnp.concatenate(outs, axis=1), wo_ref[...])


def _merge_xattn(x, batch_major, ys, ya, yl, gt, wbs, wba, wbl, wout, g, wq, kv, wo, mem_len, B, S):
    rows = PROJ_ROW_TILE

    def blk(width):
        return pl.BlockSpec((rows, width), lambda b, i: (i, b))

    return pl.pallas_call(
        _merge_xattn_kernel,
        grid=(B, S // rows),
        in_specs=[_x_spec(batch_major, rows, D_MODEL), blk(SSM_WIDTH), blk(DIFF_W), blk(LRU_WIDTH),
                  blk(3 * D_MODEL), _wspec(wbs), _wspec(wba), _wspec(wbl), _wspec(wout),
                  _whole(g.shape), _wspec(wq),
                  pl.BlockSpec((mem_len, D_MODEL), lambda b, i: (b, 0)),
                  pl.BlockSpec((mem_len, D_MODEL), lambda b, i: (b, 1)),
                  _wspec(wo)],
        out_specs=blk(D_MODEL),
        out_shape=jax.ShapeDtypeStruct((S, B * D_MODEL), F32),
        compiler_params=_params("parallel", "parallel"),
        name="merge_xattn",
    )(x, ys, ya, yl, gt, _wop(wbs), _wop(wba), _wop(wbl), _wop(wout), g, _wop(wq), kv, kv, _wop(wo))


def _memkv_kernel(m_ref, g_ref, w_ref, o_ref):
    hn = _rms(m_ref[...], g_ref[...]).astype(BF16)
    o_ref[...] = _dot(hn, w_ref[...]).astype(BF16)


def _memkv(mem_rows, g, wkv):
    n = mem_rows.shape[0]
    rows = ROW_TILE
    return pl.pallas_call(
        _memkv_kernel,
        grid=(n // rows,),
        in_specs=[pl.BlockSpec((rows, D_MODEL), lambda i: (i, 0)), _whole(g.shape), _wspec(wkv)],
        out_specs=pl.BlockSpec((rows, 2 * D_MODEL), lambda i: (i, 0)),
        out_shape=jax.ShapeDtypeStruct((n, 2 * D_MODEL), BF16),
        compiler_params=_params("parallel"),
        name="mem_kv",
    )(mem_rows, g, _wop(wkv))


def _ffn_kernel(x_ref, swap_ref, g_ref, wup_ref, cw_ref, cb_ref, wdn_ref, gf_ref, o_ref, prev, act_s,
                *, batch, steps, final):
    rows = batch * steps
    d = D_MODEL
    halo = (FFN_CONV - 1) * batch
    n = batch * batch

    @pl.when(pl.program_id(0) == 0)
    def _():
        prev[...] = jnp.zeros_like(prev)

    swap = swap_ref[...]
    g = g_ref[...]
    hn_b = [_rms(x_ref[:, b * d:(b + 1) * d], g).astype(BF16) for b in range(batch)]
    subs = []
    for k in range(steps // batch):
        hb = jnp.concatenate([h[k * batch:(k + 1) * batch] for h in hn_b], axis=0)
        subs.append(_dot(swap, hb).astype(BF16))
    hn = jnp.concatenate(subs, axis=0)

    def conv(up, part, cols):
        hist = prev[part]
        y = cb_ref[:, cols] + cw_ref[FFN_CONV - 1:FFN_CONV, cols] * up
        for t in range(FFN_CONV - 1):
            back = (FFN_CONV - 1 - t) * batch
            shifted = jnp.concatenate([hist[halo - back:], up[:rows - back]], axis=0)
            y = y + cw_ref[t:t + 1, cols] * shifted
        prev[part] = up[rows - halo:]
        return y

    for j in range(FFN_NCHUNK):
        vc = slice(j * FFN_CHUNK, (j + 1) * FFN_CHUNK)
        gc = slice(D_FF + j * FFN_CHUNK, D_FF + (j + 1) * FFN_CHUNK)
        val = conv(_dot(hn, wup_ref[:, vc]), 2 * j, vc)
        gate = conv(_dot(hn, wup_ref[:, gc]), 2 * j + 1, gc)
        act_s[:, vc] = (gate * _sigmoid(gate) * val).astype(BF16)
    acc = _dot(act_s[...], wdn_ref[...])

    for k in range(steps // batch):
        yb = jnp.swapaxes(acc[k * n:(k + 1) * n].reshape(batch, batch, d), 0, 1).reshape(n, d)
        ts = slice(k * batch, (k + 1) * batch)
        for b in range(batch):
            out = x_ref[ts, b * d:(b + 1) * d] + yb[b * batch:(b + 1) * batch]
            if final:
                o_ref[b, ts, :] = _rms(out, gf_ref[...])
            else:
                o_ref[ts, b * d:(b + 1) * d] = out


def _ffn(x, swap, g, wup, cw, cb, wdn, gf, B, S, final):
    rows = FFN_ROW_TILE
    steps = rows // B
    blk = pl.BlockSpec((steps, B * D_MODEL), lambda i: (i, 0))
    if final:
        out_spec = pl.BlockSpec((B, steps, D_MODEL), lambda i: (0, i, 0))
        out_shape = jax.ShapeDtypeStruct((B, S, D_MODEL), F32)
    else:
        out_spec, out_shape = blk, jax.ShapeDtypeStruct((S, B * D_MODEL), F32)
    return pl.pallas_call(
        functools.partial(_ffn_kernel, batch=B, steps=steps, final=final),
        grid=(S // steps,),
        in_specs=[blk, _whole(swap.shape), _whole(g.shape), _wspec(wup), _whole(cw.shape), _whole(cb.shape),
                  _wspec(wdn), _whole(gf.shape)],
        out_specs=out_spec,
        out_shape=out_shape,
        scratch_shapes=[pltpu.VMEM((2 * FFN_NCHUNK, (FFN_CONV - 1) * B, FFN_CHUNK), F32),
                        pltpu.VMEM((rows, D_FF), BF16)],
        compiler_params=_params("arbitrary"),
        name="conv_ffn",
    )(x, swap, g, _wop(wup), cw, cb, _wop(wdn), gf)


def _row(v):
    return v.reshape(1, -1).astype(F32)


def _block_diag(blocks):
    n, r, c = blocks.shape
    eye = jnp.eye(n, dtype=blocks.dtype)
    return (blocks[:, :, None, :] * eye[:, None, :, None]).reshape(n * r, n * c)


def _s5_b_weights(bbr, bbi):
    per = SSM_GROUPS // SSM_LANE_BLOCKS
    out = []
    for j in range(SSM_LANE_BLOCKS):
        r = bbr[j * LANES:(j + 1) * LANES].reshape(per, SSM_GROUP, SSM_STATE)
        i = bbi[j * LANES:(j + 1) * LANES].reshape(per, SSM_GROUP, SSM_STATE)
        out.append(jnp.concatenate([_block_diag(r), _block_diag(i)], axis=1))
    return jnp.stack(out).astype(BF16)


def _s5_c_weights(c):
    per = SSM_GROUPS // SSM_LANE_BLOCKS
    ct = jnp.swapaxes(c, 1, 2)
    return jnp.stack([_block_diag(ct[j * per:(j + 1) * per]) for j in range(SSM_LANE_BLOCKS)]).astype(BF16)


def kernel(x, mem, positions, norm_mix_g, w_in, ssm_lambda_re, ssm_lambda_im, ssm_log_step, ssm_b_re, ssm_b_im, ssm_c_re, ssm_c_im, ssm_d, ssm_w_glu, ssm_b_glu, diff_lq1, diff_lk1, diff_lq2, diff_lk2, diff_subln_g, lru_conv_w, lru_conv_b, lru_wa, lru_ba, lru_wx, lru_bx, lru_lambda, w_br_ssm, w_br_attn, w_br_lru, w_out, norm_xattn_g, norm_mem_g, xattn_wq, xattn_wkv, xattn_wo, norm_ffn_g, ffn_w_up, ffn_conv_w, ffn_conv_b, ffn_w_down, final_norm_g):
    B, S, _ = x.shape
    depth = norm_mix_g.shape[0]
    mem_len = mem.shape[1]
    assert S % ROW_TILE == 0 and ROW_TILE % B == 0 and B % 8 == 0 and (B * mem_len) % ROW_TILE == 0
    assert S % (2 * ATTN_TK) == 0 and FFN_ROW_TILE % (B * B) == 0 and (S * B) % FFN_ROW_TILE == 0
    assert REC_ROW_TILE % (B * B) == 0 and (S * B) % REC_ROW_TILE == 0 and S % PROJ_ROW_TILE == 0

    cos, sin = _rope_tables(positions.astype(F32)[..., None], B, S)

    rep = lambda a: jnp.repeat(a, SSM_GROUP, axis=1)
    b_t = lambda a: jnp.swapaxes(a, 2, 3).reshape(depth, SSM_WIDTH, SSM_STATE)
    abr, abi, bbr, bbi = _s5_prep(rep(ssm_lambda_re), rep(ssm_lambda_im), rep(ssm_log_step[..., None]),
                                  b_t(ssm_b_re), b_t(ssm_b_im))
    mem_rows = mem.reshape(B * mem_len, D_MODEL)
    swap = _swap_matrix(B)
    (w_in, ssm_w_glu, w_br_ssm, w_br_attn, w_br_lru, w_out, xattn_wq, xattn_wkv, xattn_wo, ffn_w_up,
     ffn_w_down) = (a.astype(BF16) for a in (w_in, ssm_w_glu, w_br_ssm, w_br_attn, w_br_lru, w_out, xattn_wq,
                                             xattn_wkv, xattn_wo, ffn_w_up, ffn_w_down))

    xs = x
    batch_major = True
    for l in range(depth):
        lambda_init = 0.8 - 0.6 * math.exp(-0.3 * l)
        u, q, k, v, xr, gr, gt = _inproj(xs, batch_major, _row(norm_mix_g[l]), _Layer(w_in, l), cos, sin, B, S)

        s5_params = (_s5_b_weights(bbr[l], bbi[l]),
                     abr[l, ::SSM_GROUP].reshape(1, SSM_NSTATE), abi[l, ::SSM_GROUP].reshape(1, SSM_NSTATE),
                     _s5_c_weights(ssm_c_re[l]), _s5_c_weights(ssm_c_im[l]), _row(ssm_d[l]),
                     _Layer(ssm_w_glu, l), _row(ssm_b_glu[l]))
        wax = jnp.concatenate([_block_diag(lru_wa[l]), _block_diag(lru_wx[l])], axis=1).astype(BF16)
        lru_params = (lru_conv_w[l].astype(F32), _row(lru_conv_b[l]), wax, _row(lru_ba[l]), _row(lru_bx[l]),
                      _row(lru_lambda[l]))
        y_ssm, y_lru = _recurrent(u, xr, gr, swap, s5_params, lru_params, B)

        y_att = _diff_attn(q, k, v, _row(diff_lq1[l]), _row(diff_lk1[l]), _row(diff_lq2[l]), _row(diff_lk2[l]),
                           _row(diff_subln_g[l]), lambda_init, B, S)

        kv = _memkv(mem_rows, _row(norm_mem_g[l]), _Layer(xattn_wkv, l))
        xs = _merge_xattn(xs, batch_major, y_ssm, y_att, y_lru, gt, _Layer(w_br_ssm, l), _Layer(w_br_attn, l),
                          _Layer(w_br_lru, l), _Layer(w_out, l), _row(norm_xattn_g[l]), _Layer(xattn_wq, l), kv,
                          _Layer(xattn_wo, l), mem_len, B, S)
        batch_major = False

        xs = _ffn(xs, swap, _row(norm_ffn_g[l]), _Layer(ffn_w_up, l), ffn_conv_w[l].astype(F32),
                  _row(ffn_conv_b[l]), _Layer(ffn_w_down, l), _row(final_norm_g), B, S, final=l == depth - 1)

    return xs
```

```python
import collections
import functools
import math

import jax
import jax.numpy as jnp
from jax import lax
from jax.experimental import pallas as pl
from jax.experimental.pallas import tpu as pltpu

F32 = jnp.float32
BF16 = jnp.bfloat16

EPS = 1e-6
D_MODEL = 1024
SSM_WIDTH = 384
SSM_GROUP = 16
SSM_GROUPS = 24
SSM_STATE = 64
SSM_NSTATE = SSM_GROUPS * SSM_STATE
SSM_LANE_BLOCKS = 3
DIFF_HEADS = 4
DIFF_HD = 64
DIFF_W = 512
ROPE_THETA = 10000.0
LRU_WIDTH = 512
LRU_CONV = 4
LRU_C = 8.0
XATTN_HEADS = 4
XATTN_HD = 256
D_FF = 2816
FFN_CONV = 3
FFN_CHUNK = 256
FFN_NCHUNK = D_FF // FFN_CHUNK
OFF_U, OFF_Q, OFF_K, OFF_V, OFF_XR, OFF_GR, OFF_G = 0, 384, 896, 1408, 1920, 2432, 2944
D_IN = 6016
GATE_CHUNK = 512

LANES = 128
SSM_BLOCK_STATES = SSM_NSTATE // SSM_LANE_BLOCKS

ROW_TILE = 512
PROJ_ROW_TILE = 1024
FFN_ROW_TILE = 1024
REC_ROW_TILE = 1024
ATTN_TK = 512
ATTN_NORM_ROWS = 128
ATTN_MIN_ROW_SUM = 1e-25
VMEM_LIMIT = 56 * 1024 * 1024


def _dot(a, b):
    return jnp.dot(a, b, preferred_element_type=F32)


def _dot_nt(a, b):
    return lax.dot_general(a, b, (((1,), (1,)), ((), ())), preferred_element_type=F32)


def _rms(x, g):
    ms = jnp.mean(x * x, axis=-1, keepdims=True)
    return x * lax.rsqrt(ms + EPS) * g


def _sigmoid(x):
    return 1.0 / (1.0 + jnp.exp(-x))


def _gelu(x):
    c0 = 0.7978845608028654
    half_x = 0.5 * x
    return half_x + half_x * jnp.tanh(x * (c0 + (c0 * 0.044715) * (x * x)))


def _params(*sem):
    return pltpu.CompilerParams(dimension_semantics=sem, vmem_limit_bytes=VMEM_LIMIT)


def _whole(shape):
    zeros = (0,) * len(shape)
    return pl.BlockSpec(shape, lambda *_: zeros, pipeline_mode=pl.Buffered(1))


_Layer = collections.namedtuple("_Layer", "stack index")


def _wspec(w):
    if isinstance(w, _Layer):
        shape = w.stack.shape[1:]
        idx = (w.index,) + (0,) * len(shape)
        return pl.BlockSpec((None,) + shape, lambda *_: idx, pipeline_mode=pl.Buffered(1))
    return _whole(w.shape)


def _wop(w):
    return w.stack if isinstance(w, _Layer) else w


def _x_spec(batch_major, rows, width):
    if batch_major:
        return pl.BlockSpec((None, rows, width), lambda b, i: (b, i, 0))
    return pl.BlockSpec((rows, width), lambda b, i: (i, b))


def _rope_kernel(pos_ref, cos_ref, sin_ref):
    half = DIFF_HD // 2
    lane = lax.broadcasted_iota(jnp.int32, (1, LANES), 1)
    j = (lane & (half - 1)).astype(F32)
    inv = jnp.exp((-math.log(ROPE_THETA) * (2.0 * j)) / DIFF_HD)
    ang = pos_ref[...] * inv
    first_half = (lane & (DIFF_HD - 1)) < half
    cos_ref[...] = jnp.cos(ang)
    s = jnp.sin(ang)
    sin_ref[...] = jnp.where(first_half, -s, s)


def _rope_tables(pos_f, B, S):
    rows = ROW_TILE
    spec = pl.BlockSpec((None, rows, LANES), lambda b, i: (b, i, 0))
    return pl.pallas_call(
        _rope_kernel,
        grid=(B, S // rows),
        in_specs=[pl.BlockSpec((None, rows, 1), lambda b, i: (b, i, 0))],
        out_specs=[spec, spec],
        out_shape=[jax.ShapeDtypeStruct((B, S, LANES), F32)] * 2,
        compiler_params=_params("parallel", "parallel"),
        name="rope_tables",
    )(pos_f)


def _s5_prep_kernel(lr_ref, li_ref, ls_ref, br_ref, bi_ref, abr_ref, abi_ref, bbr_ref, bbi_ref):
    lr = lr_ref[...]
    li = li_ref[...]
    dt = jnp.exp(ls_ref[...])
    mag = jnp.exp(lr * dt)
    ab_r = mag * jnp.cos(li * dt)
    ab_i = mag * jnp.sin(li * dt)
    den = lr * lr + li * li
    nr = ab_r - 1.0
    f_r = (nr * lr + ab_i * li) / den
    f_i = (ab_i * lr - nr * li) / den
    br = br_ref[...]
    bi = bi_ref[...]
    abr_ref[...] = ab_r
    abi_ref[...] = ab_i
    bbr_ref[...] = f_r * br - f_i * bi
    bbi_ref[...] = f_r * bi + f_i * br


def _s5_prep(lr_rep, li_rep, ls_rep, br_t, bi_t):
    depth = lr_rep.shape[0]
    spec = pl.BlockSpec((None, SSM_WIDTH, SSM_STATE), lambda l: (l, 0, 0))
    return pl.pallas_call(
        _s5_prep_kernel,
        grid=(depth,),
        in_specs=[spec, spec, pl.BlockSpec((None, SSM_WIDTH, 1), lambda l: (l, 0, 0)), spec, spec],
        out_specs=[spec] * 4,
        out_shape=[jax.ShapeDtypeStruct((depth, SSM_WIDTH, SSM_STATE), F32)] * 4,
        compiler_params=_params("parallel"),
        name="s5_prep",
    )(lr_rep, li_rep, ls_rep, br_t, bi_t)


def _rope(x, cos, sin_signed, first_half):
    half = DIFF_HD // 2
    back = pltpu.roll(x, half, 1)
    fwd = pltpu.roll(x, DIFF_W - half, 1)
    return x * cos + jnp.where(first_half, fwd, back) * sin_signed


def _inproj_kernel(x_ref, g_ref, w_ref, cos_ref, sin_ref,
                   u_ref, q_ref, k_ref, v_ref, xr_ref, gr_ref, gt_ref):
    hn = _rms(x_ref[...], g_ref[...]).astype(BF16)

    def seg(off, width):
        return _dot(hn, w_ref[:, off:off + width])

    cos = jnp.concatenate([cos_ref[...]] * (DIFF_W // LANES), axis=1)
    sin = jnp.concatenate([sin_ref[...]] * (DIFF_W // LANES), axis=1)
    lane = lax.broadcasted_iota(jnp.int32, (1, DIFF_W), 1)
    first_half = (lane & (DIFF_HD - 1)) < DIFF_HD // 2

    u_ref[...] = seg(OFF_U, SSM_WIDTH).astype(BF16)
    q_ref[...] = (_rope(seg(OFF_Q, DIFF_W), cos, sin, first_half) * (DIFF_HD ** -0.5)).astype(BF16)
    k_ref[...] = _rope(seg(OFF_K, DIFF_W), cos, sin, first_half).astype(BF16)
    v_ref[...] = seg(OFF_V, DIFF_W).astype(BF16)
    xr_ref[...] = seg(OFF_XR, LRU_WIDTH).astype(BF16)
    gr_ref[...] = seg(OFF_GR, LRU_WIDTH).astype(BF16)
    gw = GATE_CHUNK
    for c in range(3 * D_MODEL // gw):
        gt_ref[:, c * gw:(c + 1) * gw] = _sigmoid(seg(OFF_G + c * gw, gw)).astype(BF16)


def _inproj(x, batch_major, g, w_in, cos, sin, B, S):
    rows = PROJ_ROW_TILE

    def out(width):
        return pl.BlockSpec((rows, width), lambda b, i: (i, b))

    widths = (SSM_WIDTH, DIFF_W, DIFF_W, DIFF_W, LRU_WIDTH, LRU_WIDTH, 3 * D_MODEL)
    tab = pl.BlockSpec((None, rows, LANES), lambda b, i: (b, i, 0))
    return pl.pallas_call(
        _inproj_kernel,
        grid=(B, S // rows),
        in_specs=[_x_spec(batch_major, rows, D_MODEL), _whole((1, D_MODEL)), _wspec(w_in), tab, tab],
        out_specs=[out(w) for w in widths],
        out_shape=[jax.ShapeDtypeStruct((S, B * w), BF16) for w in widths],
        compiler_params=_params("parallel", "parallel"),
        name="inproj",
    )(x, g, _wop(w_in), cos, sin)


def _swap_matrix(batch):
    idx = jnp.arange(batch * batch)
    return (idx[:, None] == (idx[None, :] % batch) * batch + idx[None, :] // batch).astype(BF16)


def _load_time_major(ref, width, batch, steps, swap):
    subs = []
    for k in range(steps // batch):
        rows = slice(k * batch, (k + 1) * batch)
        xb = jnp.concatenate([ref[rows, b * width:(b + 1) * width] for b in range(batch)], axis=0)
        subs.append(_dot(swap, xb).astype(BF16))
    return jnp.concatenate(subs, axis=0)


def _store_batch_major(ref, y, width, batch, steps, swap):
    n = batch * batch
    for k in range(steps // batch):
        yb = _dot(swap, y[k * n:(k + 1) * n]).astype(BF16)
        for b in range(batch):
            ref[k * batch:(k + 1) * batch, b * width:(b + 1) * width] = yb[b * batch:(b + 1) * batch]


def _s5_body(u_ref, swap, wb_ref, abr_ref, abi_ref, cre_ref, cim_ref, d_ref, wglu_ref, bglu_ref,
             o_ref, sre, sim, st_re, st_im, batch, steps):
    u = _load_time_major(u_ref, SSM_WIDTH, batch, steps, swap)
    nb = SSM_BLOCK_STATES
    for j in range(SSM_LANE_BLOCKS):
        bu = _dot(u[:, j * LANES:(j + 1) * LANES], wb_ref[j])
        sre[:, j * nb:(j + 1) * nb] = bu[:, :nb]
        sim[:, j * nb:(j + 1) * nb] = bu[:, nb:]

    for j in range(SSM_LANE_BLOCKS):
        cols = slice(j * nb, (j + 1) * nb)
        ar = jnp.broadcast_to(abr_ref[:, cols], (batch, nb))
        ai = jnp.broadcast_to(abi_ref[:, cols], (batch, nb))
        sr = st_re[:, cols]
        si = st_im[:, cols]
        for t in range(steps):
            rows = slice(t * batch, (t + 1) * batch)
            nr = ar * sr - ai * si + sre[rows, cols]
            ni = ar * si + ai * sr + sim[rows, cols]
            sre[rows, cols] = nr
            sim[rows, cols] = ni
            sr, si = nr, ni
        st_re[:, cols] = sr
        st_im[:, cols] = si

    ys = []
    for j in range(SSM_LANE_BLOCKS):
        cols = slice(j * nb, (j + 1) * nb)
        ys.append(_dot(sre[:, cols].astype(BF16), cre_ref[j]) - _dot(sim[:, cols].astype(BF16), cim_ref[j]))
    y = jnp.concatenate(ys, axis=1) + d_ref[...] * u.astype(F32)
    y = _gelu(y)
    z = _dot(y.astype(BF16), wglu_ref[...]) + bglu_ref[...]
    out = (z[:, :SSM_WIDTH] * _sigmoid(z[:, SSM_WIDTH:])).astype(BF16)
    _store_batch_major(o_ref, out, SSM_WIDTH, batch, steps, swap)


def _lru_body(xr_ref, gr_ref, swap, cw_ref, cb_ref, wax_ref, ba_ref, bx_ref, lam_ref,
              o_ref, ext, a_s, b_s, h_st, batch, steps):
    rows = batch * steps
    halo = (LRU_CONV - 1) * batch
    ext[halo:halo + rows, :] = _load_time_major(xr_ref, LRU_WIDTH, batch, steps, swap).astype(F32)
    xc = cb_ref[...] + cw_ref[LRU_CONV - 1:LRU_CONV, :] * ext[halo:halo + rows, :]
    for j in range(LRU_CONV - 1):
        xc = xc + cw_ref[j:j + 1, :] * ext[j * batch:j * batch + rows, :]
    ext[0:halo, :] = ext[rows:rows + halo, :]

    xb = xc.astype(BF16)
    hw = LRU_WIDTH // 2
    za = jnp.concatenate([_dot(xb[:, :hw], wax_ref[0]), _dot(xb[:, hw:], wax_ref[1])], axis=1)
    zx = jnp.concatenate([_dot(xb[:, :hw], wax_ref[2]), _dot(xb[:, hw:], wax_ref[3])], axis=1)
    r = _sigmoid(za + ba_ref[...])
    ig = _sigmoid(zx + bx_ref[...])
    log_a_scale = (-LRU_C) * jnp.log1p(jnp.exp(-lam_ref[...]))
    a = jnp.exp(r * log_a_scale)
    a_s[...] = a
    y = 1.0 - a * a
    b_s[...] = jnp.where(y > 0.0, y * lax.rsqrt(y), 0.0) * (ig * xc)

    h = h_st[...]
    for t in range(steps):
        rs = slice(t * batch, (t + 1) * batch)
        h = a_s[rs, :] * h + b_s[rs, :]
        b_s[rs, :] = h
    h_st[...] = h
    gr = _load_time_major(gr_ref, LRU_WIDTH, batch, steps, swap).astype(F32)
    _store_batch_major(o_ref, (b_s[...] * _gelu(gr)).astype(BF16), LRU_WIDTH, batch, steps, swap)


def _recurrent_kernel(u_ref, xr_ref, gr_ref, swap_ref,
                      wb_ref, abr_ref, abi_ref, cre_ref, cim_ref, d_ref, wglu_ref, bglu_ref,
                      cw_ref, cb_ref, wax_ref, ba_ref, bx_ref, lam_ref,
                      ys_ref, yl_ref, sre, sim, st_re, st_im, ext, a_s, b_s, h_st, *, batch, steps):
    @pl.when(pl.program_id(0) == 0)
    def _():
        st_re[...] = jnp.zeros_like(st_re)
        st_im[...] = jnp.zeros_like(st_im)
        ext[0:(LRU_CONV - 1) * batch, :] = jnp.zeros(((LRU_CONV - 1) * batch, LRU_WIDTH), F32)
        h_st[...] = jnp.zeros_like(h_st)

    swap = swap_ref[...]
    _s5_body(u_ref, swap, wb_ref, abr_ref, abi_ref, cre_ref, cim_ref, d_ref, wglu_ref, bglu_ref,
             ys_ref, sre, sim, st_re, st_im, batch, steps)
    _lru_body(xr_ref, gr_ref, swap, cw_ref, cb_ref, wax_ref, ba_ref, bx_ref, lam_ref,
              yl_ref, ext, a_s, b_s, h_st, batch, steps)


def _recurrent(u, xr, gr, swap, s5_params, lru_params, B):
    S = u.shape[0]
    rows = REC_ROW_TILE
    steps = rows // B
    halo = (LRU_CONV - 1) * B

    def blk(width):
        return pl.BlockSpec((steps, B * width), lambda i: (i, 0))

    params = tuple(s5_params) + tuple(lru_params)
    return pl.pallas_call(
        functools.partial(_recurrent_kernel, batch=B, steps=steps),
        grid=(S // steps,),
        in_specs=[blk(SSM_WIDTH), blk(LRU_WIDTH), blk(LRU_WIDTH), _whole(swap.shape)] + [_wspec(p) for p in params],
        out_specs=[blk(SSM_WIDTH), blk(LRU_WIDTH)],
        out_shape=[jax.ShapeDtypeStruct((S, B * SSM_WIDTH), BF16), jax.ShapeDtypeStruct((S, B * LRU_WIDTH), BF16)],
        scratch_shapes=[pltpu.VMEM((rows, SSM_NSTATE), F32), pltpu.VMEM((rows, SSM_NSTATE), F32),
                        pltpu.VMEM((B, SSM_NSTATE), F32), pltpu.VMEM((B, SSM_NSTATE), F32),
                        pltpu.VMEM((rows + halo, LRU_WIDTH), F32), pltpu.VMEM((rows, LRU_WIDTH), F32),
                        pltpu.VMEM((rows, LRU_WIDTH), F32), pltpu.VMEM((B, LRU_WIDTH), F32)],
        compiler_params=_params("arbitrary"),
        name="recurrent_branches",
    )(u, xr, gr, swap, *[_wop(p) for p in params])


def _diff_attn_kernel(q_ref, k_ref, v_ref, lq1_ref, lk1_ref, lq2_ref, lk2_ref, g_ref, o_ref,
                      qq, m_s, acc_s, kn_s, *, lambda_init, tk):
    qi = pl.program_id(1)
    hw = 2 * DIFF_HD
    tq = 2 * tk
    ts = tk // 2
    nstrip = tq // ts
    every = slice(0, 2 * tq)
    lane = lax.broadcasted_iota(jnp.int32, (1, hw), 1)

    @pl.when((pl.program_id(0) == 0) & (qi == 0))
    def _():
        acc_s[...] = jnp.zeros_like(acc_s)

    for h in range(DIFF_HEADS):
        for st in range(nstrip):
            q = q_ref[st * ts:(st + 1) * ts, h * hw:(h + 1) * hw]
            zero = jnp.zeros_like(q)
            qq[h, 2 * st * ts:(2 * st + 1) * ts, :] = jnp.where(lane < DIFF_HD, q, zero)
            qq[h, (2 * st + 1) * ts:(2 * st + 2) * ts, :] = jnp.where(lane >= DIFF_HD, q, zero)

    @pl.when(qi == 0)
    def _():
        kn_s[...] = jnp.zeros_like(kn_s)

    new_keys = pl.multiple_of(qi * tq, tq)
    d_row = lax.broadcasted_iota(jnp.int32, (hw, 2 * hw), 0)
    d_col = lax.broadcasted_iota(jnp.int32, (hw, 2 * hw), 1)
    comp_sum = ((d_row < DIFF_HD) == (d_col < hw)).astype(BF16)

    def max_sq_norm(x):
        sq = x * x
        fold = sq[0:ATTN_NORM_ROWS]
        for r in range(ATTN_NORM_ROWS, tq, ATTN_NORM_ROWS):
            fold = jnp.maximum(fold, sq[r:r + ATTN_NORM_ROWS])
        return jnp.max(_dot(fold, comp_sum), axis=0, keepdims=True)

    bound = []
    for h in range(DIFF_HEADS):
        kn = jnp.maximum(kn_s[h:h + 1, :], max_sq_norm(k_ref[pl.ds(new_keys, tq), h * hw:(h + 1) * hw]))
        kn_s[h:h + 1, :] = kn
        qn = max_sq_norm(q_ref[:, h * hw:(h + 1) * hw])
        n2 = qn * kn
        b = n2 * lax.rsqrt(jnp.maximum(n2, 1e-30))
        bound.append((b[:, 0:1], b[:, hw:hw + 1]))

    def scores(start, nkeys, h, rows, diagonal):
        s = _dot_nt(qq[h, rows, :], k_ref[pl.ds(start, nkeys), h * hw:(h + 1) * hw])
        if diagonal:
            row = lax.broadcasted_iota(jnp.int32, s.shape, 0)
            col = lax.broadcasted_iota(jnp.int32, s.shape, 1)
            s = jnp.where((col <= (row & (ts - 1))) | (row >= 2 * ts), s, -1e30)
        return s

    def chunks(nkeys):
        return [slice(c * hw, (c + 1) * hw) for c in range(nkeys // hw)]

    def max_step(start, nkeys, rows, diagonal):
        for h in range(DIFF_HEADS):
            s = scores(start, nkeys, h, rows, diagonal)
            m = m_s[h, rows, :]
            for c in chunks(nkeys):
                m = jnp.maximum(m, s[:, c])
            m_s[h, rows, :] = m

    def shift_bound(h, rows, s):
        return jnp.concatenate([s[i * ts:(i + 1) * ts] - bound[h][i % 2] for i in range(s.shape[0] // ts)], axis=0)

    def shift_row_max(h, rows, s):
        m = m_s[h, rows, :]
        return jnp.concatenate([s[:, c] - m for c in chunks(s.shape[1])], axis=1)

    def acc_step(shift, start, nkeys, rows, diagonal):
        ones = jnp.ones((nkeys, hw), BF16)
        for h in range(DIFF_HEADS):
            p = jnp.exp(shift(h, rows, scores(start, nkeys, h, rows, diagonal))).astype(BF16)
            v_ext = jnp.concatenate([v_ref[pl.ds(start, nkeys), h * hw:(h + 1) * hw], ones], axis=1)
            acc_s[h, rows, :] = acc_s[h, rows, :] + _dot(p, v_ext)

    def loop(step):
        def body(j, c):
            step(pl.multiple_of(j * tk, tk), tk, every, False)
            return c
        lax.fori_loop(0, 2 * qi, body, 0)
        for st in range(nstrip):
            step(pl.multiple_of(qi * tq + st * ts, ts), ts, slice(2 * st * ts, 2 * tq), True)

    loop(functools.partial(acc_step, shift_bound))
    row_sum_min = jnp.min(acc_s[0][:, hw:])
    for h in range(1, DIFF_HEADS):
        row_sum_min = jnp.minimum(row_sum_min, jnp.min(acc_s[h][:, hw:]))

    @pl.when(jnp.logical_not(row_sum_min >= ATTN_MIN_ROW_SUM))
    def _():
        m_s[...] = jnp.full(m_s.shape, -1e30, F32)
        acc_s[...] = jnp.zeros_like(acc_s)
        loop(max_step)
        for h in range(DIFF_HEADS):
            m_s[h] = jnp.broadcast_to(jnp.max(m_s[h], axis=-1, keepdims=True), (2 * tq, hw))
        loop(functools.partial(acc_step, shift_row_max))

    lam = (jnp.exp(jnp.sum(lq1_ref[...] * lk1_ref[...], axis=-1, keepdims=True))
           - jnp.exp(jnp.sum(lq2_ref[...] * lk2_ref[...], axis=-1, keepdims=True)) + lambda_init)
    for h in range(DIFF_HEADS):
        for st in range(nstrip):
            acc = acc_s[h, 2 * st * ts:(2 * st + 2) * ts, :]
            acc_s[h, 2 * st * ts:(2 * st + 2) * ts, :] = jnp.zeros_like(acc)
            o = acc[:, :hw] / acc[:, hw:]
            o = o[:ts] - lam * o[ts:]
            o_ref[st * ts:(st + 1) * ts, h * hw:(h + 1) * hw] = (
                _rms(o, g_ref[...]) * (1.0 - lambda_init)).astype(BF16)


def _diff_attn(q, k, v, lq1, lk1, lq2, lk2, g, lambda_init, B, S):
    tk = ATTN_TK
    tq = 2 * tk
    hw = 2 * DIFF_HD
    qspec = pl.BlockSpec((tq, DIFF_W), lambda b, i: (i, b))
    kspec = pl.BlockSpec((S, DIFF_W), lambda b, i: (0, b))
    small = pl.BlockSpec((1, DIFF_HD), lambda b, i: (0, 0))
    return pl.pallas_call(
        functools.partial(_diff_attn_kernel, lambda_init=lambda_init, tk=tk),
        grid=(B, S // tq),
        in_specs=[qspec, kspec, kspec, small, small, small, small,
                  pl.BlockSpec((1, hw), lambda b, i: (0, 0))],
        out_specs=qspec,
        out_shape=jax.ShapeDtypeStruct((S, B * DIFF_W), BF16),
        scratch_shapes=[pltpu.VMEM((DIFF_HEADS, 2 * tq, hw), BF16), pltpu.VMEM((DIFF_HEADS, 2 * tq, hw), F32),
                        pltpu.VMEM((DIFF_HEADS, 2 * tq, 2 * hw), F32), pltpu.VMEM((DIFF_HEADS, 2 * hw), F32)],
        compiler_params=_params("arbitrary", "arbitrary"),
        name="diff_attn",
    )(q, k, v, lq1, lk1, lq2, lk2, g)


def _merge_xattn_kernel(x_ref, ys_ref, ya_ref, yl_ref, gt_ref, wbs_ref, wba_ref, wbl_ref, wout_ref,
                        g_ref, wq_ref, k_ref, v_ref, wo_ref, o_ref):
    d = D_MODEL
    m = gt_ref[:, 0:d].astype(F32) * _dot(ys_ref[...], wbs_ref[...])
    m = m + gt_ref[:, d:2 * d].astype(F32) * _dot(ya_ref[...], wba_ref[...])
    m = m + gt_ref[:, 2 * d:3 * d].astype(F32) * _dot(yl_ref[...], wbl_ref[...])
    x = x_ref[...] + _dot(m.astype(BF16), wout_ref[...])

    hn = _rms(x, g_ref[...]).astype(BF16)
    q = (_dot(hn, wq_ref[...]) * (XATTN_HD ** -0.5)).astype(BF16)
    outs = []
    for h in range(XATTN_HEADS):
        cols = slice(h * XATTN_HD, (h + 1) * XATTN_HD)
        s = _dot_nt(q[:, cols], k_ref[:, cols])
        p = jnp.exp(s - jnp.max(s, axis=-1, keepdims=True))
        p = p / jnp.sum(p, axis=-1, keepdims=True)
        outs.append(_dot(p.astype(BF16), v_ref[:, cols]).astype(BF16))
    o_ref[...] = x + _dot(jnp.concatenate(outs, axis=1), wo_ref[...])


def _merge_xattn(x, batch_major, ys, ya, yl, gt, wbs, wba, wbl, wout, g, wq, kv, wo, mem_len, B, S):
    rows = PROJ_ROW_TILE

    def blk(width):
        return pl.BlockSpec((rows, width), lambda b, i: (i, b))

    return pl.pallas_call(
        _merge_xattn_kernel,
        grid=(B, S // rows),
        in_specs=[_x_spec(batch_major, rows, D_MODEL), blk(SSM_WIDTH), blk(DIFF_W), blk(LRU_WIDTH),
                  blk(3 * D_MODEL), _wspec(wbs), _wspec(wba), _wspec(wbl), _wspec(wout),
                  _whole(g.shape), _wspec(wq),
                  pl.BlockSpec((mem_len, D_MODEL), lambda b, i: (b, 0)),
                  pl.BlockSpec((mem_len, D_MODEL), lambda b, i: (b, 1)),
                  _wspec(wo)],
        out_specs=blk(D_MODEL),
        out_shape=jax.ShapeDtypeStruct((S, B * D_MODEL), F32),
        compiler_params=_params("parallel", "parallel"),
        name="merge_xattn",
    )(x, ys, ya, yl, gt, _wop(wbs), _wop(wba), _wop(wbl), _wop(wout), g, _wop(wq), kv, kv, _wop(wo))


def _memkv_kernel(m_ref, g_ref, w_ref, o_ref):
    hn = _rms(m_ref[...], g_ref[...]).astype(BF16)
    o_ref[...] = _dot(hn, w_ref[...]).astype(BF16)


def _memkv(mem_rows, g, wkv):
    n = mem_rows.shape[0]
    rows = ROW_TILE
    return pl.pallas_call(
        _memkv_kernel,
        grid=(n // rows,),
        in_specs=[pl.BlockSpec((rows, D_MODEL), lambda i: (i, 0)), _whole(g.shape), _wspec(wkv)],
        out_specs=pl.BlockSpec((rows, 2 * D_MODEL), lambda i: (i, 0)),
        out_shape=jax.ShapeDtypeStruct((n, 2 * D_MODEL), BF16),
        compiler_params=_params("parallel"),
        name="mem_kv",
    )(mem_rows, g, _wop(wkv))


def _ffn_kernel(x_ref, swap_ref, g_ref, wup_ref, cw_ref, cb_ref, wdn_ref, gf_ref, o_ref, prev, act_s,
                *, batch, steps, final):
    rows = batch * steps
    d = D_MODEL
    halo = (FFN_CONV - 1) * batch
    n = batch * batch

    @pl.when(pl.program_id(0) == 0)
    def _():
        prev[...] = jnp.zeros_like(prev)

    swap = swap_ref[...]
    g = g_ref[...]
    hn_b = [_rms(x_ref[:, b * d:(b + 1) * d], g).astype(BF16) for b in range(batch)]
    subs = []
    for k in range(steps // batch):
        hb = jnp.concatenate([h[k * batch:(k + 1) * batch] for h in hn_b], axis=0)
        subs.append(_dot(swap, hb).astype(BF16))
    hn = jnp.concatenate(subs, axis=0)

    def conv(up, part, cols):
        hist = prev[part]
        y = cb_ref[:, cols] + cw_ref[FFN_CONV - 1:FFN_CONV, cols] * up
        for t in range(FFN_CONV - 1):
            back = (FFN_CONV - 1 - t) * batch
            shifted = jnp.concatenate([hist[halo - back:], up[:rows - back]], axis=0)
            y = y + cw_ref[t:t + 1, cols] * shifted
        prev[part] = up[rows - halo:]
        return y

    for j in range(FFN_NCHUNK):
        vc = slice(j * FFN_CHUNK, (j + 1) * FFN_CHUNK)
        gc = slice(D_FF + j * FFN_CHUNK, D_FF + (j + 1) * FFN_CHUNK)
        val = conv(_dot(hn, wup_ref[:, vc]), 2 * j, vc)
        gate = conv(_dot(hn, wup_ref[:, gc]), 2 * j + 1, gc)
        act_s[:, vc] = (gate * _sigmoid(gate) * val).astype(BF16)
    acc = _dot(act_s[...], wdn_ref[...])

    for k in range(steps // batch):
        yb = jnp.swapaxes(acc[k * n:(k + 1) * n].reshape(batch, batch, d), 0, 1).reshape(n, d)
        ts = slice(k * batch, (k + 1) * batch)
        for b in range(batch):
            out = x_ref[ts, b * d:(b + 1) * d] + yb[b * batch:(b + 1) * batch]
            if final:
                o_ref[b, ts, :] = _rms(out, gf_ref[...])
            else:
                o_ref[ts, b * d:(b + 1) * d] = out


def _ffn(x, swap, g, wup, cw, cb, wdn, gf, B, S, final):
    rows = FFN_ROW_TILE
    steps = rows // B
    blk = pl.BlockSpec((steps, B * D_MODEL), lambda i: (i, 0))
    if final:
        out_spec = pl.BlockSpec((B, steps, D_MODEL), lambda i: (0, i, 0))
        out_shape = jax.ShapeDtypeStruct((B, S, D_MODEL), F32)
    else:
        out_spec, out_shape = blk, jax.ShapeDtypeStruct((S, B * D_MODEL), F32)
    return pl.pallas_call(
        functools.partial(_ffn_kernel, batch=B, steps=steps, final=final),
        grid=(S // steps,),
        in_specs=[blk, _whole(swap.shape), _whole(g.shape), _wspec(wup), _whole(cw.shape), _whole(cb.shape),
                  _wspec(wdn), _whole(gf.shape)],
        out_specs=out_spec,
        out_shape=out_shape,
        scratch_shapes=[pltpu.VMEM((2 * FFN_NCHUNK, (FFN_CONV - 1) * B, FFN_CHUNK), F32),
                        pltpu.VMEM((rows, D_FF), BF16)],
        compiler_params=_params("arbitrary"),
        name="conv_ffn",
    )(x, swap, g, _wop(wup), cw, cb, _wop(wdn), gf)


def _row(v):
    return v.reshape(1, -1).astype(F32)


def _block_diag(blocks):
    n, r, c = blocks.shape
    eye = jnp.eye(n, dtype=blocks.dtype)
    return (blocks[:, :, None, :] * eye[:, None, :, None]).reshape(n * r, n * c)


def _s5_b_weights(bbr, bbi):
    per = SSM_GROUPS // SSM_LANE_BLOCKS
    out = []
    for j in range(SSM_LANE_BLOCKS):
        r = bbr[j * LANES:(j + 1) * LANES].reshape(per, SSM_GROUP, SSM_STATE)
        i = bbi[j * LANES:(j + 1) * LANES].reshape(per, SSM_GROUP, SSM_STATE)
        out.append(jnp.concatenate([_block_diag(r), _block_diag(i)], axis=1))
    return jnp.stack(out).astype(BF16)


def _s5_c_weights(c):
    per = SSM_GROUPS // SSM_LANE_BLOCKS
    ct = jnp.swapaxes(c, 1, 2)
    return jnp.stack([_block_diag(ct[j * per:(j + 1) * per]) for j in range(SSM_LANE_BLOCKS)]).astype(BF16)


def kernel(x, mem, positions, norm_mix_g, w_in, ssm_lambda_re, ssm_lambda_im, ssm_log_step, ssm_b_re, ssm_b_im, ssm_c_re, ssm_c_im, ssm_d, ssm_w_glu, ssm_b_glu, diff_lq1, diff_lk1, diff_lq2, diff_lk2, diff_subln_g, lru_conv_w, lru_conv_b, lru_wa, lru_ba, lru_wx, lru_bx, lru_lambda, w_br_ssm, w_br_attn, w_br_lru, w_out, norm_xattn_g, norm_mem_g, xattn_wq, xattn_wkv, xattn_wo, norm_ffn_g, ffn_w_up, ffn_conv_w, ffn_conv_b, ffn_w_down, final_norm_g):
    B, S, _ = x.shape
    depth = norm_mix_g.shape[0]
    mem_len = mem.shape[1]
    assert S % ROW_TILE == 0 and ROW_TILE % B == 0 and B % 8 == 0 and (B * mem_len) % ROW_TILE == 0
    assert S % (2 * ATTN_TK) == 0 and FFN_ROW_TILE % (B * B) == 0 and (S * B) % FFN_ROW_TILE == 0
    assert REC_ROW_TILE % (B * B) == 0 and (S * B) % REC_ROW_TILE == 0 and S % PROJ_ROW_TILE == 0

    cos, sin = _rope_tables(positions.astype(F32)[..., None], B, S)

    rep = lambda a: jnp.repeat(a, SSM_GROUP, axis=1)
    b_t = lambda a: jnp.swapaxes(a, 2, 3).reshape(depth, SSM_WIDTH, SSM_STATE)
    abr, abi, bbr, bbi = _s5_prep(rep(ssm_lambda_re), rep(ssm_lambda_im), rep(ssm_log_step[..., None]),
                                  b_t(ssm_b_re), b_t(ssm_b_im))
    mem_rows = mem.reshape(B * mem_len, D_MODEL)
    swap = _swap_matrix(B)
    (w_in, ssm_w_glu, w_br_ssm, w_br_attn, w_br_lru, w_out, xattn_wq, xattn_wkv, xattn_wo, ffn_w_up,
     ffn_w_down) = (a.astype(BF16) for a in (w_in, ssm_w_glu, w_br_ssm, w_br_attn, w_br_lru, w_out, xattn_wq,
                                             xattn_wkv, xattn_wo, ffn_w_up, ffn_w_down))

    xs = x
    batch_major = True
    for l in range(depth):
        lambda_init = 0.8 - 0.6 * math.exp(-0.3 * l)
        u, q, k, v, xr, gr, gt = _inproj(xs, batch_major, _row(norm_mix_g[l]), _Layer(w_in, l), cos, sin, B, S)

        s5_params = (_s5_b_weights(bbr[l], bbi[l]),
                     abr[l, ::SSM_GROUP].reshape(1, SSM_NSTATE), abi[l, ::SSM_GROUP].reshape(1, SSM_NSTATE),
                     _s5_c_weights(ssm_c_re[l]), _s5_c_weights(ssm_c_im[l]), _row(ssm_d[l]),
                     _Layer(ssm_w_glu, l), _row(ssm_b_glu[l]))
        nh = lru_wa.shape[1] // 2
        wax = jnp.stack([_block_diag(w[l, i * nh:(i + 1) * nh]) for w in (lru_wa, lru_wx)
                         for i in range(2)]).astype(BF16)
        lru_params = (lru_conv_w[l].astype(F32), _row(lru_conv_b[l]), wax, _row(lru_ba[l]), _row(lru_bx[l]),
                      _row(lru_lambda[l]))
        y_ssm, y_lru = _recurrent(u, xr, gr, swap, s5_params, lru_params, B)

        y_att = _diff_attn(q, k, v, _row(diff_lq1[l]), _row(diff_lk1[l]), _row(diff_lq2[l]), _row(diff_lk2[l]),
                           _row(diff_subln_g[l]), lambda_init, B, S)

        kv = _memkv(mem_rows, _row(norm_mem_g[l]), _Layer(xattn_wkv, l))
        xs = _merge_xattn(xs, batch_major, y_ssm, y_att, y_lru, gt, _Layer(w_br_ssm, l), _Layer(w_br_attn, l),
                          _Layer(w_br_lru, l), _Layer(w_out, l), _row(norm_xattn_g[l]), _Layer(xattn_wq, l), kv,
                          _Layer(xattn_wo, l), mem_len, B, S)
        batch_major = False

        xs = _ffn(xs, swap, _row(norm_ffn_g[l]), _Layer(ffn_w_up, l), ffn_conv_w[l].astype(F32),
                  _row(ffn_conv_b[l]), _Layer(ffn_w_down, l), _row(final_norm_g), B, S, final=l == depth - 1)

    return xs
```

```python
import collections
import functools
import math

import jax
import jax.numpy as jnp
from jax import lax
from jax.experimental import pallas as pl
from jax.experimental.pallas import tpu as pltpu

F32 = jnp.float32
BF16 = jnp.bfloat16

EPS = 1e-6
D_MODEL = 1024
SSM_WIDTH = 384
SSM_GROUP = 16
SSM_GROUPS = 24
SSM_STATE = 64
SSM_NSTATE = SSM_GROUPS * SSM_STATE
SSM_LANE_BLOCKS = 3
DIFF_HEADS = 4
DIFF_HD = 64
DIFF_W = 512
ROPE_THETA = 10000.0
LRU_WIDTH = 512
LRU_CONV = 4
LRU_C = 8.0
XATTN_HEADS = 4
XATTN_HD = 256
D_FF = 2816
FFN_CONV = 3
FFN_CHUNK = 256
FFN_NCHUNK = D_FF // FFN_CHUNK
OFF_U, OFF_Q, OFF_K, OFF_V, OFF_XR, OFF_GR, OFF_G = 0, 384, 896, 1408, 1920, 2432, 2944
D_IN = 6016
GATE_CHUNK = 512

LANES = 128
SSM_BLOCK_STATES = SSM_NSTATE // SSM_LANE_BLOCKS

ROW_TILE = 512
PROJ_ROW_TILE = 1024
FFN_ROW_TILE = 1024
REC_ROW_TILE = 1024
ATTN_TK = 512
ATTN_NORM_ROWS = 128
ATTN_MIN_ROW_SUM = 1e-25
VMEM_LIMIT = 56 * 1024 * 1024


def _dot(a, b):
    return jnp.dot(a, b, preferred_element_type=F32)


def _dot_nt(a, b):
    return lax.dot_general(a, b, (((1,), (1,)), ((), ())), preferred_element_type=F32)


def _rms(x, g):
    ms = jnp.mean(x * x, axis=-1, keepdims=True)
    return x * lax.rsqrt(ms + EPS) * g


def _sigmoid(x):
    return 1.0 / (1.0 + jnp.exp(-x))


def _gelu(x):
    c0 = 0.7978845608028654
    half_x = 0.5 * x
    return half_x + half_x * jnp.tanh(x * (c0 + (c0 * 0.044715) * (x * x)))


def _params(*sem):
    return pltpu.CompilerParams(dimension_semantics=sem, vmem_limit_bytes=VMEM_LIMIT)


def _whole(shape):
    zeros = (0,) * len(shape)
    return pl.BlockSpec(shape, lambda *_: zeros, pipeline_mode=pl.Buffered(1))


_Layer = collections.namedtuple("_Layer", "stack index")


def _wspec(w):
    if isinstance(w, _Layer):
        shape = w.stack.shape[1:]
        idx = (w.index,) + (0,) * len(shape)
        return pl.BlockSpec((None,) + shape, lambda *_: idx, pipeline_mode=pl.Buffered(1))
    return _whole(w.shape)


def _wop(w):
    return w.stack if isinstance(w, _Layer) else w


def _x_spec(batch_major, rows, width):
    if batch_major:
        return pl.BlockSpec((None, rows, width), lambda b, i: (b, i, 0))
    return pl.BlockSpec((rows, width), lambda b, i: (i, b))


def _rope_kernel(pos_ref, cos_ref, sin_ref):
    half = DIFF_HD // 2
    lane = lax.broadcasted_iota(jnp.int32, (1, LANES), 1)
    j = (lane & (half - 1)).astype(F32)
    inv = jnp.exp((-math.log(ROPE_THETA) * (2.0 * j)) / DIFF_HD)
    ang = pos_ref[...] * inv
    first_half = (lane & (DIFF_HD - 1)) < half
    cos_ref[...] = jnp.cos(ang)
    s = jnp.sin(ang)
    sin_ref[...] = jnp.where(first_half, -s, s)


def _rope_tables(pos_f, B, S):
    rows = ROW_TILE
    spec = pl.BlockSpec((None, rows, LANES), lambda b, i: (b, i, 0))
    return pl.pallas_call(
        _rope_kernel,
        grid=(B, S // rows),
        in_specs=[pl.BlockSpec((None, rows, 1), lambda b, i: (b, i, 0))],
        out_specs=[spec, spec],
        out_shape=[jax.ShapeDtypeStruct((B, S, LANES), F32)] * 2,
        compiler_params=_params("parallel", "parallel"),
        name="rope_tables",
    )(pos_f)


def _s5_prep_kernel(lr_ref, li_ref, ls_ref, br_ref, bi_ref, abr_ref, abi_ref, bbr_ref, bbi_ref):
    lr = lr_ref[...]
    li = li_ref[...]
    dt = jnp.exp(ls_ref[...])
    mag = jnp.exp(lr * dt)
    ab_r = mag * jnp.cos(li * dt)
    ab_i = mag * jnp.sin(li * dt)
    den = lr * lr + li * li
    nr = ab_r - 1.0
    f_r = (nr * lr + ab_i * li) / den
    f_i = (ab_i * lr - nr * li) / den
    br = br_ref[...]
    bi = bi_ref[...]
    abr_ref[...] = ab_r
    abi_ref[...] = ab_i
    bbr_ref[...] = f_r * br - f_i * bi
    bbi_ref[...] = f_r * bi + f_i * br


def _s5_prep(lr_rep, li_rep, ls_rep, br_t, bi_t):
    depth = lr_rep.shape[0]
    spec = pl.BlockSpec((None, SSM_WIDTH, SSM_STATE), lambda l: (l, 0, 0))
    return pl.pallas_call(
        _s5_prep_kernel,
        grid=(depth,),
        in_specs=[spec, spec, pl.BlockSpec((None, SSM_WIDTH, 1), lambda l: (l, 0, 0)), spec, spec],
        out_specs=[spec] * 4,
        out_shape=[jax.ShapeDtypeStruct((depth, SSM_WIDTH, SSM_STATE), F32)] * 4,
        compiler_params=_params("parallel"),
        name="s5_prep",
    )(lr_rep, li_rep, ls_rep, br_t, bi_t)


def _rope(x, cos, sin_signed, first_half):
    half = DIFF_HD // 2
    back = pltpu.roll(x, half, 1)
    fwd = pltpu.roll(x, DIFF_W - half, 1)
    return x * cos + jnp.where(first_half, fwd, back) * sin_signed


def _inproj_kernel(x_ref, g_ref, w_ref, cos_ref, sin_ref,
                   u_ref, q_ref, k_ref, v_ref, xr_ref, gr_ref, gt_ref):
    hn = _rms(x_ref[...], g_ref[...]).astype(BF16)

    def seg(off, width):
        return _dot(hn, w_ref[:, off:off + width])

    cos = jnp.concatenate([cos_ref[...]] * (DIFF_W // LANES), axis=1)
    sin = jnp.concatenate([sin_ref[...]] * (DIFF_W // LANES), axis=1)
    lane = lax.broadcasted_iota(jnp.int32, (1, DIFF_W), 1)
    first_half = (lane & (DIFF_HD - 1)) < DIFF_HD // 2

    u_ref[...] = seg(OFF_U, SSM_WIDTH).astype(BF16)
    q_ref[...] = (_rope(seg(OFF_Q, DIFF_W), cos, sin, first_half) * (DIFF_HD ** -0.5)).astype(BF16)
    k_ref[...] = _rope(seg(OFF_K, DIFF_W), cos, sin, first_half).astype(BF16)
    v_ref[...] = seg(OFF_V, DIFF_W).astype(BF16)
    xr_ref[...] = seg(OFF_XR, LRU_WIDTH).astype(BF16)
    gr_ref[...] = seg(OFF_GR, LRU_WIDTH).astype(BF16)
    gw = GATE_CHUNK
    for c in range(3 * D_MODEL // gw):
        gt_ref[:, c * gw:(c + 1) * gw] = _sigmoid(seg(OFF_G + c * gw, gw)).astype(BF16)


def _inproj(x, batch_major, g, w_in, cos, sin, B, S):
    rows = PROJ_ROW_TILE

    def out(width):
        return pl.BlockSpec((rows, width), lambda b, i: (i, b))

    widths = (SSM_WIDTH, DIFF_W, DIFF_W, DIFF_W, LRU_WIDTH, LRU_WIDTH, 3 * D_MODEL)
    tab = pl.BlockSpec((None, rows, LANES), lambda b, i: (b, i, 0))
    return pl.pallas_call(
        _inproj_kernel,
        grid=(B, S // rows),
        in_specs=[_x_spec(batch_major, rows, D_MODEL), _whole((1, D_MODEL)), _wspec(w_in), tab, tab],
        out_specs=[out(w) for w in widths],
        out_shape=[jax.ShapeDtypeStruct((S, B * w), BF16) for w in widths],
        compiler_params=_params("parallel", "parallel"),
        name="inproj",
    )(x, g, _wop(w_in), cos, sin)


def _swap_matrix(batch):
    idx = jnp.arange(batch * batch)
    return (idx[:, None] == (idx[None, :] % batch) * batch + idx[None, :] // batch).astype(BF16)


def _load_time_major(ref, width, batch, steps, swap):
    subs = []
    for k in range(steps // batch):
        rows = slice(k * batch, (k + 1) * batch)
        xb = jnp.concatenate([ref[rows, b * width:(b + 1) * width] for b in range(batch)], axis=0)
        subs.append(_dot(swap, xb).astype(BF16))
    return jnp.concatenate(subs, axis=0)


def _store_batch_major(ref, y, width, batch, steps, swap):
    n = batch * batch
    for k in range(steps // batch):
        yb = _dot(swap, y[k * n:(k + 1) * n]).astype(BF16)
        for b in range(batch):
            ref[k * batch:(k + 1) * batch, b * width:(b + 1) * width] = yb[b * batch:(b + 1) * batch]


def _s5_body(u_ref, swap, wb_ref, abr_ref, abi_ref, cre_ref, cim_ref, d_ref, wglu_ref, bglu_ref,
             o_ref, sre, sim, st_re, st_im, batch, steps):
    u = _load_time_major(u_ref, SSM_WIDTH, batch, steps, swap)
    nb = SSM_BLOCK_STATES
    for j in range(SSM_LANE_BLOCKS):
        bu = _dot(u[:, j * LANES:(j + 1) * LANES], wb_ref[j])
        sre[:, j * nb:(j + 1) * nb] = bu[:, :nb]
        sim[:, j * nb:(j + 1) * nb] = bu[:, nb:]

    for j in range(SSM_LANE_BLOCKS):
        cols = slice(j * nb, (j + 1) * nb)
        ar = jnp.broadcast_to(abr_ref[:, cols], (batch, nb))
        ai = jnp.broadcast_to(abi_ref[:, cols], (batch, nb))
        sr = st_re[:, cols]
        si = st_im[:, cols]
        for t in range(steps):
            rows = slice(t * batch, (t + 1) * batch)
            nr = ar * sr - ai * si + sre[rows, cols]
            ni = ar * si + ai * sr + sim[rows, cols]
            sre[rows, cols] = nr
            sim[rows, cols] = ni
            sr, si = nr, ni
        st_re[:, cols] = sr
        st_im[:, cols] = si

    ys = []
    for j in range(SSM_LANE_BLOCKS):
        cols = slice(j * nb, (j + 1) * nb)
        ys.append(_dot(sre[:, cols].astype(BF16), cre_ref[j]) - _dot(sim[:, cols].astype(BF16), cim_ref[j]))
    y = jnp.concatenate(ys, axis=1) + d_ref[...] * u.astype(F32)
    y = _gelu(y)
    z = _dot(y.astype(BF16), wglu_ref[...]) + bglu_ref[...]
    out = (z[:, :SSM_WIDTH] * _sigmoid(z[:, SSM_WIDTH:])).astype(BF16)
    _store_batch_major(o_ref, out, SSM_WIDTH, batch, steps, swap)


def _lru_body(xr_ref, gr_ref, swap, cw_ref, cb_ref, wax_ref, ba_ref, bx_ref, lam_ref,
              o_ref, ext, a_s, b_s, h_st, batch, steps):
    rows = batch * steps
    halo = (LRU_CONV - 1) * batch
    ext[halo:halo + rows, :] = _load_time_major(xr_ref, LRU_WIDTH, batch, steps, swap).astype(F32)
    xc = cb_ref[...] + cw_ref[LRU_CONV - 1:LRU_CONV, :] * ext[halo:halo + rows, :]
    for j in range(LRU_CONV - 1):
        xc = xc + cw_ref[j:j + 1, :] * ext[j * batch:j * batch + rows, :]
    ext[0:halo, :] = ext[rows:rows + halo, :]

    xb = xc.astype(BF16)
    hw = LRU_WIDTH // 2
    za = jnp.concatenate([_dot(xb[:, :hw], wax_ref[0]), _dot(xb[:, hw:], wax_ref[1])], axis=1)
    zx = jnp.concatenate([_dot(xb[:, :hw], wax_ref[2]), _dot(xb[:, hw:], wax_ref[3])], axis=1)
    r = _sigmoid(za + ba_ref[...])
    ig = _sigmoid(zx + bx_ref[...])
    log_a_scale = (-LRU_C) * jnp.log1p(jnp.exp(-lam_ref[...]))
    a = jnp.exp(r * log_a_scale)
    a_s[...] = a
    y = 1.0 - a * a
    b_s[...] = jnp.where(y > 0.0, y * lax.rsqrt(y), 0.0) * (ig * xc)

    h = h_st[...]
    for t in range(steps):
        rs = slice(t * batch, (t + 1) * batch)
        h = a_s[rs, :] * h + b_s[rs, :]
        b_s[rs, :] = h
    h_st[...] = h
    gr = _load_time_major(gr_ref, LRU_WIDTH, batch, steps, swap).astype(F32)
    _store_batch_major(o_ref, (b_s[...] * _gelu(gr)).astype(BF16), LRU_WIDTH, batch, steps, swap)


def _recurrent_kernel(u_ref, xr_ref, gr_ref, swap_ref,
                      wb_ref, abr_ref, abi_ref, cre_ref, cim_ref, d_ref, wglu_ref, bglu_ref,
                      cw_ref, cb_ref, wax_ref, ba_ref, bx_ref, lam_ref,
                      ys_ref, yl_ref, sre, sim, st_re, st_im, ext, a_s, b_s, h_st, *, batch, steps):
    @pl.when(pl.program_id(0) == 0)
    def _():
        st_re[...] = jnp.zeros_like(st_re)
        st_im[...] = jnp.zeros_like(st_im)
        ext[0:(LRU_CONV - 1) * batch, :] = jnp.zeros(((LRU_CONV - 1) * batch, LRU_WIDTH), F32)
        h_st[...] = jnp.zeros_like(h_st)

    swap = swap_ref[...]
    _s5_body(u_ref, swap, wb_ref, abr_ref, abi_ref, cre_ref, cim_ref, d_ref, wglu_ref, bglu_ref,
             ys_ref, sre, sim, st_re, st_im, batch, steps)
    _lru_body(xr_ref, gr_ref, swap, cw_ref, cb_ref, wax_ref, ba_ref, bx_ref, lam_ref,
              yl_ref, ext, a_s, b_s, h_st, batch, steps)


def _recurrent(u, xr, gr, swap, s5_params, lru_params, B):
    S = u.shape[0]
    rows = REC_ROW_TILE
    steps = rows // B
    halo = (LRU_CONV - 1) * B

    def blk(width):
        return pl.BlockSpec((steps, B * width), lambda i: (i, 0))

    params = tuple(s5_params) + tuple(lru_params)
    return pl.pallas_call(
        functools.partial(_recurrent_kernel, batch=B, steps=steps),
        grid=(S // steps,),
        in_specs=[blk(SSM_WIDTH), blk(LRU_WIDTH), blk(LRU_WIDTH), _whole(swap.shape)] + [_wspec(p) for p in params],
        out_specs=[blk(SSM_WIDTH), blk(LRU_WIDTH)],
        out_shape=[jax.ShapeDtypeStruct((S, B * SSM_WIDTH), BF16), jax.ShapeDtypeStruct((S, B * LRU_WIDTH), BF16)],
        scratch_shapes=[pltpu.VMEM((rows, SSM_NSTATE), F32), pltpu.VMEM((rows, SSM_NSTATE), F32),
                        pltpu.VMEM((B, SSM_NSTATE), F32), pltpu.VMEM((B, SSM_NSTATE), F32),
                        pltpu.VMEM((rows + halo, LRU_WIDTH), F32), pltpu.VMEM((rows, LRU_WIDTH), F32),
                        pltpu.VMEM((rows, LRU_WIDTH), F32), pltpu.VMEM((B, LRU_WIDTH), F32)],
        compiler_params=_params("arbitrary"),
        name="recurrent_branches",
    )(u, xr, gr, swap, *[_wop(p) for p in params])


def _diff_attn_kernel(q_ref, k_ref, v_ref, lq1_ref, lk1_ref, lq2_ref, lk2_ref, g_ref, o_ref,
                      qq, m_s, acc_s, kn_s, *, lambda_init, tk):
    qi = pl.program_id(1)
    hw = 2 * DIFF_HD
    tq = 2 * tk
    ts = tk // 2
    nstrip = tq // ts
    every = slice(0, 2 * tq)
    lane = lax.broadcasted_iota(jnp.int32, (1, hw), 1)

    @pl.when((pl.program_id(0) == 0) & (qi == 0))
    def _():
        acc_s[...] = jnp.zeros_like(acc_s)

    for h in range(DIFF_HEADS):
        for st in range(nstrip):
            q = q_ref[st * ts:(st + 1) * ts, h * hw:(h + 1) * hw]
            zero = jnp.zeros_like(q)
            qq[h, 2 * st * ts:(2 * st + 1) * ts, :] = jnp.where(lane < DIFF_HD, q, zero)
            qq[h, (2 * st + 1) * ts:(2 * st + 2) * ts, :] = jnp.where(lane >= DIFF_HD, q, zero)

    @pl.when(qi == 0)
    def _():
        kn_s[...] = jnp.zeros_like(kn_s)

    new_keys = pl.multiple_of(qi * tq, tq)
    d_row = lax.broadcasted_iota(jnp.int32, (hw, 2 * hw), 0)
    d_col = lax.broadcasted_iota(jnp.int32, (hw, 2 * hw), 1)
    comp_sum = ((d_row < DIFF_HD) == (d_col < hw)).astype(BF16)

    def max_sq_norm(x):
        sq = x * x
        fold = sq[0:ATTN_NORM_ROWS]
        for r in range(ATTN_NORM_ROWS, tq, ATTN_NORM_ROWS):
            fold = jnp.maximum(fold, sq[r:r + ATTN_NORM_ROWS])
        return jnp.max(_dot(fold, comp_sum), axis=0, keepdims=True)

    bound = []
    for h in range(DIFF_HEADS):
        kn = jnp.maximum(kn_s[h:h + 1, :], max_sq_norm(k_ref[pl.ds(new_keys, tq), h * hw:(h + 1) * hw]))
        kn_s[h:h + 1, :] = kn
        qn = max_sq_norm(q_ref[:, h * hw:(h + 1) * hw])
        n2 = qn * kn
        b = n2 * lax.rsqrt(jnp.maximum(n2, 1e-30))
        bound.append((b[:, 0:1], b[:, hw:hw + 1]))

    def scores(start, nkeys, h, rows, diagonal):
        s = _dot_nt(qq[h, rows, :], k_ref[pl.ds(start, nkeys), h * hw:(h + 1) * hw])
        if diagonal:
            row = lax.broadcasted_iota(jnp.int32, s.shape, 0)
            col = lax.broadcasted_iota(jnp.int32, s.shape, 1)
            s = jnp.where((col <= (row & (ts - 1))) | (row >= 2 * ts), s, -1e30)
        return s

    def chunks(nkeys):
        return [slice(c * hw, (c + 1) * hw) for c in range(nkeys // hw)]

    def max_step(start, nkeys, rows, diagonal):
        for h in range(DIFF_HEADS):
            s = scores(start, nkeys, h, rows, diagonal)
            m = m_s[h, rows, :]
            for c in chunks(nkeys):
                m = jnp.maximum(m, s[:, c])
            m_s[h, rows, :] = m

    def shift_bound(h, rows, s):
        return jnp.concatenate([s[i * ts:(i + 1) * ts] - bound[h][i % 2] for i in range(s.shape[0] // ts)], axis=0)

    def shift_row_max(h, rows, s):
        m = m_s[h, rows, :]
        return jnp.concatenate([s[:, c] - m for c in chunks(s.shape[1])], axis=1)

    def acc_step(shift, start, nkeys, rows, diagonal):
        ones = jnp.ones((nkeys, hw), BF16)
        for h in range(DIFF_HEADS):
            p = jnp.exp(shift(h, rows, scores(start, nkeys, h, rows, diagonal))).astype(BF16)
            v_ext = jnp.concatenate([v_ref[pl.ds(start, nkeys), h * hw:(h + 1) * hw], ones], axis=1)
            acc_s[h, rows, :] = acc_s[h, rows, :] + _dot(p, v_ext)

    def loop(step):
        def body(j, c):
            step(pl.multiple_of(j * tk, tk), tk, every, False)
            return c
        lax.fori_loop(0, 2 * qi, body, 0)
        for st in range(nstrip):
            step(pl.multiple_of(qi * tq + st * ts, ts), ts, slice(2 * st * ts, 2 * tq), True)

    loop(functools.partial(acc_step, shift_bound))
    row_sum_min = jnp.min(acc_s[0][:, hw:])
    for h in range(1, DIFF_HEADS):
        row_sum_min = jnp.minimum(row_sum_min, jnp.min(acc_s[h][:, hw:]))

    @pl.when(jnp.logical_not(row_sum_min >= ATTN_MIN_ROW_SUM))
    def _():
        m_s[...] = jnp.full(m_s.shape, -1e30, F32)
        acc_s[...] = jnp.zeros_like(acc_s)
        loop(max_step)
        for h in range(DIFF_HEADS):
            m_s[h] = jnp.broadcast_to(jnp.max(m_s[h], axis=-1, keepdims=True), (2 * tq, hw))
        loop(functools.partial(acc_step, shift_row_max))

    lam = (jnp.exp(jnp.sum(lq1_ref[...] * lk1_ref[...], axis=-1, keepdims=True))
           - jnp.exp(jnp.sum(lq2_ref[...] * lk2_ref[...], axis=-1, keepdims=True)) + lambda_init)
    for h in range(DIFF_HEADS):
        for st in range(nstrip):
            acc = acc_s[h, 2 * st * ts:(2 * st + 2) * ts, :]
            acc_s[h, 2 * st * ts:(2 * st + 2) * ts, :] = jnp.zeros_like(acc)
            o = acc[:, :hw] / acc[:, hw:]
            o = o[:ts] - lam * o[ts:]
            o_ref[st * ts:(st + 1) * ts, h * hw:(h + 1) * hw] = (
                _rms(o, g_ref[...]) * (1.0 - lambda_init)).astype(BF16)


def _diff_attn(q, k, v, lq1, lk1, lq2, lk2, g, lambda_init, B, S):
    tk = ATTN_TK
    tq = 2 * tk
    hw = 2 * DIFF_HD
    qspec = pl.BlockSpec((tq, DIFF_W), lambda b, i: (i, b))
    kspec = pl.BlockSpec((S, DIFF_W), lambda b, i: (0, b))
    small = pl.BlockSpec((1, DIFF_HD), lambda b, i: (0, 0))
    return pl.pallas_call(
        functools.partial(_diff_attn_kernel, lambda_init=lambda_init, tk=tk),
        grid=(B, S // tq),
        in_specs=[qspec, kspec, kspec, small, small, small, small,
                  pl.BlockSpec((1, hw), lambda b, i: (0, 0))],
        out_specs=qspec,
        out_shape=jax.ShapeDtypeStruct((S, B * DIFF_W), BF16),
        scratch_shapes=[pltpu.VMEM((DIFF_HEADS, 2 * tq, hw), BF16), pltpu.VMEM((DIFF_HEADS, 2 * tq, hw), F32),
                        pltpu.VMEM((DIFF_HEADS, 2 * tq, 2 * hw), F32), pltpu.VMEM((DIFF_HEADS, 2 * hw), F32)],
        compiler_params=_params("arbitrary", "arbitrary"),
        name="diff_attn",
    )(q, k, v, lq1, lk1, lq2, lk2, g)


def _merge_xattn_kernel(x_ref, ys_ref, ya_ref, yl_ref, gt_ref, wbs_ref, wba_ref, wbl_ref, wout_ref,
                        g_ref, wq_ref, k_ref, v_ref, wo_ref, o_ref):
    d = D_MODEL
    m = gt_ref[:, 0:d].astype(F32) * _dot(ys_ref[...], wbs_ref[...])
    m = m + gt_ref[:, d:2 * d].astype(F32) * _dot(ya_ref[...], wba_ref[...])
    m = m + gt_ref[:, 2 * d:3 * d].astype(F32) * _dot(yl_ref[...], wbl_ref[...])
    x = x_ref[...] + _dot(m.astype(BF16), wout_ref[...])

    hn = _rms(x, g_ref[...]).astype(BF16)
    q = (_dot(hn, wq_ref[...]) * (XATTN_HD ** -0.5)).astype(BF16)
    outs = []
    for h in range(XATTN_HEADS):
        cols = slice(h * XATTN_HD, (h + 1) * XATTN_HD)
        s = _dot_nt(q[:, cols], k_ref[:, cols])
        p = jnp.exp(s - jnp.max(s, axis=-1, keepdims=True))
        p = p / jnp.sum(p, axis=-1, keepdims=True)
        outs.append(_dot(p.astype(BF16), v_ref[:, cols]).astype(BF16))
    o_ref[...] = x + _dot(jnp.concatenate(outs, axis=1), wo_ref[...])


def _merge_xattn(x, batch_major, ys, ya, yl, gt, wbs, wba, wbl, wout, g, wq, kv, wo, mem_len, B, S):
    rows = PROJ_ROW_TILE

    def blk(width):
        return pl.BlockSpec((rows, width), lambda b, i: (i, b))

    return pl.pallas_call(
        _merge_xattn_kernel,
        grid=(B, S // rows),
        in_specs=[_x_spec(batch_major, rows, D_MODEL), blk(SSM_WIDTH), blk(DIFF_W), blk(LRU_WIDTH),
                  blk(3 * D_MODEL), _wspec(wbs), _wspec(wba), _wspec(wbl), _wspec(wout),
                  _whole(g.shape), _wspec(wq),
                  pl.BlockSpec((mem_len, D_MODEL), lambda b, i: (b, 0)),
                  pl.BlockSpec((mem_len, D_MODEL), lambda b, i: (b, 1)),
                  _wspec(wo)],
        out_specs=blk(D_MODEL),
        out_shape=jax.ShapeDtypeStruct((S, B * D_MODEL), F32),
        compiler_params=_params("parallel", "parallel"),
        name="merge_xattn",
    )(x, ys, ya, yl, gt, _wop(wbs), _wop(wba), _wop(wbl), _wop(wout), g, _wop(wq), kv, kv, _wop(wo))


def _memkv_kernel(m_ref, g_ref, w_ref, o_ref):
    hn = _rms(m_ref[...], g_ref[...]).astype(BF16)
    o_ref[...] = _dot(hn, w_ref[...]).astype(BF16)


def _memkv(mem_rows, g, wkv):
    n = mem_rows.shape[0]
    rows = ROW_TILE
    return pl.pallas_call(
        _memkv_kernel,
        grid=(n // rows,),
        in_specs=[pl.BlockSpec((rows, D_MODEL), lambda i: (i, 0)), _whole(g.shape), _wspec(wkv)],
        out_specs=pl.BlockSpec((rows, 2 * D_MODEL), lambda i: (i, 0)),
        out_shape=jax.ShapeDtypeStruct((n, 2 * D_MODEL), BF16),
        compiler_params=_params("parallel"),
        name="mem_kv",
    )(mem_rows, g, _wop(wkv))


def _ffn_kernel(x_ref, swap_ref, g_ref, wup_ref, cw_ref, cb_ref, wdn_ref, gf_ref, o_ref, prev, act_s,
                *, batch, steps, final):
    rows = batch * steps
    d = D_MODEL
    halo = (FFN_CONV - 1) * batch
    n = batch * batch

    @pl.when(pl.program_id(0) == 0)
    def _():
        prev[...] = jnp.zeros_like(prev)

    swap = swap_ref[...]
    g = g_ref[...]
    hn_b = [_rms(x_ref[:, b * d:(b + 1) * d], g).astype(BF16) for b in range(batch)]
    subs = []
    for k in range(steps // batch):
        hb = jnp.concatenate([h[k * batch:(k + 1) * batch] for h in hn_b], axis=0)
        subs.append(_dot(swap, hb).astype(BF16))
    hn = jnp.concatenate(subs, axis=0)

    def conv(up, part, cols):
        hist = prev[part]
        y = cb_ref[:, cols] + cw_ref[FFN_CONV - 1:FFN_CONV, cols] * up
        for t in range(FFN_CONV - 1):
            back = (FFN_CONV - 1 - t) * batch
            shifted = jnp.concatenate([hist[halo - back:], up[:rows - back]], axis=0)
            y = y + cw_ref[t:t + 1, cols] * shifted
        prev[part] = up[rows - halo:]
        return y

    for j in range(FFN_NCHUNK):
        vc = slice(j * FFN_CHUNK, (j + 1) * FFN_CHUNK)
        gc = slice(D_FF + j * FFN_CHUNK, D_FF + (j + 1) * FFN_CHUNK)
        val = conv(_dot(hn, wup_ref[:, vc]), 2 * j, vc)
        gate = conv(_dot(hn, wup_ref[:, gc]), 2 * j + 1, gc)
        act_s[:, vc] = (gate * _sigmoid(gate) * val).astype(BF16)
    acc = _dot(act_s[...], wdn_ref[...])

    for k in range(steps // batch):
        yb = jnp.swapaxes(acc[k * n:(k + 1) * n].reshape(batch, batch, d), 0, 1).reshape(n, d)
        ts = slice(k * batch, (k + 1) * batch)
        for b in range(batch):
            out = x_ref[ts, b * d:(b + 1) * d] + yb[b * batch:(b + 1) * batch]
            if final:
                o_ref[b, ts, :] = _rms(out, gf_ref[...])
            else:
                o_ref[ts, b * d:(b + 1) * d] = out


def _ffn(x, swap, g, wup, cw, cb, wdn, gf, B, S, final):
    rows = FFN_ROW_TILE
    steps = rows // B
    blk = pl.BlockSpec((steps, B * D_MODEL), lambda i: (i, 0))
    if final:
        out_spec = pl.BlockSpec((B, steps, D_MODEL), lambda i: (0, i, 0))
        out_shape = jax.ShapeDtypeStruct((B, S, D_MODEL), F32)
    else:
        out_spec, out_shape = blk, jax.ShapeDtypeStruct((S, B * D_MODEL), F32)
    return pl.pallas_call(
        functools.partial(_ffn_kernel, batch=B, steps=steps, final=final),
        grid=(S // steps,),
        in_specs=[blk, _whole(swap.shape), _whole(g.shape), _wspec(wup), _whole(cw.shape), _whole(cb.shape),
                  _wspec(wdn), _whole(gf.shape)],
        out_specs=out_spec,
        out_shape=out_shape,
        scratch_shapes=[pltpu.VMEM((2 * FFN_NCHUNK, (FFN_CONV - 1) * B, FFN_CHUNK), F32),
                        pltpu.VMEM((rows, D_FF), BF16)],
        compiler_params=_params("arbitrary"),
        name="conv_ffn",
    )(x, swap, g, _wop(wup), cw, cb, _wop(wdn), gf)


def _row(v):
    return v.reshape(1, -1).astype(F32)


def _block_diag(blocks):
    n, r, c = blocks.shape
    eye = jnp.eye(n, dtype=blocks.dtype)
    return (blocks[:, :, None, :] * eye[:, None, :, None]).reshape(n * r, n * c)


def _s5_b_weights(bbr, bbi):
    per = SSM_GROUPS // SSM_LANE_BLOCKS
    out = []
    for j in range(SSM_LANE_BLOCKS):
        r = bbr[j * LANES:(j + 1) * LANES].reshape(per, SSM_GROUP, SSM_STATE)
        i = bbi[j * LANES:(j + 1) * LANES].reshape(per, SSM_GROUP, SSM_STATE)
        out.append(jnp.concatenate([_block_diag(r), _block_diag(i)], axis=1))
    return jnp.stack(out).astype(BF16)


def _s5_c_weights(c):
    per = SSM_GROUPS // SSM_LANE_BLOCKS
    ct = jnp.swapaxes(c, 1, 2)
    return jnp.stack([_block_diag(ct[j * per:(j + 1) * per]) for j in range(SSM_LANE_BLOCKS)]).astype(BF16)


def kernel(x, mem, positions, norm_mix_g, w_in, ssm_lambda_re, ssm_lambda_im, ssm_log_step, ssm_b_re, ssm_b_im, ssm_c_re, ssm_c_im, ssm_d, ssm_w_glu, ssm_b_glu, diff_lq1, diff_lk1, diff_lq2, diff_lk2, diff_subln_g, lru_conv_w, lru_conv_b, lru_wa, lru_ba, lru_wx, lru_bx, lru_lambda, w_br_ssm, w_br_attn, w_br_lru, w_out, norm_xattn_g, norm_mem_g, xattn_wq, xattn_wkv, xattn_wo, norm_ffn_g, ffn_w_up, ffn_conv_w, ffn_conv_b, ffn_w_down, final_norm_g):
    B, S, _ = x.shape
    depth = norm_mix_g.shape[0]
    mem_len = mem.shape[1]
    assert x.shape[-1] == D_MODEL and w_in.shape[1:] == (D_MODEL, D_IN)
    assert S % ROW_TILE == 0 and ROW_TILE % B == 0 and B % 8 == 0 and (B * mem_len) % ROW_TILE == 0
    assert S % (2 * ATTN_TK) == 0 and FFN_ROW_TILE % (B * B) == 0 and (S * B) % FFN_ROW_TILE == 0
    assert REC_ROW_TILE % (B * B) == 0 and (S * B) % REC_ROW_TILE == 0 and S % PROJ_ROW_TILE == 0

    cos, sin = _rope_tables(positions.astype(F32)[..., None], B, S)

    rep = lambda a: jnp.repeat(a, SSM_GROUP, axis=1)
    b_t = lambda a: jnp.swapaxes(a, 2, 3).reshape(depth, SSM_WIDTH, SSM_STATE)
    abr, abi, bbr, bbi = _s5_prep(rep(ssm_lambda_re), rep(ssm_lambda_im), rep(ssm_log_step[..., None]),
                                  b_t(ssm_b_re), b_t(ssm_b_im))
    mem_rows = mem.reshape(B * mem_len, D_MODEL)
    swap = _swap_matrix(B)
    (w_in, ssm_w_glu, w_br_ssm, w_br_attn, w_br_lru, w_out, xattn_wq, xattn_wkv, xattn_wo, ffn_w_up,
     ffn_w_down) = (a.astype(BF16) for a in (w_in, ssm_w_glu, w_br_ssm, w_br_attn, w_br_lru, w_out, xattn_wq,
                                             xattn_wkv, xattn_wo, ffn_w_up, ffn_w_down))

    xs = x
    batch_major = True
    for l in range(depth):
        lambda_init = 0.8 - 0.6 * math.exp(-0.3 * l)
        u, q, k, v, xr, gr, gt = _inproj(xs, batch_major, _row(norm_mix_g[l]), _Layer(w_in, l), cos, sin, B, S)

        s5_params = (_s5_b_weights(bbr[l], bbi[l]),
                     abr[l, ::SSM_GROUP].reshape(1, SSM_NSTATE), abi[l, ::SSM_GROUP].reshape(1, SSM_NSTATE),
                     _s5_c_weights(ssm_c_re[l]), _s5_c_weights(ssm_c_im[l]), _row(ssm_d[l]),
                     _Layer(ssm_w_glu, l), _row(ssm_b_glu[l]))
        nh = lru_wa.shape[1] // 2
        wax = jnp.stack([_block_diag(w[l, i * nh:(i + 1) * nh]) for w in (lru_wa, lru_wx)
                         for i in range(2)]).astype(BF16)
        lru_params = (lru_conv_w[l].astype(F32), _row(lru_conv_b[l]), wax, _row(lru_ba[l]), _row(lru_bx[l]),
                      _row(lru_lambda[l]))
        y_ssm, y_lru = _recurrent(u, xr, gr, swap, s5_params, lru_params, B)

        y_att = _diff_attn(q, k, v, _row(diff_lq1[l]), _row(diff_lk1[l]), _row(diff_lq2[l]), _row(diff_lk2[l]),
                           _row(diff_subln_g[l]), lambda_init, B, S)

        kv = _memkv(mem_rows, _row(norm_mem_g[l]), _Layer(xattn_wkv, l))
        xs = _merge_xattn(xs, batch_major, y_ssm, y_att, y_lru, gt, _Layer(w_br_ssm, l), _Layer(w_br_attn, l),
                          _Layer(w_br_lru, l), _Layer(w_out, l), _row(norm_xattn_g[l]), _Layer(xattn_wq, l), kv,
                          _Layer(xattn_wo, l), mem_len, B, S)
        batch_major = False

        xs = _ffn(xs, swap, _row(norm_ffn_g[l]), _Layer(ffn_w_up, l), ffn_conv_w[l].astype(F32),
                  _row(ffn_conv_b[l]), _Layer(ffn_w_down, l), _row(final_norm_g), B, S, final=l == depth - 1)

    return xs
```

```python
import collections
import functools
import math

import jax
import jax.numpy as jnp
from jax import lax
from jax.experimental import pallas as pl
from jax.experimental.pallas import tpu as pltpu

F32 = jnp.float32
BF16 = jnp.bfloat16

EPS = 1e-6
D_MODEL = 1024
SSM_WIDTH = 384
SSM_GROUP = 16
SSM_GROUPS = 24
SSM_STATE = 64
SSM_NSTATE = SSM_GROUPS * SSM_STATE
SSM_LANE_BLOCKS = 3
DIFF_HEADS = 4
DIFF_HD = 64
DIFF_W = 512
ROPE_THETA = 10000.0
LRU_WIDTH = 512
LRU_CONV = 4
LRU_C = 8.0
XATTN_HEADS = 4
XATTN_HD = 256
D_FF = 2816
FFN_CONV = 3
FFN_CHUNK = 256
FFN_NCHUNK = D_FF // FFN_CHUNK
OFF_U, OFF_Q, OFF_K, OFF_V, OFF_XR, OFF_GR, OFF_G = 0, 384, 896, 1408, 1920, 2432, 2944
D_IN = 6016
GATE_CHUNK = 512

LANES = 128
SSM_BLOCK_STATES = SSM_NSTATE // SSM_LANE_BLOCKS

ROW_TILE = 512
PROJ_ROW_TILE = 1024
FFN_ROW_TILE = 1024
REC_ROW_TILE = 1024
ATTN_TK = 512
ATTN_NORM_ROWS = 128
ATTN_MIN_ROW_SUM = 1e-25
VMEM_LIMIT = 56 * 1024 * 1024


def _dot(a, b):
    return jnp.dot(a, b, preferred_element_type=F32)


def _dot_nt(a, b):
    return lax.dot_general(a, b, (((1,), (1,)), ((), ())), preferred_element_type=F32)


def _rms(x, g):
    ms = jnp.mean(x * x, axis=-1, keepdims=True)
    return x * lax.rsqrt(ms + EPS) * g


def _sigmoid(x):
    return 1.0 / (1.0 + jnp.exp(-x))


def _gelu(x):
    c0 = 0.7978845608028654
    half_x = 0.5 * x
    return half_x + half_x * jnp.tanh(x * (c0 + (c0 * 0.044715) * (x * x)))


def _params(*sem):
    return pltpu.CompilerParams(dimension_semantics=sem, vmem_limit_bytes=VMEM_LIMIT)


def _whole(shape):
    zeros = (0,) * len(shape)
    return pl.BlockSpec(shape, lambda *_: zeros, pipeline_mode=pl.Buffered(1))


_Layer = collections.namedtuple("_Layer", "stack index")


def _wspec(w):
    if isinstance(w, _Layer):
        shape = w.stack.shape[1:]
        idx = (w.index,) + (0,) * len(shape)
        return pl.BlockSpec((None,) + shape, lambda *_: idx, pipeline_mode=pl.Buffered(1))
    return _whole(w.shape)


def _wop(w):
    return w.stack if isinstance(w, _Layer) else w


def _x_spec(batch_major, rows, width):
    if batch_major:
        return pl.BlockSpec((None, rows, width), lambda b, i: (b, i, 0))
    return pl.BlockSpec((rows, width), lambda b, i: (i, b))


def _rope_kernel(pos_ref, cos_ref, sin_ref):
    half = DIFF_HD // 2
    lane = lax.broadcasted_iota(jnp.int32, (1, LANES), 1)
    j = (lane & (half - 1)).astype(F32)
    inv = jnp.exp((-math.log(ROPE_THETA) * (2.0 * j)) / DIFF_HD)
    ang = pos_ref[...] * inv
    first_half = (lane & (DIFF_HD - 1)) < half
    cos_ref[...] = jnp.cos(ang)
    s = jnp.sin(ang)
    sin_ref[...] = jnp.where(first_half, -s, s)


def _rope_tables(pos_f, B, S):
    rows = ROW_TILE
    spec = pl.BlockSpec((None, rows, LANES), lambda b, i: (b, i, 0))
    return pl.pallas_call(
        _rope_kernel,
        grid=(B, S // rows),
        in_specs=[pl.BlockSpec((None, rows, 1), lambda b, i: (b, i, 0))],
        out_specs=[spec, spec],
        out_shape=[jax.ShapeDtypeStruct((B, S, LANES), F32)] * 2,
        compiler_params=_params("parallel", "parallel"),
        name="rope_tables",
    )(pos_f)


def _s5_prep_kernel(lr_ref, li_ref, ls_ref, br_ref, bi_ref, abr_ref, abi_ref, bbr_ref, bbi_ref):
    lr = lr_ref[...]
    li = li_ref[...]
    dt = jnp.exp(ls_ref[...])
    mag = jnp.exp(lr * dt)
    ab_r = mag * jnp.cos(li * dt)
    ab_i = mag * jnp.sin(li * dt)
    den = lr * lr + li * li
    nr = ab_r - 1.0
    f_r = (nr * lr + ab_i * li) / den
    f_i = (ab_i * lr - nr * li) / den
    br = br_ref[...]
    bi = bi_ref[...]
    abr_ref[...] = ab_r
    abi_ref[...] = ab_i
    bbr_ref[...] = f_r * br - f_i * bi
    bbi_ref[...] = f_r * bi + f_i * br


def _s5_prep(lr_rep, li_rep, ls_rep, br_t, bi_t):
    depth = lr_rep.shape[0]
    spec = pl.BlockSpec((None, SSM_WIDTH, SSM_STATE), lambda l: (l, 0, 0))
    return pl.pallas_call(
        _s5_prep_kernel,
        grid=(depth,),
        in_specs=[spec, spec, pl.BlockSpec((None, SSM_WIDTH, 1), lambda l: (l, 0, 0)), spec, spec],
        out_specs=[spec] * 4,
        out_shape=[jax.ShapeDtypeStruct((depth, SSM_WIDTH, SSM_STATE), F32)] * 4,
        compiler_params=_params("parallel"),
        name="s5_prep",
    )(lr_rep, li_rep, ls_rep, br_t, bi_t)


def _rope(x, cos, sin_signed, first_half):
    half = DIFF_HD // 2
    back = pltpu.roll(x, half, 1)
    fwd = pltpu.roll(x, DIFF_W - half, 1)
    return x * cos + jnp.where(first_half, fwd, back) * sin_signed


def _inproj_kernel(x_ref, g_ref, w_ref, cos_ref, sin_ref,
                   u_ref, q_ref, k_ref, v_ref, xr_ref, gr_ref, gt_ref):
    hn = _rms(x_ref[...], g_ref[...]).astype(BF16)

    def seg(off, width):
        return _dot(hn, w_ref[:, off:off + width])

    cos = jnp.concatenate([cos_ref[...]] * (DIFF_W // LANES), axis=1)
    sin = jnp.concatenate([sin_ref[...]] * (DIFF_W // LANES), axis=1)
    lane = lax.broadcasted_iota(jnp.int32, (1, DIFF_W), 1)
    first_half = (lane & (DIFF_HD - 1)) < DIFF_HD // 2

    u_ref[...] = seg(OFF_U, SSM_WIDTH).astype(BF16)
    q_ref[...] = (_rope(seg(OFF_Q, DIFF_W), cos, sin, first_half) * (DIFF_HD ** -0.5)).astype(BF16)
    k_ref[...] = _rope(seg(OFF_K, DIFF_W), cos, sin, first_half).astype(BF16)
    v_ref[...] = seg(OFF_V, DIFF_W).astype(BF16)
    xr_ref[...] = seg(OFF_XR, LRU_WIDTH).astype(BF16)
    gr_ref[...] = seg(OFF_GR, LRU_WIDTH).astype(BF16)
    gw = GATE_CHUNK
    for c in range(3 * D_MODEL // gw):
        gt_ref[:, c * gw:(c + 1) * gw] = _sigmoid(seg(OFF_G + c * gw, gw)).astype(BF16)


def _inproj(x, batch_major, g, w_in, cos, sin, B, S):
    rows = PROJ_ROW_TILE

    def out(width):
        return pl.BlockSpec((rows, width), lambda b, i: (i, b))

    widths = (SSM_WIDTH, DIFF_W, DIFF_W, DIFF_W, LRU_WIDTH, LRU_WIDTH, 3 * D_MODEL)
    tab = pl.BlockSpec((None, rows, LANES), lambda b, i: (b, i, 0))
    return pl.pallas_call(
        _inproj_kernel,
        grid=(B, S // rows),
        in_specs=[_x_spec(batch_major, rows, D_MODEL), _whole((1, D_MODEL)), _wspec(w_in), tab, tab],
        out_specs=[out(w) for w in widths],
        out_shape=[jax.ShapeDtypeStruct((S, B * w), BF16) for w in widths],
        compiler_params=_params("parallel", "parallel"),
        name="inproj",
    )(x, g, _wop(w_in), cos, sin)


def _swap_matrix(batch):
    idx = jnp.arange(batch * batch)
    return (idx[:, None] == (idx[None, :] % batch) * batch + idx[None, :] // batch).astype(BF16)


def _load_time_major(ref, width, batch, steps, swap):
    subs = []
    for k in range(steps // batch):
        rows = slice(k * batch, (k + 1) * batch)
        xb = jnp.concatenate([ref[rows, b * width:(b + 1) * width] for b in range(batch)], axis=0)
        subs.append(_dot(swap, xb).astype(BF16))
    return jnp.concatenate(subs, axis=0)


def _store_batch_major(ref, y, width, batch, steps, swap):
    n = batch * batch
    for k in range(steps // batch):
        yb = _dot(swap, y[k * n:(k + 1) * n]).astype(BF16)
        for b in range(batch):
            ref[k * batch:(k + 1) * batch, b * width:(b + 1) * width] = yb[b * batch:(b + 1) * batch]


def _s5_body(u_ref, swap, wb_ref, abr_ref, abi_ref, cre_ref, cim_ref, d_ref, wglu_ref, bglu_ref,
             o_ref, sre, sim, st_re, st_im, batch, steps):
    u = _load_time_major(u_ref, SSM_WIDTH, batch, steps, swap)
    nb = SSM_BLOCK_STATES
    for j in range(SSM_LANE_BLOCKS):
        bu = _dot(u[:, j * LANES:(j + 1) * LANES], wb_ref[j])
        sre[:, j * nb:(j + 1) * nb] = bu[:, :nb]
        sim[:, j * nb:(j + 1) * nb] = bu[:, nb:]

    for j in range(SSM_LANE_BLOCKS):
        cols = slice(j * nb, (j + 1) * nb)
        ar = jnp.broadcast_to(abr_ref[:, cols], (batch, nb))
        ai = jnp.broadcast_to(abi_ref[:, cols], (batch, nb))
        sr = st_re[:, cols]
        si = st_im[:, cols]
        for t in range(steps):
            rows = slice(t * batch, (t + 1) * batch)
            nr = ar * sr - ai * si + sre[rows, cols]
            ni = ar * si + ai * sr + sim[rows, cols]
            sre[rows, cols] = nr
            sim[rows, cols] = ni
            sr, si = nr, ni
        st_re[:, cols] = sr
        st_im[:, cols] = si

    ys = []
    for j in range(SSM_LANE_BLOCKS):
        cols = slice(j * nb, (j + 1) * nb)
        ys.append(_dot(sre[:, cols].astype(BF16), cre_ref[j]) - _dot(sim[:, cols].astype(BF16), cim_ref[j]))
    y = jnp.concatenate(ys, axis=1) + d_ref[...] * u.astype(F32)
    y = _gelu(y)
    z = _dot(y.astype(BF16), wglu_ref[...]) + bglu_ref[...]
    out = (z[:, :SSM_WIDTH] * _sigmoid(z[:, SSM_WIDTH:])).astype(BF16)
    _store_batch_major(o_ref, out, SSM_WIDTH, batch, steps, swap)


def _lru_body(xr_ref, gr_ref, swap, cw_ref, cb_ref, wax_ref, ba_ref, bx_ref, lam_ref,
              o_ref, ext, a_s, b_s, h_st, batch, steps):
    rows = batch * steps
    halo = (LRU_CONV - 1) * batch
    ext[halo:halo + rows, :] = _load_time_major(xr_ref, LRU_WIDTH, batch, steps, swap).astype(F32)
    xc = cb_ref[...] + cw_ref[LRU_CONV - 1:LRU_CONV, :] * ext[halo:halo + rows, :]
    for j in range(LRU_CONV - 1):
        xc = xc + cw_ref[j:j + 1, :] * ext[j * batch:j * batch + rows, :]
    ext[0:halo, :] = ext[rows:rows + halo, :]

    xb = xc.astype(BF16)
    hw = LRU_WIDTH // 2
    za = jnp.concatenate([_dot(xb[:, :hw], wax_ref[0]), _dot(xb[:, hw:], wax_ref[1])], axis=1)
    zx = jnp.concatenate([_dot(xb[:, :hw], wax_ref[2]), _dot(xb[:, hw:], wax_ref[3])], axis=1)
    r = _sigmoid(za + ba_ref[...])
    ig = _sigmoid(zx + bx_ref[...])
    log_a_scale = (-LRU_C) * jnp.log1p(jnp.exp(-lam_ref[...]))
    a = jnp.exp(r * log_a_scale)
    a_s[...] = a
    y = 1.0 - a * a
    b_s[...] = jnp.where(y > 0.0, y * lax.rsqrt(y), 0.0) * (ig * xc)

    h = h_st[...]
    for t in range(steps):
        rs = slice(t * batch, (t + 1) * batch)
        h = a_s[rs, :] * h + b_s[rs, :]
        b_s[rs, :] = h
    h_st[...] = h
    gr = _load_time_major(gr_ref, LRU_WIDTH, batch, steps, swap).astype(F32)
    _store_batch_major(o_ref, (b_s[...] * _gelu(gr)).astype(BF16), LRU_WIDTH, batch, steps, swap)


def _recurrent_kernel(u_ref, xr_ref, gr_ref, swap_ref,
                      wb_ref, abr_ref, abi_ref, cre_ref, cim_ref, d_ref, wglu_ref, bglu_ref,
                      cw_ref, cb_ref, wax_ref, ba_ref, bx_ref, lam_ref,
                      ys_ref, yl_ref, sre, sim, st_re, st_im, ext, a_s, b_s, h_st, *, batch, steps):
    @pl.when(pl.program_id(0) == 0)
    def _():
        st_re[...] = jnp.zeros_like(st_re)
        st_im[...] = jnp.zeros_like(st_im)
        ext[0:(LRU_CONV - 1) * batch, :] = jnp.zeros(((LRU_CONV - 1) * batch, LRU_WIDTH), F32)
        h_st[...] = jnp.zeros_like(h_st)

    swap = swap_ref[...]
    _s5_body(u_ref, swap, wb_ref, abr_ref, abi_ref, cre_ref, cim_ref, d_ref, wglu_ref, bglu_ref,
             ys_ref, sre, sim, st_re, st_im, batch, steps)
    _lru_body(xr_ref, gr_ref, swap, cw_ref, cb_ref, wax_ref, ba_ref, bx_ref, lam_ref,
              yl_ref, ext, a_s, b_s, h_st, batch, steps)


def _recurrent(u, xr, gr, swap, s5_params, lru_params, B):
    S = u.shape[0]
    rows = REC_ROW_TILE
    steps = rows // B
    halo = (LRU_CONV - 1) * B

    def blk(width):
        return pl.BlockSpec((steps, B * width), lambda i: (i, 0))

    params = tuple(s5_params) + tuple(lru_params)
    return pl.pallas_call(
        functools.partial(_recurrent_kernel, batch=B, steps=steps),
        grid=(S // steps,),
        in_specs=[blk(SSM_WIDTH), blk(LRU_WIDTH), blk(LRU_WIDTH), _whole(swap.shape)] + [_wspec(p) for p in params],
        out_specs=[blk(SSM_WIDTH), blk(LRU_WIDTH)],
        out_shape=[jax.ShapeDtypeStruct((S, B * SSM_WIDTH), BF16), jax.ShapeDtypeStruct((S, B * LRU_WIDTH), BF16)],
        scratch_shapes=[pltpu.VMEM((rows, SSM_NSTATE), F32), pltpu.VMEM((rows, SSM_NSTATE), F32),
                        pltpu.VMEM((B, SSM_NSTATE), F32), pltpu.VMEM((B, SSM_NSTATE), F32),
                        pltpu.VMEM((rows + halo, LRU_WIDTH), F32), pltpu.VMEM((rows, LRU_WIDTH), F32),
                        pltpu.VMEM((rows, LRU_WIDTH), F32), pltpu.VMEM((B, LRU_WIDTH), F32)],
        compiler_params=_params("arbitrary"),
        name="recurrent_branches",
    )(u, xr, gr, swap, *[_wop(p) for p in params])


def _diff_attn_kernel(q_ref, k_ref, v_ref, lq1_ref, lk1_ref, lq2_ref, lk2_ref, g_ref, o_ref,
                      qq, m_s, acc_s, kn_s, *, lambda_init, tk):
    qi = pl.program_id(1)
    hw = 2 * DIFF_HD
    tq = 2 * tk
    ts = tk // 2
    nstrip = tq // ts
    every = slice(0, 2 * tq)
    lane = lax.broadcasted_iota(jnp.int32, (1, hw), 1)

    @pl.when((pl.program_id(0) == 0) & (qi == 0))
    def _():
        acc_s[...] = jnp.zeros_like(acc_s)

    for h in range(DIFF_HEADS):
        for st in range(nstrip):
            q = q_ref[st * ts:(st + 1) * ts, h * hw:(h + 1) * hw]
            zero = jnp.zeros_like(q)
            qq[h, 2 * st * ts:(2 * st + 1) * ts, :] = jnp.where(lane < DIFF_HD, q, zero)
            qq[h, (2 * st + 1) * ts:(2 * st + 2) * ts, :] = jnp.where(lane >= DIFF_HD, q, zero)

    @pl.when(qi == 0)
    def _():
        kn_s[...] = jnp.zeros_like(kn_s)

    new_keys = pl.multiple_of(qi * tq, tq)
    d_row = lax.broadcasted_iota(jnp.int32, (hw, 2 * hw), 0)
    d_col = lax.broadcasted_iota(jnp.int32, (hw, 2 * hw), 1)
    comp_sum = ((d_row < DIFF_HD) == (d_col < hw)).astype(BF16)

    def max_sq_norm(x):
        sq = x * x
        fold = sq[0:ATTN_NORM_ROWS]
        for r in range(ATTN_NORM_ROWS, tq, ATTN_NORM_ROWS):
            fold = jnp.maximum(fold, sq[r:r + ATTN_NORM_ROWS])
        return jnp.max(_dot(fold, comp_sum), axis=0, keepdims=True)

    bound = []
    for h in range(DIFF_HEADS):
        kn = jnp.maximum(kn_s[h:h + 1, :], max_sq_norm(k_ref[pl.ds(new_keys, tq), h * hw:(h + 1) * hw]))
        kn_s[h:h + 1, :] = kn
        qn = max_sq_norm(q_ref[:, h * hw:(h + 1) * hw])
        n2 = qn * kn
        b = n2 * lax.rsqrt(jnp.maximum(n2, 1e-30))
        bound.append((b[:, 0:1], b[:, hw:hw + 1]))

    def scores(start, nkeys, h, rows, diagonal):
        s = _dot_nt(qq[h, rows, :], k_ref[pl.ds(start, nkeys), h * hw:(h + 1) * hw])
        if diagonal:
            row = lax.broadcasted_iota(jnp.int32, s.shape, 0)
            col = lax.broadcasted_iota(jnp.int32, s.shape, 1)
            s = jnp.where((col <= (row & (ts - 1))) | (row >= 2 * ts), s, -1e30)
        return s

    def chunks(nkeys):
        return [slice(c * hw, (c + 1) * hw) for c in range(nkeys // hw)]

    def max_step(start, nkeys, rows, diagonal):
        for h in range(DIFF_HEADS):
            s = scores(start, nkeys, h, rows, diagonal)
            m = m_s[h, rows, :]
            for c in chunks(nkeys):
                m = jnp.maximum(m, s[:, c])
            m_s[h, rows, :] = m

    def shift_bound(h, rows, s):
        return jnp.concatenate([s[i * ts:(i + 1) * ts] - bound[h][i % 2] for i in range(s.shape[0] // ts)], axis=0)

    def shift_row_max(h, rows, s):
        m = m_s[h, rows, :]
        return jnp.concatenate([s[:, c] - m for c in chunks(s.shape[1])], axis=1)

    def acc_step(shift, start, nkeys, rows, diagonal):
        ones = jnp.ones((nkeys, hw), BF16)
        for h in range(DIFF_HEADS):
            p = jnp.exp(shift(h, rows, scores(start, nkeys, h, rows, diagonal))).astype(BF16)
            v_ext = jnp.concatenate([v_ref[pl.ds(start, nkeys), h * hw:(h + 1) * hw], ones], axis=1)
            acc_s[h, rows, :] = acc_s[h, rows, :] + _dot(p, v_ext)

    def loop(step):
        def body(j, c):
            step(pl.multiple_of(j * tk, tk), tk, every, False)
            return c
        lax.fori_loop(0, 2 * qi, body, 0)
        for st in range(nstrip):
            step(pl.multiple_of(qi * tq + st * ts, ts), ts, slice(2 * st * ts, 2 * tq), True)

    loop(functools.partial(acc_step, shift_bound))
    row_sum_min = jnp.min(acc_s[0][:, hw:])
    for h in range(1, DIFF_HEADS):
        row_sum_min = jnp.minimum(row_sum_min, jnp.min(acc_s[h][:, hw:]))

    @pl.when(jnp.logical_not(row_sum_min >= ATTN_MIN_ROW_SUM))
    def _():
        m_s[...] = jnp.full(m_s.shape, -1e30, F32)
        acc_s[...] = jnp.zeros_like(acc_s)
        loop(max_step)
        for h in range(DIFF_HEADS):
            m_s[h] = jnp.broadcast_to(jnp.max(m_s[h], axis=-1, keepdims=True), (2 * tq, hw))
        loop(functools.partial(acc_step, shift_row_max))

    lam = (jnp.exp(jnp.sum(lq1_ref[...] * lk1_ref[...], axis=-1, keepdims=True))
           - jnp.exp(jnp.sum(lq2_ref[...] * lk2_ref[...], axis=-1, keepdims=True)) + lambda_init)
    for h in range(DIFF_HEADS):
        for st in range(nstrip):
            acc = acc_s[h, 2 * st * ts:(2 * st + 2) * ts, :]
            acc_s[h, 2 * st * ts:(2 * st + 2) * ts, :] = jnp.zeros_like(acc)
            o = acc[:, :hw] / acc[:, hw:]
            o = o[:ts] - lam * o[ts:]
            o_ref[st * ts:(st + 1) * ts, h * hw:(h + 1) * hw] = (
                _rms(o, g_ref[...]) * (1.0 - lambda_init)).astype(BF16)


def _diff_attn(q, k, v, lq1, lk1, lq2, lk2, g, lambda_init, B, S):
    tk = ATTN_TK
    tq = 2 * tk
    hw = 2 * DIFF_HD
    qspec = pl.BlockSpec((tq, DIFF_W), lambda b, i: (i, b))
    kspec = pl.BlockSpec((S, DIFF_W), lambda b, i: (0, b))
    small = pl.BlockSpec((1, DIFF_HD), lambda b, i: (0, 0))
    return pl.pallas_call(
        functools.partial(_diff_attn_kernel, lambda_init=lambda_init, tk=tk),
        grid=(B, S // tq),
        in_specs=[qspec, kspec, kspec, small, small, small, small,
                  pl.BlockSpec((1, hw), lambda b, i: (0, 0))],
        out_specs=qspec,
        out_shape=jax.ShapeDtypeStruct((S, B * DIFF_W), BF16),
        scratch_shapes=[pltpu.VMEM((DIFF_HEADS, 2 * tq, hw), BF16), pltpu.VMEM((DIFF_HEADS, 2 * tq, hw), F32),
                        pltpu.VMEM((DIFF_HEADS, 2 * tq, 2 * hw), F32), pltpu.VMEM((DIFF_HEADS, 2 * hw), F32)],
        compiler_params=_params("arbitrary", "arbitrary"),
        name="diff_attn",
    )(q, k, v, lq1, lk1, lq2, lk2, g)


def _merge_xattn_kernel(x_ref, ys_ref, ya_ref, yl_ref, gt_ref, wbs_ref, wba_ref, wbl_ref, wout_ref,
                        g_ref, wq_ref, k_ref, v_ref, wo_ref, o_ref):
    d = D_MODEL
    m = gt_ref[:, 0:d].astype(F32) * _dot(ys_ref[...], wbs_ref[...])
    m = m + gt_ref[:, d:2 * d].astype(F32) * _dot(ya_ref[...], wba_ref[...])
    m = m + gt_ref[:, 2 * d:3 * d].astype(F32) * _dot(yl_ref[...], wbl_ref[...])
    x = x_ref[...] + _dot(m.astype(BF16), wout_ref[...])

    hn = _rms(x, g_ref[...]).astype(BF16)
    q = (_dot(hn, wq_ref[...]) * (XATTN_HD ** -0.5)).astype(BF16)
    outs = []
    for h in range(XATTN_HEADS):
        cols = slice(h * XATTN_HD, (h + 1) * XATTN_HD)
        s = _dot_nt(q[:, cols], k_ref[:, cols])
        p = jnp.exp(s - jnp.max(s, axis=-1, keepdims=True))
        pv = _dot(p.astype(BF16), v_ref[:, cols])
        outs.append((pv / jnp.sum(p, axis=-1, keepdims=True)).astype(BF16))
    o_ref[...] = x + _dot(jnp.concatenate(outs, axis=1), wo_ref[...])


def _merge_xattn(x, batch_major, ys, ya, yl, gt, wbs, wba, wbl, wout, g, wq, kv, wo, mem_len, B, S):
    rows = PROJ_ROW_TILE

    def blk(width):
        return pl.BlockSpec((rows, width), lambda b, i: (i, b))

    return pl.pallas_call(
        _merge_xattn_kernel,
        grid=(B, S // rows),
        in_specs=[_x_spec(batch_major, rows, D_MODEL), blk(SSM_WIDTH), blk(DIFF_W), blk(LRU_WIDTH),
                  blk(3 * D_MODEL), _wspec(wbs), _wspec(wba), _wspec(wbl), _wspec(wout),
                  _whole(g.shape), _wspec(wq),
                  pl.BlockSpec((mem_len, D_MODEL), lambda b, i: (b, 0)),
                  pl.BlockSpec((mem_len, D_MODEL), lambda b, i: (b, 1)),
                  _wspec(wo)],
        out_specs=blk(D_MODEL),
        out_shape=jax.ShapeDtypeStruct((S, B * D_MODEL), F32),
        compiler_params=_params("parallel", "parallel"),
        name="merge_xattn",
    )(x, ys, ya, yl, gt, _wop(wbs), _wop(wba), _wop(wbl), _wop(wout), g, _wop(wq), kv, kv, _wop(wo))


def _memkv_kernel(m_ref, g_ref, w_ref, o_ref):
    hn = _rms(m_ref[...], g_ref[...]).astype(BF16)
    o_ref[...] = _dot(hn, w_ref[...]).astype(BF16)


def _memkv(mem_rows, g, wkv):
    n = mem_rows.shape[0]
    rows = ROW_TILE
    return pl.pallas_call(
        _memkv_kernel,
        grid=(n // rows,),
        in_specs=[pl.BlockSpec((rows, D_MODEL), lambda i: (i, 0)), _whole(g.shape), _wspec(wkv)],
        out_specs=pl.BlockSpec((rows, 2 * D_MODEL), lambda i: (i, 0)),
        out_shape=jax.ShapeDtypeStruct((n, 2 * D_MODEL), BF16),
        compiler_params=_params("parallel"),
        name="mem_kv",
    )(mem_rows, g, _wop(wkv))


def _ffn_kernel(x_ref, swap_ref, g_ref, wup_ref, cw_ref, cb_ref, wdn_ref, gf_ref, o_ref, prev, act_s,
                *, batch, steps, final):
    rows = batch * steps
    d = D_MODEL
    halo = (FFN_CONV - 1) * batch
    n = batch * batch

    @pl.when(pl.program_id(0) == 0)
    def _():
        prev[...] = jnp.zeros_like(prev)

    swap = swap_ref[...]
    g = g_ref[...]
    hn_b = [_rms(x_ref[:, b * d:(b + 1) * d], g).astype(BF16) for b in range(batch)]
    subs = []
    for k in range(steps // batch):
        hb = jnp.concatenate([h[k * batch:(k + 1) * batch] for h in hn_b], axis=0)
        subs.append(_dot(swap, hb).astype(BF16))
    hn = jnp.concatenate(subs, axis=0)

    def conv(up, part, cols):
        hist = prev[part]
        y = cb_ref[:, cols] + cw_ref[FFN_CONV - 1:FFN_CONV, cols] * up
        for t in range(FFN_CONV - 1):
            back = (FFN_CONV - 1 - t) * batch
            shifted = jnp.concatenate([hist[halo - back:], up[:rows - back]], axis=0)
            y = y + cw_ref[t:t + 1, cols] * shifted
        prev[part] = up[rows - halo:]
        return y

    for j in range(FFN_NCHUNK):
        vc = slice(j * FFN_CHUNK, (j + 1) * FFN_CHUNK)
        gc = slice(D_FF + j * FFN_CHUNK, D_FF + (j + 1) * FFN_CHUNK)
        val = conv(_dot(hn, wup_ref[:, vc]), 2 * j, vc)
        gate = conv(_dot(hn, wup_ref[:, gc]), 2 * j + 1, gc)
        act_s[:, vc] = (gate * _sigmoid(gate) * val).astype(BF16)
    acc = _dot(act_s[...], wdn_ref[...])

    for k in range(steps // batch):
        yb = jnp.swapaxes(acc[k * n:(k + 1) * n].reshape(batch, batch, d), 0, 1).reshape(n, d)
        ts = slice(k * batch, (k + 1) * batch)
        for b in range(batch):
            out = x_ref[ts, b * d:(b + 1) * d] + yb[b * batch:(b + 1) * batch]
            if final:
                o_ref[b, ts, :] = _rms(out, gf_ref[...])
            else:
                o_ref[ts, b * d:(b + 1) * d] = out


def _ffn(x, swap, g, wup, cw, cb, wdn, gf, B, S, final):
    rows = FFN_ROW_TILE
    steps = rows // B
    blk = pl.BlockSpec((steps, B * D_MODEL), lambda i: (i, 0))
    if final:
        out_spec = pl.BlockSpec((B, steps, D_MODEL), lambda i: (0, i, 0))
        out_shape = jax.ShapeDtypeStruct((B, S, D_MODEL), F32)
    else:
        out_spec, out_shape = blk, jax.ShapeDtypeStruct((S, B * D_MODEL), F32)
    return pl.pallas_call(
        functools.partial(_ffn_kernel, batch=B, steps=steps, final=final),
        grid=(S // steps,),
        in_specs=[blk, _whole(swap.shape), _whole(g.shape), _wspec(wup), _whole(cw.shape), _whole(cb.shape),
                  _wspec(wdn), _whole(gf.shape)],
        out_specs=out_spec,
        out_shape=out_shape,
        scratch_shapes=[pltpu.VMEM((2 * FFN_NCHUNK, (FFN_CONV - 1) * B, FFN_CHUNK), F32),
                        pltpu.VMEM((rows, D_FF), BF16)],
        compiler_params=_params("arbitrary"),
        name="conv_ffn",
    )(x, swap, g, _wop(wup), cw, cb, _wop(wdn), gf)


def _row(v):
    return v.reshape(1, -1).astype(F32)


def _block_diag(blocks):
    n, r, c = blocks.shape
    eye = jnp.eye(n, dtype=blocks.dtype)
    return (blocks[:, :, None, :] * eye[:, None, :, None]).reshape(n * r, n * c)


def _s5_b_weights(bbr, bbi):
    per = SSM_GROUPS // SSM_LANE_BLOCKS
    out = []
    for j in range(SSM_LANE_BLOCKS):
        r = bbr[j * LANES:(j + 1) * LANES].reshape(per, SSM_GROUP, SSM_STATE)
        i = bbi[j * LANES:(j + 1) * LANES].reshape(per, SSM_GROUP, SSM_STATE)
        out.append(jnp.concatenate([_block_diag(r), _block_diag(i)], axis=1))
    return jnp.stack(out).astype(BF16)


def _s5_c_weights(c):
    per = SSM_GROUPS // SSM_LANE_BLOCKS
    ct = jnp.swapaxes(c, 1, 2)
    return jnp.stack([_block_diag(ct[j * per:(j + 1) * per]) for j in range(SSM_LANE_BLOCKS)]).astype(BF16)


def kernel(x, mem, positions, norm_mix_g, w_in, ssm_lambda_re, ssm_lambda_im, ssm_log_step, ssm_b_re, ssm_b_im, ssm_c_re, ssm_c_im, ssm_d, ssm_w_glu, ssm_b_glu, diff_lq1, diff_lk1, diff_lq2, diff_lk2, diff_subln_g, lru_conv_w, lru_conv_b, lru_wa, lru_ba, lru_wx, lru_bx, lru_lambda, w_br_ssm, w_br_attn, w_br_lru, w_out, norm_xattn_g, norm_mem_g, xattn_wq, xattn_wkv, xattn_wo, norm_ffn_g, ffn_w_up, ffn_conv_w, ffn_conv_b, ffn_w_down, final_norm_g):
    B, S, _ = x.shape
    depth = norm_mix_g.shape[0]
    mem_len = mem.shape[1]
    assert x.shape[-1] == D_MODEL and w_in.shape[1:] == (D_MODEL, D_IN)
    assert S % ROW_TILE == 0 and ROW_TILE % B == 0 and B % 8 == 0 and (B * mem_len) % ROW_TILE == 0
    assert S % (2 * ATTN_TK) == 0 and FFN_ROW_TILE % (B * B) == 0 and (S * B) % FFN_ROW_TILE == 0
    assert REC_ROW_TILE % (B * B) == 0 and (S * B) % REC_ROW_TILE == 0 and S % PROJ_ROW_TILE == 0

    cos, sin = _rope_tables(positions.astype(F32)[..., None], B, S)

    rep = lambda a: jnp.repeat(a, SSM_GROUP, axis=1)
    b_t = lambda a: jnp.swapaxes(a, 2, 3).reshape(depth, SSM_WIDTH, SSM_STATE)
    abr, abi, bbr, bbi = _s5_prep(rep(ssm_lambda_re), rep(ssm_lambda_im), rep(ssm_log_step[..., None]),
                                  b_t(ssm_b_re), b_t(ssm_b_im))
    mem_rows = mem.reshape(B * mem_len, D_MODEL)
    swap = _swap_matrix(B)
    (w_in, ssm_w_glu, w_br_ssm, w_br_attn, w_br_lru, w_out, xattn_wq, xattn_wkv, xattn_wo, ffn_w_up,
     ffn_w_down) = (a.astype(BF16) for a in (w_in, ssm_w_glu, w_br_ssm, w_br_attn, w_br_lru, w_out, xattn_wq,
                                             xattn_wkv, xattn_wo, ffn_w_up, ffn_w_down))

    xs = x
    batch_major = True
    for l in range(depth):
        lambda_init = 0.8 - 0.6 * math.exp(-0.3 * l)
        u, q, k, v, xr, gr, gt = _inproj(xs, batch_major, _row(norm_mix_g[l]), _Layer(w_in, l), cos, sin, B, S)

        s5_params = (_s5_b_weights(bbr[l], bbi[l]),
                     abr[l, ::SSM_GROUP].reshape(1, SSM_NSTATE), abi[l, ::SSM_GROUP].reshape(1, SSM_NSTATE),
                     _s5_c_weights(ssm_c_re[l]), _s5_c_weights(ssm_c_im[l]), _row(ssm_d[l]),
                     _Layer(ssm_w_glu, l), _row(ssm_b_glu[l]))
        nh = lru_wa.shape[1] // 2
        wax = jnp.stack([_block_diag(w[l, i * nh:(i + 1) * nh]) for w in (lru_wa, lru_wx)
                         for i in range(2)]).astype(BF16)
        lru_params = (lru_conv_w[l].astype(F32), _row(lru_conv_b[l]), wax, _row(lru_ba[l]), _row(lru_bx[l]),
                      _row(lru_lambda[l]))
        y_ssm, y_lru = _recurrent(u, xr, gr, swap, s5_params, lru_params, B)

        y_att = _diff_attn(q, k, v, _row(diff_lq1[l]), _row(diff_lk1[l]), _row(diff_lq2[l]), _row(diff_lk2[l]),
                           _row(diff_subln_g[l]), lambda_init, B, S)

        kv = _memkv(mem_rows, _row(norm_mem_g[l]), _Layer(xattn_wkv, l))
        xs = _merge_xattn(xs, batch_major, y_ssm, y_att, y_lru, gt, _Layer(w_br_ssm, l), _Layer(w_br_attn, l),
                          _Layer(w_br_lru, l), _Layer(w_out, l), _row(norm_xattn_g[l]), _Layer(xattn_wq, l), kv,
                          _Layer(xattn_wo, l), mem_len, B, S)
        batch_major = False

        xs = _ffn(xs, swap, _row(norm_ffn_g[l]), _Layer(ffn_w_up, l), ffn_conv_w[l].astype(F32),
                  _row(ffn_conv_b[l]), _Layer(ffn_w_down, l), _row(final_norm_g), B, S, final=l == depth - 1)

    return xs
```
